```python
import math
import jax, jax.numpy as jnp
from jax import lax
import numpy as np

D_MODEL = 1024
BATCH = 32
SEQ = 256
DEPTH = 1
DEC_BATCH = 4
DEC_SEQ = 2048
PAST_LEN = 512

GRID_W = 64
N_HEADS = 4
HEAD_DIM = 64
V_DIM = 2 * HEAD_DIM
ATTN_WIDTH = N_HEADS * V_DIM
QK_WIDTH = N_HEADS * 2 * HEAD_DIM
N_FOUR_GROUPS = 4
FOUR_GROUP = 128
FOUR_WIDTH = N_FOUR_GROUPS * FOUR_GROUP
PROJ_WIDTH = 2 * QK_WIDTH + ATTN_WIDTH + FOUR_WIDTH
MIX_WIDTH = ATTN_WIDTH + FOUR_WIDTH
D_FF = 4 * D_MODEL
ROPE_BASE = 10000.0
ROPE_AXIS_DIM = HEAD_DIM // 2
Q_BLOCK = 128
EPS = 1e-6

kernel_name = "hybrid_diffattn_fnet_prefix_dit_step"


def rms_norm(x, g):
    xf = x.astype(jnp.float32)
    y = xf * lax.rsqrt(jnp.mean(xf * xf, axis=-1, keepdims=True) + EPS)
    return (y * g.astype(jnp.float32)).astype(x.dtype)


def adaln(cvec, w_mod, b_mod):
    m = jax.nn.silu(cvec) @ w_mod + b_mod
    return jnp.split(m[:, None, :], 6, axis=-1)


def rope_tables(n):
    rows = n // GRID_W
    row = jnp.broadcast_to(jnp.arange(rows)[:, None], (rows, GRID_W)).reshape(n).astype(jnp.float32)
    col = jnp.broadcast_to(jnp.arange(GRID_W)[None, :], (rows, GRID_W)).reshape(n).astype(jnp.float32)
    inv = ROPE_BASE ** (-jnp.arange(0, ROPE_AXIS_DIM, 2, dtype=jnp.float32) / ROPE_AXIS_DIM)
    ang_r = (row[:, None] * inv)[:, None, None, :]
    ang_c = (col[:, None] * inv)[:, None, None, :]
    return jnp.cos(ang_r), jnp.sin(ang_r), jnp.cos(ang_c), jnp.sin(ang_c)


def rotate_half_pairs(x, cos, sin):
    x1, x2 = jnp.split(x, 2, axis=-1)
    cos = cos.astype(x.dtype)
    sin = sin.astype(x.dtype)
    return jnp.concatenate([x1 * cos - x2 * sin, x2 * cos + x1 * sin], axis=-1)


def apply_axial_rope(x, tables):
    cos_r, sin_r, cos_c, sin_c = tables
    xr, xc = jnp.split(x, 2, axis=-1)
    return jnp.concatenate([rotate_half_pairs(xr, cos_r, sin_r), rotate_half_pairs(xc, cos_c, sin_c)], axis=-1)


def project(h, w_in, q_g, k_g):
    b, n, _ = h.shape
    p = h @ w_in
    q, k, v, f = jnp.split(p, [QK_WIDTH, 2 * QK_WIDTH, 2 * QK_WIDTH + ATTN_WIDTH], axis=-1)
    q = rms_norm(q.reshape(b, n, N_HEADS, 2, HEAD_DIM), q_g)
    k = rms_norm(k.reshape(b, n, N_HEADS, 2, HEAD_DIM), k_g)
    v = v.reshape(b, n, N_HEADS, V_DIM)
    f = f.reshape(b, n, N_FOUR_GROUPS, FOUR_GROUP)
    return q, k, v, f


def diff_lambda(lq1, lk1, lq2, lk2, lam_init):
    l1 = jnp.exp(jnp.sum(lq1.astype(jnp.float32) * lk1.astype(jnp.float32)))
    l2 = jnp.exp(jnp.sum(lq2.astype(jnp.float32) * lk2.astype(jnp.float32)))
    return l1 - l2 + lam_init


def diff_attention(q, k, v, lam):
    b, nq, h, _, dh = q.shape
    nblk = nq // Q_BLOCK
    scale = 1.0 / math.sqrt(dh)
    qb = q.reshape(b, nblk, Q_BLOCK, h, 2, dh).transpose(1, 0, 2, 3, 4, 5)

    def one_block(qblk):
        s = jnp.einsum('bqhmd,bkhmd->bhmqk', qblk, k).astype(jnp.float32) * scale
        p = jax.nn.softmax(s, axis=-1)
        a = p[:, :, 0] - lam * p[:, :, 1]
        return jnp.einsum('bhqk,bkhe->bqhe', a.astype(v.dtype), v)

    o = lax.map(one_block, qb)
    return o.transpose(1, 0, 2, 3, 4).reshape(b, nq, h, V_DIM)


def fourier_mix(f, w_four):
    b, n, _, _ = f.shape
    spec = jnp.fft.fft2(f.astype(jnp.float32), axes=(1, 3), norm='ortho').real.astype(f.dtype)
    return jnp.einsum('bngc,gce->bnge', spec, w_four).reshape(b, n, FOUR_WIDTH)


def merge(attn, four, subln_g, lam_init, w_out):
    b, n = attn.shape[:2]
    a = rms_norm(attn, subln_g) * (1.0 - lam_init)
    return jnp.concatenate([a.reshape(b, n, ATTN_WIDTH), four], axis=-1) @ w_out


def sq_relu_mlp(h, w1, w2):
    return jnp.square(jax.nn.relu(h @ w1)) @ w2


def setup_inputs(seed: int = 0) -> dict:
    key = jax.random.key(seed)
    ks = jax.random.split(key, 24)
    f32 = jnp.float32
    nrm = lambda k, shape, s: (jax.random.normal(k, shape, f32) * s)
    return {
        'x_prompt': nrm(ks[0], (BATCH, SEQ, D_MODEL), 1.0),
        'x_sample': nrm(ks[1], (DEC_BATCH, DEC_SEQ, D_MODEL), 1.0),
        'c': nrm(ks[2], (DEC_BATCH, D_MODEL), 1.0),
        'cache_k': nrm(ks[3], (DEC_BATCH, DEPTH, PAST_LEN, N_HEADS, 2, HEAD_DIM), 1.0),
        'cache_v': nrm(ks[4], (DEC_BATCH, DEPTH, PAST_LEN, N_HEADS, V_DIM), 1.0),
        'c_ctx': nrm(ks[5], (D_MODEL,), 1.0),
        'w_mod': nrm(ks[6], (DEPTH, D_MODEL, 6 * D_MODEL), 0.5 * D_MODEL ** -0.5),
        'b_mod': nrm(ks[7], (DEPTH, 6 * D_MODEL), 0.02),
        'norm1_g': 1.0 + nrm(ks[8], (DEPTH, D_MODEL), 0.02),
        'w_in': nrm(ks[9], (DEPTH, D_MODEL, PROJ_WIDTH), D_MODEL ** -0.5),
        'q_norm_g': 1.0 + nrm(ks[10], (DEPTH, HEAD_DIM), 0.02),
        'k_norm_g': 1.0 + nrm(ks[11], (DEPTH, HEAD_DIM), 0.02),
        'lambda_q1': nrm(ks[12], (DEPTH, HEAD_DIM), 0.1),
        'lambda_k1': nrm(ks[13], (DEPTH, HEAD_DIM), 0.1),
        'lambda_q2': nrm(ks[14], (DEPTH, HEAD_DIM), 0.1),
        'lambda_k2': nrm(ks[15], (DEPTH, HEAD_DIM), 0.1),
        'subln_g': 1.0 + nrm(ks[16], (DEPTH, V_DIM), 0.02),
        'w_four': nrm(ks[17], (DEPTH, N_FOUR_GROUPS, FOUR_GROUP, FOUR_GROUP), FOUR_GROUP ** -0.5),
        'w_out': nrm(ks[18], (DEPTH, MIX_WIDTH, D_MODEL), MIX_WIDTH ** -0.5),
        'norm2_g': 1.0 + nrm(ks[19], (DEPTH, D_MODEL), 0.02),
        'w1': nrm(ks[20], (DEPTH, D_MODEL, D_FF), D_MODEL ** -0.5),
        'w2': nrm(ks[21], (DEPTH, D_FF, D_MODEL), D_FF ** -0.5),
    }


def reference(x_prompt, x_sample, c, cache_k, cache_v, c_ctx, w_mod, b_mod, norm1_g, w_in,
              q_norm_g, k_norm_g, lambda_q1, lambda_k1, lambda_q2, lambda_k2, subln_g,
              w_four, w_out, norm2_g, w1, w2):
    xp = x_prompt
    xs = x_sample
    tables = rope_tables(xs.shape[1])
    new_k, new_v = [], []
    for l in range(DEPTH):
        lam_init = 0.8 - 0.6 * math.exp(-0.3 * l)
        lam = diff_lambda(lambda_q1[l], lambda_k1[l], lambda_q2[l], lambda_k2[l], lam_init)

        sh1, sc1, g1, sh2, sc2, g2 = adaln(c_ctx[None, :], w_mod[l], b_mod[l])
        h = rms_norm(xp, norm1_g[l]) * (1 + sc1) + sh1
        q, k, v, f = project(h, w_in[l], q_norm_g[l], k_norm_g[l])
        attn = diff_attention(q, k, v, lam)
        mix = merge(attn, fourier_mix(f, w_four[l]), subln_g[l], lam_init, w_out[l])
        xp = xp + g1 * mix
        h = rms_norm(xp, norm2_g[l]) * (1 + sc2) + sh2
        xp = xp + g2 * sq_relu_mlp(h, w1[l], w2[l])
        new_k.append(k)
        new_v.append(v)

        sh1, sc1, g1, sh2, sc2, g2 = adaln(c, w_mod[l], b_mod[l])
        h = rms_norm(xs, norm1_g[l]) * (1 + sc1) + sh1
        q, k, v, f = project(h, w_in[l], q_norm_g[l], k_norm_g[l])
        q = apply_axial_rope(q, tables)
        k = apply_axial_rope(k, tables)
        k_all = jnp.concatenate([cache_k[:, l].astype(k.dtype), k], axis=1)
        v_all = jnp.concatenate([cache_v[:, l].astype(v.dtype), v], axis=1)
        attn = diff_attention(q, k_all, v_all, lam)
        mix = merge(attn, fourier_mix(f, w_four[l]), subln_g[l], lam_init, w_out[l])
        xs = xs + g1 * mix
        h = rms_norm(xs, norm2_g[l]) * (1 + sc2) + sh2
        xs = xs + g2 * sq_relu_mlp(h, w1[l], w2[l])

    new_cache_k = jnp.stack(new_k, axis=1)
    new_cache_v = jnp.stack(new_v, axis=1)
    return (xp, xs, new_cache_k, new_cache_v)
```

```python
import functools
import math

import jax
import jax.numpy as jnp
import numpy as np
from jax import lax
from jax.experimental import pallas as pl
from jax.experimental.pallas import tpu as pltpu

D_MODEL = 1024
N_HEADS = 4
HEAD_DIM = 64
V_DIM = 2 * HEAD_DIM
QK_WIDTH = N_HEADS * 2 * HEAD_DIM
ATTN_WIDTH = N_HEADS * V_DIM
N_FOUR_GROUPS = 4
FOUR_GROUP = 128
FOUR_WIDTH = N_FOUR_GROUPS * FOUR_GROUP
PROJ_WIDTH = 2 * QK_WIDTH + ATTN_WIDTH + FOUR_WIDTH
D_FF = 4 * D_MODEL
GRID_W = 64
ROPE_BASE = 10000.0
ROPE_AXIS_DIM = HEAD_DIM // 2
EPS = 1e-6
LAM_INIT = 0.8 - 0.6 * math.exp(-0.3 * 0)
MOD_ROWS = 8

F32 = jnp.float32
BF16 = jnp.bfloat16

VMEM_LIMIT_BYTES = 56 * 1024 * 1024


def _params(n_axes):
    return pltpu.CompilerParams(dimension_semantics=("arbitrary",) * n_axes,
                                vmem_limit_bytes=VMEM_LIMIT_BYTES)


def _resident(shape):
    nd = len(shape)
    return pl.BlockSpec(shape, lambda *_: (0,) * nd, pipeline_mode=pl.Buffered(1))


@functools.lru_cache(maxsize=None)
def _channel_dft():
    idx = np.arange(FOUR_GROUP)
    ang = 2.0 * np.pi * ((idx[:, None] * idx[None, :]) % FOUR_GROUP) / FOUR_GROUP
    s = 1.0 / np.sqrt(FOUR_GROUP)
    return np.stack([np.cos(ang) * s, np.sin(ang) * s]).astype(np.float32)


@functools.lru_cache(maxsize=None)
def _position_dft(n):
    idx = np.arange(n)
    ang = 2.0 * np.pi * ((idx[:, None] * idx[None, :]) % n) / n
    s = 1.0 / np.sqrt(n)
    return (np.cos(ang) * s).astype(np.float32), (-np.sin(ang) * s).astype(np.float32)


@functools.lru_cache(maxsize=None)
def _rope_tables(n):
    rows = n // GRID_W
    row = np.repeat(np.arange(rows), GRID_W).astype(np.float64)
    col = np.tile(np.arange(GRID_W), rows).astype(np.float64)
    inv = ROPE_BASE ** (-np.arange(0, ROPE_AXIS_DIM, 2, dtype=np.float64) / ROPE_AXIS_DIM)
    d = np.arange(QK_WIDTH) % HEAD_DIM
    part = d // ROPE_AXIS_DIM
    i = d % ROPE_AXIS_DIM
    first = i < ROPE_AXIS_DIM // 2
    pos = np.where(part[None, :] == 0, row[:, None], col[:, None])
    ang = pos * inv[i % (ROPE_AXIS_DIM // 2)][None, :]
    cos = np.cos(ang)
    sin = np.where(first[None, :], -np.sin(ang), np.sin(ang))
    return cos.astype(np.float32), sin.astype(np.float32)


@functools.lru_cache(maxsize=None)
def _head_mean_matrix():
    g = np.kron(np.eye(QK_WIDTH // HEAD_DIM), np.ones((HEAD_DIM, HEAD_DIM))) / HEAD_DIM
    return g.astype(np.float32)


MOD_TILE = 1024


def _mod_kernel(c_ref, w_ref, b_ref, o_ref):
    c = c_ref[...]
    s = c / (1.0 + jnp.exp(-c))
    o_ref[...] = jnp.dot(s.astype(BF16), w_ref[...].astype(BF16),
                         preferred_element_type=F32) + b_ref[...]


def _modulation(cvec, w_mod, b_mod):
    n = w_mod.shape[1]
    return pl.pallas_call(
        _mod_kernel,
        grid=(n // MOD_TILE,),
        in_specs=[pl.BlockSpec((MOD_ROWS, D_MODEL), lambda j: (0, 0)),
                  pl.BlockSpec((D_MODEL, MOD_TILE), lambda j: (0, j)),
                  pl.BlockSpec((1, MOD_TILE), lambda j: (0, j))],
        out_specs=pl.BlockSpec((MOD_ROWS, MOD_TILE), lambda j: (0, j)),
        out_shape=jax.ShapeDtypeStruct((MOD_ROWS, n), F32),
        compiler_params=_params(1),
        name="modulation",
    )(cvec, w_mod, b_mod)


def _wcs_kernel(dft_ref, w_ref, o_ref):
    for g in range(N_FOUR_GROUPS):
        w = w_ref[g]
        wc = jnp.dot(dft_ref[0], w, precision=lax.Precision.HIGHEST, preferred_element_type=F32)
        ws = jnp.dot(dft_ref[1], w, precision=lax.Precision.HIGHEST, preferred_element_type=F32)
        o_ref[g] = jnp.concatenate([wc, ws], axis=-1).astype(BF16)


def _fourier_weights(w_four):
    dft = jnp.asarray(_channel_dft())
    return pl.pallas_call(
        _wcs_kernel,
        out_shape=jax.ShapeDtypeStruct((N_FOUR_GROUPS, FOUR_GROUP, 2 * FOUR_GROUP), BF16),
        name="fourier_weights",
    )(dft, w_four)


PROJ_TILE = 512


def _inproj_kernel(*refs, rope, emit_cache):
    it = iter(refs)
    x_ref, mod_ref, g1_ref, w_ref, qg_ref, kg_ref, gm_ref, wcs_ref = (next(it) for _ in range(8))
    if rope:
        cos_ref, sin_ref = next(it), next(it)
    q_ref, k_ref, v_ref, uc_ref, us_ref = (next(it) for _ in range(5))
    if emit_cache:
        k32_ref, v32_ref = next(it), next(it)

    x = x_ref[0]
    m = mod_ref[0]
    sh1 = m[:, 0:D_MODEL]
    sc1 = m[:, D_MODEL:2 * D_MODEL]
    ms = jnp.mean(x * x, axis=-1, keepdims=True)
    y = x * lax.rsqrt(ms + EPS) * g1_ref[...]
    h = y * (1.0 + sc1) + sh1
    p = jnp.dot(h.astype(BF16), w_ref[...], preferred_element_type=F32)
    q = p[:, 0:QK_WIDTH]
    k = p[:, QK_WIDTH:2 * QK_WIDTH]
    v = p[:, 2 * QK_WIDTH:2 * QK_WIDTH + ATTN_WIDTH]

    def head_norm(t, g):
        msq = jnp.dot((t * t).astype(BF16), gm_ref[...], preferred_element_type=F32)
        return t * lax.rsqrt(msq + EPS) * g

    q = head_norm(q, qg_ref[...])
    k = head_norm(k, kg_ref[...])
    if emit_cache:
        k32_ref[0] = k
        v32_ref[0] = v

    if rope:
        lane = lax.broadcasted_iota(jnp.int32, q.shape, 1)
        first = (lane % ROPE_AXIS_DIM) < (ROPE_AXIS_DIM // 2)
        half = ROPE_AXIS_DIM // 2

        def rot(t):
            sw = jnp.where(first, pltpu.roll(t, QK_WIDTH - half, 1), pltpu.roll(t, half, 1))
            return t * cos_ref[...] + sw * sin_ref[...]

        q = rot(q)
        k = rot(k)

    q_ref[0] = (q * (1.0 / math.sqrt(HEAD_DIM))).astype(BF16)
    k_ref[0] = k.astype(BF16)
    v_ref[0] = v.astype(BF16)

    f0 = 2 * QK_WIDTH + ATTN_WIDTH
    for g in range(N_FOUR_GROUPS):
        fg = p[:, f0 + g * FOUR_GROUP:f0 + (g + 1) * FOUR_GROUP].astype(BF16)
        u = jnp.dot(fg, wcs_ref[g], preferred_element_type=F32)
        uc_ref[0, :, g * FOUR_GROUP:(g + 1) * FOUR_GROUP] = u[:, 0:FOUR_GROUP].astype(BF16)
        us_ref[0, :, g * FOUR_GROUP:(g + 1) * FOUR_GROUP] = u[:, FOUR_GROUP:].astype(BF16)


def _in_projection(x3, mod3, mod_row0, norm_g, w_in, qg, kg, gm, wcs, rope_tabs, emit_cache):
    nb, n, _ = x3.shape
    tm = PROJ_TILE
    rope = rope_tabs is not None
    tile = lambda width: pl.BlockSpec((1, tm, width), lambda t, b: (b, t, 0))
    in_specs = [tile(D_MODEL),
                pl.BlockSpec((1, 1, 6 * D_MODEL), lambda t, b: (mod_row0 + b, 0, 0)),
                _resident((1, D_MODEL)),
                _resident((D_MODEL, PROJ_WIDTH)),
                _resident((1, QK_WIDTH)),
                _resident((1, QK_WIDTH)),
                _resident((QK_WIDTH, QK_WIDTH)),
                _resident((N_FOUR_GROUPS, FOUR_GROUP, 2 * FOUR_GROUP))]
    args = [x3, mod3, norm_g, w_in, qg, kg, gm, wcs]
    if rope:
        in_specs += [pl.BlockSpec((tm, QK_WIDTH), lambda t, b: (t, 0))] * 2
        args += list(rope_tabs)
    n_bf = 5
    out_shape = [jax.ShapeDtypeStruct((nb, n, QK_WIDTH), BF16)] * n_bf
    out_specs = [tile(QK_WIDTH)] * n_bf
    if emit_cache:
        out_shape += [jax.ShapeDtypeStruct((nb, n, QK_WIDTH), F32)] * 2
        out_specs += [tile(QK_WIDTH)] * 2
    return pl.pallas_call(
        functools.partial(_inproj_kernel, rope=rope, emit_cache=emit_cache),
        grid=(n // tm, nb),
        in_specs=in_specs,
        out_specs=out_specs,
        out_shape=out_shape,
        compiler_params=_params(2),
        name="in_projection_rope" if rope else "in_projection",
    )(*args)


ATTN_Q_TILE = 256


def _attn_kernel(*refs, has_cache):
    it = iter(refs)
    lam_ref, sg_ref, q_ref, k_ref, v_ref = (next(it) for _ in range(5))
    if has_cache:
        ck_ref, cv_ref = next(it), next(it)
    o_ref = next(it)

    lv = lam_ref[...]
    l1 = jnp.exp(jnp.sum(lv[0:1] * lv[1:2], axis=-1, keepdims=True))
    l2 = jnp.exp(jnp.sum(lv[2:3] * lv[3:4], axis=-1, keepdims=True))
    lam = l1 - l2 + LAM_INIT

    nt = (((1,), (1,)), ((), ()))
    lane = lax.broadcasted_iota(jnp.int32, (1, V_DIM), 1)
    half0 = lane < HEAD_DIM
    zero = jnp.zeros((), BF16)

    for h in range(N_HEADS):
        sl = slice(h * V_DIM, (h + 1) * V_DIM)
        qh = q_ref[0, :, sl]
        kh = k_ref[0, :, sl]
        vh = v_ref[0, :, sl]
        if has_cache:
            ckh = ck_ref[0, :, sl].astype(BF16)
            cvh = cv_ref[0, :, sl].astype(BF16)
        probs = []
        for mhalf in range(2):
            qm = jnp.where(half0 if mhalf == 0 else jnp.logical_not(half0), qh, zero)
            s = lax.dot_general(qm, kh, nt, preferred_element_type=F32)
            mx = jnp.max(s, axis=-1, keepdims=True)
            if has_cache:
                sc = lax.dot_general(qm, ckh, nt, preferred_element_type=F32)
                mx = jnp.maximum(mx, jnp.max(sc, axis=-1, keepdims=True))
            e = jnp.exp(s - mx)
            den = jnp.sum(e, axis=-1, keepdims=True)
            if has_cache:
                ec = jnp.exp(sc - mx)
                den = den + jnp.sum(ec, axis=-1, keepdims=True)
            else:
                ec = None
            probs.append((e, ec, den))
        (e0, ec0, d0), (e1, ec1, d1) = probs
        r0 = 1.0 / d0
        r1 = lam / d1
        a = (e0 * r0 - e1 * r1).astype(BF16)
        o = jnp.dot(a, vh, preferred_element_type=F32)
        if has_cache:
            ac = (ec0 * r0 - ec1 * r1).astype(BF16)
            o = o + jnp.dot(ac, cvh, preferred_element_type=F32)
        ms = jnp.mean(o * o, axis=-1, keepdims=True)
        o = o * lax.rsqrt(ms + EPS) * sg_ref[...] * (1.0 - LAM_INIT)
        o_ref[0, :, sl] = o.astype(BF16)


def _attention(lamv, subln_g, q, k, v, cache_k=None, cache_v=None):
    nb, n, _ = q.shape
    tq = min(ATTN_Q_TILE, n)
    has_cache = cache_k is not None
    qtile = pl.BlockSpec((1, tq, QK_WIDTH), lambda b, t: (b, t, 0))
    per_batch = lambda rows: pl.BlockSpec((1, rows, QK_WIDTH), lambda b, t: (b, 0, 0))
    in_specs = [_resident((4, HEAD_DIM)), _resident((1, V_DIM)), qtile, per_batch(n), per_batch(n)]
    args = [lamv, subln_g, q, k, v]
    if has_cache:
        past = cache_k.shape[1]
        in_specs += [per_batch(past), per_batch(past)]
        args += [cache_k, cache_v]
    return pl.pallas_call(
        functools.partial(_attn_kernel, has_cache=has_cache),
        grid=(nb, n // tq),
        in_specs=in_specs,
        out_specs=qtile,
        out_shape=jax.ShapeDtypeStruct((nb, n, ATTN_WIDTH), BF16),
        compiler_params=_params(2),
        name="diff_attention_cached" if has_cache else "diff_attention",
    )(*args)


def _posdft_kernel(pc_ref, ps_ref, uc_ref, us_ref, o_ref):
    o = jnp.dot(pc_ref[...], uc_ref[0], preferred_element_type=F32)
    o = o + jnp.dot(ps_ref[...], us_ref[0], preferred_element_type=F32)
    o_ref[0] = o.astype(BF16)


def _position_mix(uc, us):
    nb, n, _ = uc.shape
    tr = min(512, n)
    pc, ps = _position_dft(n)
    pc = jnp.asarray(pc).astype(BF16)
    ps = jnp.asarray(ps).astype(BF16)
    rows = pl.BlockSpec((tr, n), lambda r, b: (r, 0))
    per_batch = pl.BlockSpec((1, n, FOUR_WIDTH), lambda r, b: (b, 0, 0))
    return pl.pallas_call(
        _posdft_kernel,
        grid=(n // tr, nb),
        in_specs=[rows, rows, per_batch, per_batch],
        out_specs=pl.BlockSpec((1, tr, FOUR_WIDTH), lambda r, b: (b, r, 0)),
        out_shape=jax.ShapeDtypeStruct((nb, n, FOUR_WIDTH), BF16),
        compiler_params=_params(2),
        name="position_dft",
    )(pc, ps, uc, us)


OUT_TILE = 512
FF_CHUNK = 1024


def _out_mlp_kernel(x_ref, a_ref, f_ref, mod_ref, wo_ref, g2_ref, w1_ref, w2_ref, o_ref):
    m = mod_ref[0]
    g1 = m[:, 2 * D_MODEL:3 * D_MODEL]
    sh2 = m[:, 3 * D_MODEL:4 * D_MODEL]
    sc2 = m[:, 4 * D_MODEL:5 * D_MODEL]
    g2 = m[:, 5 * D_MODEL:6 * D_MODEL]
    mix = jnp.dot(a_ref[0], wo_ref[0:ATTN_WIDTH, :], preferred_element_type=F32)
    mix = mix + jnp.dot(f_ref[0], wo_ref[ATTN_WIDTH:, :], preferred_element_type=F32)
    x1 = x_ref[0] + g1 * mix
    ms = jnp.mean(x1 * x1, axis=-1, keepdims=True)
    h = x1 * lax.rsqrt(ms + EPS) * g2_ref[...] * (1.0 + sc2) + sh2
    hb = h.astype(BF16)
    acc = jnp.zeros(x1.shape, F32)
    for c in range(D_FF // FF_CHUNK):
        t = jnp.dot(hb, w1_ref[:, c * FF_CHUNK:(c + 1) * FF_CHUNK], preferred_element_type=F32)
        t = jnp.square(jnp.maximum(t, 0.0)).astype(BF16)
        acc = acc + jnp.dot(t, w2_ref[c * FF_CHUNK:(c + 1) * FF_CHUNK, :], preferred_element_type=F32)
    o_ref[0] = x1 + g2 * acc


def _output_mlp(x3, attn, four, mod3, mod_row0, w_out, norm2_g, w1, w2):
    nb, n, _ = x3.shape
    tm = OUT_TILE
    tile = lambda width: pl.BlockSpec((1, tm, width), lambda t, b: (b, t, 0))
    return pl.pallas_call(
        _out_mlp_kernel,
        grid=(n // tm, nb),
        in_specs=[tile(D_MODEL), tile(ATTN_WIDTH), tile(FOUR_WIDTH),
                  pl.BlockSpec((1, 1, 6 * D_MODEL), lambda t, b: (mod_row0 + b, 0, 0)),
                  _resident((ATTN_WIDTH + FOUR_WIDTH, D_MODEL)),
                  _resident((1, D_MODEL)),
                  _resident((D_MODEL, D_FF)),
                  _resident((D_FF, D_MODEL))],
        out_specs=tile(D_MODEL),
        out_shape=jax.ShapeDtypeStruct((nb, n, D_MODEL), F32),
        compiler_params=_params(2),
        name="output_mlp",
    )(x3, attn, four, mod3, w_out, norm2_g, w1, w2)


def kernel(x_prompt, x_sample, c, cache_k, cache_v, c_ctx, w_mod, b_mod, norm1_g, w_in, q_norm_g, k_norm_g,
           lambda_q1, lambda_k1, lambda_q2, lambda_k2, subln_g, w_four, w_out, norm2_g, w1, w2):
    batch, seq, _ = x_prompt.shape
    dec_batch, dec_seq, _ = x_sample.shape
    past = cache_k.shape[2]
    l = 0

    cvec = jnp.concatenate([c_ctx[None, :], c, jnp.zeros((MOD_ROWS - 1 - dec_batch, D_MODEL), F32)], axis=0)
    mod3 = _modulation(cvec, w_mod[l], b_mod[l][None, :]).reshape(MOD_ROWS, 1, 6 * D_MODEL)
    wcs = _fourier_weights(w_four[l])

    w_in_b = w_in[l].astype(BF16)
    w_out_b = w_out[l].astype(BF16)
    w1_b = w1[l].astype(BF16)
    w2_b = w2[l].astype(BF16)
    n1 = norm1_g[l][None, :]
    n2 = norm2_g[l][None, :]
    qg = jnp.tile(q_norm_g[l], QK_WIDTH // HEAD_DIM)[None, :]
    kg = jnp.tile(k_norm_g[l], QK_WIDTH // HEAD_DIM)[None, :]
    sg = subln_g[l][None, :]
    lamv = jnp.stack([lambda_q1[l], lambda_k1[l], lambda_q2[l], lambda_k2[l]])
    gm = jnp.asarray(_head_mean_matrix()).astype(BF16)
    rope_tabs = tuple(jnp.asarray(t) for t in _rope_tables(dec_seq))

    xp3 = x_prompt.reshape(1, batch * seq, D_MODEL)
    q, k, v, uc, us, k32, v32 = _in_projection(xp3, mod3, 0, n1, w_in_b, qg, kg, gm, wcs, None, True)
    per_seq = lambda t: t.reshape(batch, seq, t.shape[-1])
    attn = _attention(lamv, sg, per_seq(q), per_seq(k), per_seq(v))
    four = _position_mix(per_seq(uc), per_seq(us))
    flat = lambda t: t.reshape(1, batch * seq, t.shape[-1])
    yp = _output_mlp(xp3, flat(attn), flat(four), mod3, 0, w_out_b, n2, w1_b, w2_b)
    y_prompt = yp.reshape(batch, seq, D_MODEL)
    new_cache_k = k32.reshape(batch, 1, seq, N_HEADS, 2, HEAD_DIM)
    new_cache_v = v32.reshape(batch, 1, seq, N_HEADS, V_DIM)

    q, k, v, uc, us = _in_projection(x_sample, mod3, 1, n1, w_in_b, qg, kg, gm, wcs, rope_tabs, False)
    ck = cache_k[:, l].reshape(dec_batch, past, QK_WIDTH)
    cv = cache_v[:, l].reshape(dec_batch, past, ATTN_WIDTH)
    attn = _attention(lamv, sg, q, k, v, ck, cv)
    four = _position_mix(uc, us)
    y_sample = _output_mlp(x_sample, attn, four, mod3, 1, w_out_b, n2, w1_b, w2_b)

    return (y_prompt, y_sample, new_cache_k, new_cache_v)
```

```python
import functools
import math

import jax
import jax.numpy as jnp
import numpy as np
from jax import lax
from jax.experimental import pallas as pl
from jax.experimental.pallas import tpu as pltpu

D_MODEL = 1024
N_HEADS = 4
HEAD_DIM = 64
V_DIM = 2 * HEAD_DIM
QK_WIDTH = N_HEADS * 2 * HEAD_DIM
ATTN_WIDTH = N_HEADS * V_DIM
N_FOUR_GROUPS = 4
FOUR_GROUP = 128
FOUR_WIDTH = N_FOUR_GROUPS * FOUR_GROUP
PROJ_WIDTH = 2 * QK_WIDTH + ATTN_WIDTH + FOUR_WIDTH
D_FF = 4 * D_MODEL
GRID_W = 64
ROPE_BASE = 10000.0
ROPE_AXIS_DIM = HEAD_DIM // 2
EPS = 1e-6
LAM_INIT = 0.8 - 0.6 * math.exp(-0.3 * 0)
MOD_ROWS = 8

F32 = jnp.float32
BF16 = jnp.bfloat16

VMEM_LIMIT_BYTES = 56 * 1024 * 1024


def _params(n_axes):
    return pltpu.CompilerParams(dimension_semantics=("arbitrary",) * n_axes,
                                vmem_limit_bytes=VMEM_LIMIT_BYTES)


def _resident(shape):
    nd = len(shape)
    return pl.BlockSpec(shape, lambda *_: (0,) * nd, pipeline_mode=pl.Buffered(1))


@functools.lru_cache(maxsize=None)
def _channel_dft():
    idx = np.arange(FOUR_GROUP)
    ang = 2.0 * np.pi * ((idx[:, None] * idx[None, :]) % FOUR_GROUP) / FOUR_GROUP
    s = 1.0 / np.sqrt(FOUR_GROUP)
    return np.stack([np.cos(ang) * s, np.sin(ang) * s]).astype(np.float32)


@functools.lru_cache(maxsize=None)
def _position_dft(n):
    idx = np.arange(n)
    ang = 2.0 * np.pi * ((idx[:, None] * idx[None, :]) % n) / n
    s = 1.0 / np.sqrt(n)
    return (np.cos(ang) * s).astype(np.float32), (-np.sin(ang) * s).astype(np.float32)


@functools.lru_cache(maxsize=None)
def _rope_tables(n):
    rows = n // GRID_W
    row = np.repeat(np.arange(rows), GRID_W).astype(np.float64)
    col = np.tile(np.arange(GRID_W), rows).astype(np.float64)
    inv = ROPE_BASE ** (-np.arange(0, ROPE_AXIS_DIM, 2, dtype=np.float64) / ROPE_AXIS_DIM)
    d = np.arange(QK_WIDTH) % HEAD_DIM
    part = d // ROPE_AXIS_DIM
    i = d % ROPE_AXIS_DIM
    first = i < ROPE_AXIS_DIM // 2
    pos = np.where(part[None, :] == 0, row[:, None], col[:, None])
    ang = pos * inv[i % (ROPE_AXIS_DIM // 2)][None, :]
    cos = np.cos(ang)
    sin = np.where(first[None, :], -np.sin(ang), np.sin(ang))
    return cos.astype(np.float32), sin.astype(np.float32)


@functools.lru_cache(maxsize=None)
def _head_mean_matrix():
    g = np.kron(np.eye(QK_WIDTH // HEAD_DIM), np.ones((HEAD_DIM, HEAD_DIM))) / HEAD_DIM
    return g.astype(np.float32)


MOD_TILE = 1024


def _mod_kernel(c_ref, w_ref, b_ref, o_ref):
    c = c_ref[...]
    s = c / (1.0 + jnp.exp(-c))
    o_ref[...] = jnp.dot(s.astype(BF16), w_ref[...].astype(BF16),
                         preferred_element_type=F32) + b_ref[...]


def _modulation(cvec, w_mod, b_mod):
    n = w_mod.shape[1]
    return pl.pallas_call(
        _mod_kernel,
        grid=(n // MOD_TILE,),
        in_specs=[pl.BlockSpec((MOD_ROWS, D_MODEL), lambda j: (0, 0)),
                  pl.BlockSpec((D_MODEL, MOD_TILE), lambda j: (0, j)),
                  pl.BlockSpec((1, MOD_TILE), lambda j: (0, j))],
        out_specs=pl.BlockSpec((MOD_ROWS, MOD_TILE), lambda j: (0, j)),
        out_shape=jax.ShapeDtypeStruct((MOD_ROWS, n), F32),
        compiler_params=_params(1),
        name="modulation",
    )(cvec, w_mod, b_mod)


def _wcs_kernel(dft_ref, w_ref, o_ref):
    for g in range(N_FOUR_GROUPS):
        w = w_ref[g]
        wc = jnp.dot(dft_ref[0], w, precision=lax.Precision.HIGHEST, preferred_element_type=F32)
        ws = jnp.dot(dft_ref[1], w, precision=lax.Precision.HIGHEST, preferred_element_type=F32)
        o_ref[g] = jnp.concatenate([wc, ws], axis=-1).astype(BF16)


def _fourier_weights(w_four):
    dft = jnp.asarray(_channel_dft())
    return pl.pallas_call(
        _wcs_kernel,
        out_shape=jax.ShapeDtypeStruct((N_FOUR_GROUPS, FOUR_GROUP, 2 * FOUR_GROUP), BF16),
        name="fourier_weights",
    )(dft, w_four)


PROJ_TILE = 512


def _inproj_kernel(*refs, rope, emit_cache, bt, tn):
    it = iter(refs)
    x_ref, mod_ref, g1_ref, w_ref, qg_ref, kg_ref, gm_ref, wcs_ref = (next(it) for _ in range(8))
    if rope:
        cos_ref, sin_ref = next(it), next(it)
    q_ref, k_ref, v_ref, uc_ref, us_ref = (next(it) for _ in range(5))
    if emit_cache:
        k32_ref, v32_ref = next(it), next(it)

    x = x_ref[...].reshape(bt * tn, D_MODEL)
    m = mod_ref[0]
    sh1 = m[:, 0:D_MODEL]
    sc1 = m[:, D_MODEL:2 * D_MODEL]
    ms = jnp.mean(x * x, axis=-1, keepdims=True)
    y = x * lax.rsqrt(ms + EPS) * g1_ref[...]
    h = y * (1.0 + sc1) + sh1
    p = jnp.dot(h.astype(BF16), w_ref[...], preferred_element_type=F32)
    q = p[:, 0:QK_WIDTH]
    k = p[:, QK_WIDTH:2 * QK_WIDTH]
    v = p[:, 2 * QK_WIDTH:2 * QK_WIDTH + ATTN_WIDTH]

    def head_norm(t, g):
        msq = jnp.dot((t * t).astype(BF16), gm_ref[...], preferred_element_type=F32)
        return t * lax.rsqrt(msq + EPS) * g

    q = head_norm(q, qg_ref[...])
    k = head_norm(k, kg_ref[...])
    if emit_cache:
        for i in range(bt):
            rows = slice(i * tn, (i + 1) * tn)
            k32_ref[i] = k[rows, :].T
            for hd in range(N_HEADS):
                v32_ref[i, pl.ds(hd, tn, stride=N_HEADS), :] = v[rows, hd * V_DIM:(hd + 1) * V_DIM]

    if rope:
        lane = lax.broadcasted_iota(jnp.int32, q.shape, 1)
        first = (lane % ROPE_AXIS_DIM) < (ROPE_AXIS_DIM // 2)
        half = ROPE_AXIS_DIM // 2

        def rot(t):
            sw = jnp.where(first, pltpu.roll(t, QK_WIDTH - half, 1), pltpu.roll(t, half, 1))
            return t * cos_ref[...] + sw * sin_ref[...]

        q = rot(q)
        k = rot(k)

    blk = lambda t: t.reshape(bt, tn, t.shape[-1])
    q_ref[...] = blk((q * (1.0 / math.sqrt(HEAD_DIM))).astype(BF16))
    k_ref[...] = blk(k.astype(BF16))
    v_ref[...] = blk(v.astype(BF16))

    f0 = 2 * QK_WIDTH + ATTN_WIDTH
    for g in range(N_FOUR_GROUPS):
        fg = p[:, f0 + g * FOUR_GROUP:f0 + (g + 1) * FOUR_GROUP].astype(BF16)
        u = jnp.dot(fg, wcs_ref[g], preferred_element_type=F32)
        uc_ref[:, :, g * FOUR_GROUP:(g + 1) * FOUR_GROUP] = blk(u[:, 0:FOUR_GROUP].astype(BF16))
        us_ref[:, :, g * FOUR_GROUP:(g + 1) * FOUR_GROUP] = blk(u[:, FOUR_GROUP:].astype(BF16))


def _in_projection(x3, mod3, mod_row0, per_batch_mod, norm_g, w_in, qg, kg, gm, wcs, rope_tabs, emit_cache):
    nb, n, _ = x3.shape
    tn = min(PROJ_TILE, n)
    bt = PROJ_TILE // tn
    assert per_batch_mod is False or bt == 1
    rope = rope_tabs is not None
    tile = lambda width: pl.BlockSpec((bt, tn, width), lambda t, b: (b, t, 0))
    mod_row = (lambda t, b: (mod_row0 + b, 0, 0)) if per_batch_mod else (lambda t, b: (mod_row0, 0, 0))
    in_specs = [tile(D_MODEL),
                pl.BlockSpec((1, 1, 6 * D_MODEL), mod_row),
                _resident((1, D_MODEL)),
                _resident((D_MODEL, PROJ_WIDTH)),
                _resident((1, QK_WIDTH)),
                _resident((1, QK_WIDTH)),
                _resident((QK_WIDTH, QK_WIDTH)),
                _resident((N_FOUR_GROUPS, FOUR_GROUP, 2 * FOUR_GROUP))]
    args = [x3, mod3, norm_g, w_in, qg, kg, gm, wcs]
    if rope:
        in_specs += [pl.BlockSpec((tn, QK_WIDTH), lambda t, b: (t, 0))] * 2
        args += list(rope_tabs)
    n_bf = 5
    out_shape = [jax.ShapeDtypeStruct((nb, n, QK_WIDTH), BF16)] * n_bf
    out_specs = [tile(QK_WIDTH)] * n_bf
    if emit_cache:
        assert tn == n
        out_shape += [jax.ShapeDtypeStruct((nb, QK_WIDTH, n), F32),
                      jax.ShapeDtypeStruct((nb, n * N_HEADS, V_DIM), F32)]
        out_specs += [pl.BlockSpec((bt, QK_WIDTH, n), lambda t, b: (b, 0, 0)),
                      pl.BlockSpec((bt, n * N_HEADS, V_DIM), lambda t, b: (b, 0, 0))]
    return pl.pallas_call(
        functools.partial(_inproj_kernel, rope=rope, emit_cache=emit_cache, bt=bt, tn=tn),
        grid=(n // tn, nb // bt),
        in_specs=in_specs,
        out_specs=out_specs,
        out_shape=out_shape,
        compiler_params=_params(2),
        name="in_projection_rope" if rope else "in_projection",
    )(*args)


ATTN_Q_TILE = 256


def _attn_kernel(*refs, has_cache):
    it = iter(refs)
    lam_ref, sg_ref, q_ref, k_ref, v_ref = (next(it) for _ in range(5))
    if has_cache:
        ck_ref, cv_ref = next(it), next(it)
    o_ref = next(it)

    lv = lam_ref[...]
    l1 = jnp.exp(jnp.sum(lv[0:1] * lv[1:2], axis=-1, keepdims=True))
    l2 = jnp.exp(jnp.sum(lv[2:3] * lv[3:4], axis=-1, keepdims=True))
    lam = l1 - l2 + LAM_INIT

    nt = (((1,), (1,)), ((), ()))
    lane = lax.broadcasted_iota(jnp.int32, (1, V_DIM), 1)
    half0 = lane < HEAD_DIM
    zero = jnp.zeros((), BF16)

    for h in range(N_HEADS):
        sl = slice(h * V_DIM, (h + 1) * V_DIM)
        qh = q_ref[0, :, sl]
        kh = k_ref[0, :, sl]
        vh = v_ref[0, :, sl]
        if has_cache:
            past = ck_ref.shape[2]
            ckh = ck_ref[0, sl, :].astype(BF16)
            cvh = cv_ref[0, pl.ds(h, past, stride=N_HEADS), :].astype(BF16)
        probs = []
        for mhalf in range(2):
            qm = jnp.where(half0 if mhalf == 0 else jnp.logical_not(half0), qh, zero)
            s = lax.dot_general(qm, kh, nt, preferred_element_type=F32)
            mx = jnp.max(s, axis=-1, keepdims=True)
            if has_cache:
                sc = jnp.dot(qm, ckh, preferred_element_type=F32)
                mx = jnp.maximum(mx, jnp.max(sc, axis=-1, keepdims=True))
            e = jnp.exp(s - mx)
            den = jnp.sum(e, axis=-1, keepdims=True)
            if has_cache:
                ec = jnp.exp(sc - mx)
                den = den + jnp.sum(ec, axis=-1, keepdims=True)
            else:
                ec = None
            probs.append((e, ec, den))
        (e0, ec0, d0), (e1, ec1, d1) = probs
        r0 = 1.0 / d0
        r1 = lam / d1
        a = (e0 * r0 - e1 * r1).astype(BF16)
        o = jnp.dot(a, vh, preferred_element_type=F32)
        if has_cache:
            ac = (ec0 * r0 - ec1 * r1).astype(BF16)
            o = o + jnp.dot(ac, cvh, preferred_element_type=F32)
        ms = jnp.mean(o * o, axis=-1, keepdims=True)
        o = o * lax.rsqrt(ms + EPS) * sg_ref[...] * (1.0 - LAM_INIT)
        o_ref[0, :, sl] = o.astype(BF16)


def _attention(lamv, subln_g, q, k, v, cache_k=None, cache_v=None):
    nb, n, _ = q.shape
    tq = min(ATTN_Q_TILE, n)
    has_cache = cache_k is not None
    qtile = pl.BlockSpec((1, tq, QK_WIDTH), lambda b, t: (b, t, 0))
    per_batch = lambda rows: pl.BlockSpec((1, rows, QK_WIDTH), lambda b, t: (b, 0, 0))
    in_specs = [_resident((4, HEAD_DIM)), _resident((1, V_DIM)), qtile, per_batch(n), per_batch(n)]
    args = [lamv, subln_g, q, k, v]
    if has_cache:
        past = cache_k.shape[2]
        in_specs += [pl.BlockSpec((1, QK_WIDTH, past), lambda b, t: (b, 0, 0)),
                     pl.BlockSpec((1, past * N_HEADS, V_DIM), lambda b, t: (b, 0, 0))]
        args += [cache_k, cache_v]
    return pl.pallas_call(
        functools.partial(_attn_kernel, has_cache=has_cache),
        grid=(nb, n // tq),
        in_specs=in_specs,
        out_specs=qtile,
        out_shape=jax.ShapeDtypeStruct((nb, n, ATTN_WIDTH), BF16),
        compiler_params=_params(2),
        name="diff_attention_cached" if has_cache else "diff_attention",
    )(*args)


def _posdft_kernel(pc_ref, ps_ref, uc_ref, us_ref, o_ref):
    o = jnp.dot(pc_ref[...], uc_ref[0], preferred_element_type=F32)
    o = o + jnp.dot(ps_ref[...], us_ref[0], preferred_element_type=F32)
    o_ref[0] = o.astype(BF16)


def _position_mix(uc, us):
    nb, n, _ = uc.shape
    tr = min(512, n)
    pc, ps = _position_dft(n)
    pc = jnp.asarray(pc).astype(BF16)
    ps = jnp.asarray(ps).astype(BF16)
    rows = pl.BlockSpec((tr, n), lambda r, b: (r, 0))
    per_batch = pl.BlockSpec((1, n, FOUR_WIDTH), lambda r, b: (b, 0, 0))
    return pl.pallas_call(
        _posdft_kernel,
        grid=(n // tr, nb),
        in_specs=[rows, rows, per_batch, per_batch],
        out_specs=pl.BlockSpec((1, tr, FOUR_WIDTH), lambda r, b: (b, r, 0)),
        out_shape=jax.ShapeDtypeStruct((nb, n, FOUR_WIDTH), BF16),
        compiler_params=_params(2),
        name="position_dft",
    )(pc, ps, uc, us)


OUT_TILE = 512
FF_CHUNK = 1024


def _out_mlp_kernel(x_ref, a_ref, f_ref, mod_ref, wo_ref, g2_ref, w1_ref, w2_ref, o_ref):
    m = mod_ref[0]
    g1 = m[:, 2 * D_MODEL:3 * D_MODEL]
    sh2 = m[:, 3 * D_MODEL:4 * D_MODEL]
    sc2 = m[:, 4 * D_MODEL:5 * D_MODEL]
    g2 = m[:, 5 * D_MODEL:6 * D_MODEL]
    mix = jnp.dot(a_ref[0], wo_ref[0:ATTN_WIDTH, :], preferred_element_type=F32)
    mix = mix + jnp.dot(f_ref[0], wo_ref[ATTN_WIDTH:, :], preferred_element_type=F32)
    x1 = x_ref[0] + g1 * mix
    ms = jnp.mean(x1 * x1, axis=-1, keepdims=True)
    h = x1 * lax.rsqrt(ms + EPS) * g2_ref[...] * (1.0 + sc2) + sh2
    hb = h.astype(BF16)
    acc = jnp.zeros(x1.shape, F32)
    for c in range(D_FF // FF_CHUNK):
        t = jnp.dot(hb, w1_ref[:, c * FF_CHUNK:(c + 1) * FF_CHUNK], preferred_element_type=F32)
        t = jnp.square(jnp.maximum(t, 0.0)).astype(BF16)
        acc = acc + jnp.dot(t, w2_ref[c * FF_CHUNK:(c + 1) * FF_CHUNK, :], preferred_element_type=F32)
    o_ref[0] = x1 + g2 * acc


def _output_mlp(x3, attn, four, mod3, mod_row0, w_out, norm2_g, w1, w2):
    nb, n, _ = x3.shape
    tm = OUT_TILE
    tile = lambda width: pl.BlockSpec((1, tm, width), lambda t, b: (b, t, 0))
    return pl.pallas_call(
        _out_mlp_kernel,
        grid=(n // tm, nb),
        in_specs=[tile(D_MODEL), tile(ATTN_WIDTH), tile(FOUR_WIDTH),
                  pl.BlockSpec((1, 1, 6 * D_MODEL), lambda t, b: (mod_row0 + b, 0, 0)),
                  _resident((ATTN_WIDTH + FOUR_WIDTH, D_MODEL)),
                  _resident((1, D_MODEL)),
                  _resident((D_MODEL, D_FF)),
                  _resident((D_FF, D_MODEL))],
        out_specs=tile(D_MODEL),
        out_shape=jax.ShapeDtypeStruct((nb, n, D_MODEL), F32),
        compiler_params=_params(2),
        name="output_mlp",
    )(x3, attn, four, mod3, w_out, norm2_g, w1, w2)


def kernel(x_prompt, x_sample, c, cache_k, cache_v, c_ctx, w_mod, b_mod, norm1_g, w_in, q_norm_g, k_norm_g,
           lambda_q1, lambda_k1, lambda_q2, lambda_k2, subln_g, w_four, w_out, norm2_g, w1, w2):
    batch, seq, _ = x_prompt.shape
    dec_batch, dec_seq, _ = x_sample.shape
    past = cache_k.shape[2]
    l = 0

    cvec = jnp.concatenate([c_ctx[None, :], c, jnp.zeros((MOD_ROWS - 1 - dec_batch, D_MODEL), F32)], axis=0)
    mod3 = _modulation(cvec, w_mod[l], b_mod[l][None, :]).reshape(MOD_ROWS, 1, 6 * D_MODEL)
    wcs = _fourier_weights(w_four[l])

    w_in_b = w_in[l].astype(BF16)
    w_out_b = w_out[l].astype(BF16)
    w1_b = w1[l].astype(BF16)
    w2_b = w2[l].astype(BF16)
    n1 = norm1_g[l][None, :]
    n2 = norm2_g[l][None, :]
    qg = jnp.tile(q_norm_g[l], QK_WIDTH // HEAD_DIM)[None, :]
    kg = jnp.tile(k_norm_g[l], QK_WIDTH // HEAD_DIM)[None, :]
    sg = subln_g[l][None, :]
    lamv = jnp.stack([lambda_q1[l], lambda_k1[l], lambda_q2[l], lambda_k2[l]])
    gm = jnp.asarray(_head_mean_matrix()).astype(BF16)
    rope_tabs = tuple(jnp.asarray(t) for t in _rope_tables(dec_seq))

    q, k, v, uc, us, kt32, v32 = _in_projection(x_prompt, mod3, 0, False, n1, w_in_b, qg, kg, gm, wcs, None, True)
    attn = _attention(lamv, sg, q, k, v)
    four = _position_mix(uc, us)
    flat = lambda t: t.reshape(1, batch * seq, t.shape[-1])
    yp = _output_mlp(flat(x_prompt), flat(attn), flat(four), mod3, 0, w_out_b, n2, w1_b, w2_b)
    y_prompt = yp.reshape(batch, seq, D_MODEL)
    new_cache_k = kt32.reshape(batch, N_HEADS, 2, HEAD_DIM, seq).transpose(0, 4, 1, 2, 3)[:, None]
    new_cache_v = v32.reshape(batch, 1, seq, N_HEADS, V_DIM)

    q, k, v, uc, us = _in_projection(x_sample, mod3, 1, True, n1, w_in_b, qg, kg, gm, wcs, rope_tabs, False)
    ck = cache_k[:, l].transpose(0, 2, 3, 4, 1).reshape(dec_batch, QK_WIDTH, past)
    cv = cache_v[:, l].reshape(dec_batch, past * N_HEADS, V_DIM)
    attn = _attention(lamv, sg, q, k, v, ck, cv)
    four = _position_mix(uc, us)
    y_sample = _output_mlp(x_sample, attn, four, mod3, 1, w_out_b, n2, w1_b, w2_b)

    return (y_prompt, y_sample, new_cache_k, new_cache_v)
```

```python
import functools
import math

import jax
import jax.numpy as jnp
import numpy as np
from jax import lax
from jax.experimental import pallas as pl
from jax.experimental.pallas import tpu as pltpu

D_MODEL = 1024
N_HEADS = 4
HEAD_DIM = 64
V_DIM = 2 * HEAD_DIM
QK_WIDTH = N_HEADS * 2 * HEAD_DIM
ATTN_WIDTH = N_HEADS * V_DIM
N_FOUR_GROUPS = 4
FOUR_GROUP = 128
FOUR_WIDTH = N_FOUR_GROUPS * FOUR_GROUP
PROJ_WIDTH = 2 * QK_WIDTH + ATTN_WIDTH + FOUR_WIDTH
D_FF = 4 * D_MODEL
GRID_W = 64
ROPE_BASE = 10000.0
ROPE_AXIS_DIM = HEAD_DIM // 2
EPS = 1e-6
LAM_INIT = 0.8 - 0.6 * math.exp(-0.3 * 0)
LOG2E = 1.4426950408889634
MOD_ROWS = 8

F32 = jnp.float32
BF16 = jnp.bfloat16

VMEM_LIMIT_BYTES = 56 * 1024 * 1024


def _params(n_axes):
    return pltpu.CompilerParams(dimension_semantics=("arbitrary",) * n_axes,
                                vmem_limit_bytes=VMEM_LIMIT_BYTES)


def _resident(shape):
    nd = len(shape)
    return pl.BlockSpec(shape, lambda *_: (0,) * nd, pipeline_mode=pl.Buffered(1))


@functools.lru_cache(maxsize=None)
def _channel_dft():
    idx = np.arange(FOUR_GROUP)
    ang = 2.0 * np.pi * ((idx[:, None] * idx[None, :]) % FOUR_GROUP) / FOUR_GROUP
    s = 1.0 / np.sqrt(FOUR_GROUP)
    return np.stack([np.cos(ang) * s, np.sin(ang) * s]).astype(np.float32)


@functools.lru_cache(maxsize=None)
def _position_dft(n):
    idx = np.arange(n)
    ang = 2.0 * np.pi * ((idx[:, None] * idx[None, :]) % n) / n
    s = 1.0 / np.sqrt(n)
    return (np.cos(ang) * s).astype(np.float32), (-np.sin(ang) * s).astype(np.float32)


@functools.lru_cache(maxsize=None)
def _rope_tables(n):
    rows = n // GRID_W
    row = np.repeat(np.arange(rows), GRID_W).astype(np.float64)
    col = np.tile(np.arange(GRID_W), rows).astype(np.float64)
    inv = ROPE_BASE ** (-np.arange(0, ROPE_AXIS_DIM, 2, dtype=np.float64) / ROPE_AXIS_DIM)
    d = np.arange(QK_WIDTH) % HEAD_DIM
    part = d // ROPE_AXIS_DIM
    i = d % ROPE_AXIS_DIM
    first = i < ROPE_AXIS_DIM // 2
    pos = np.where(part[None, :] == 0, row[:, None], col[:, None])
    ang = pos * inv[i % (ROPE_AXIS_DIM // 2)][None, :]
    cos = np.cos(ang)
    sin = np.where(first[None, :], -np.sin(ang), np.sin(ang))
    return cos.astype(np.float32), sin.astype(np.float32)


@functools.lru_cache(maxsize=None)
def _head_mean_matrix():
    g = np.kron(np.eye(QK_WIDTH // HEAD_DIM), np.ones((HEAD_DIM, HEAD_DIM))) / HEAD_DIM
    return g.astype(np.float32)


MOD_TILE = 1024


def _mod_kernel(c_ref, w_ref, b_ref, o_ref):
    c = c_ref[...]
    s = c / (1.0 + jnp.exp(-c))
    o_ref[...] = jnp.dot(s.astype(BF16), w_ref[...].astype(BF16),
                         preferred_element_type=F32) + b_ref[...]


def _modulation(cvec, w_mod, b_mod):
    n = w_mod.shape[1]
    return pl.pallas_call(
        _mod_kernel,
        grid=(n // MOD_TILE,),
        in_specs=[pl.BlockSpec((MOD_ROWS, D_MODEL), lambda j: (0, 0)),
                  pl.BlockSpec((D_MODEL, MOD_TILE), lambda j: (0, j)),
                  pl.BlockSpec((1, MOD_TILE), lambda j: (0, j))],
        out_specs=pl.BlockSpec((MOD_ROWS, MOD_TILE), lambda j: (0, j)),
        out_shape=jax.ShapeDtypeStruct((MOD_ROWS, n), F32),
        compiler_params=_params(1),
        name="modulation",
    )(cvec, w_mod, b_mod)


def _wcs_kernel(dft_ref, w_ref, o_ref):
    for g in range(N_FOUR_GROUPS):
        w = w_ref[g]
        wc = jnp.dot(dft_ref[0], w, precision=lax.Precision.HIGHEST, preferred_element_type=F32)
        ws = jnp.dot(dft_ref[1], w, precision=lax.Precision.HIGHEST, preferred_element_type=F32)
        o_ref[g] = jnp.concatenate([wc, ws], axis=-1).astype(BF16)


def _fourier_weights(w_four):
    dft = jnp.asarray(_channel_dft())
    return pl.pallas_call(
        _wcs_kernel,
        out_shape=jax.ShapeDtypeStruct((N_FOUR_GROUPS, FOUR_GROUP, 2 * FOUR_GROUP), BF16),
        name="fourier_weights",
    )(dft, w_four)


PROJ_TILE = 512


def _inproj_kernel(*refs, rope, emit_cache, bt, tn):
    it = iter(refs)
    x_ref, mod_ref, g1_ref, w_ref, qg_ref, kg_ref, gm_ref, wcs_ref = (next(it) for _ in range(8))
    if rope:
        cos_ref, sin_ref = next(it), next(it)
    q_ref, k_ref, v_ref, uc_ref, us_ref = (next(it) for _ in range(5))
    if emit_cache:
        k32_ref, v32_ref = next(it), next(it)

    x = x_ref[...].reshape(bt * tn, D_MODEL)
    m = mod_ref[0]
    sh1 = m[:, 0:D_MODEL]
    sc1 = m[:, D_MODEL:2 * D_MODEL]
    ms = jnp.mean(x * x, axis=-1, keepdims=True)
    y = x * lax.rsqrt(ms + EPS) * g1_ref[...]
    h = y * (1.0 + sc1) + sh1
    p = jnp.dot(h.astype(BF16), w_ref[...], preferred_element_type=F32)
    q = p[:, 0:QK_WIDTH]
    k = p[:, QK_WIDTH:2 * QK_WIDTH]
    v = p[:, 2 * QK_WIDTH:2 * QK_WIDTH + ATTN_WIDTH]

    def head_norm(t, g):
        msq = jnp.dot((t * t).astype(BF16), gm_ref[...], preferred_element_type=F32)
        return t * lax.rsqrt(msq + EPS) * g

    q = head_norm(q, qg_ref[...])
    k = head_norm(k, kg_ref[...])
    if emit_cache:
        for i in range(bt):
            rows = slice(i * tn, (i + 1) * tn)
            k32_ref[i] = k[rows, :].T
            for hd in range(N_HEADS):
                v32_ref[i, pl.ds(hd, tn, stride=N_HEADS), :] = v[rows, hd * V_DIM:(hd + 1) * V_DIM]

    if rope:
        lane = lax.broadcasted_iota(jnp.int32, q.shape, 1)
        first = (lane % ROPE_AXIS_DIM) < (ROPE_AXIS_DIM // 2)
        half = ROPE_AXIS_DIM // 2

        def rot(t):
            sw = jnp.where(first, pltpu.roll(t, QK_WIDTH - half, 1), pltpu.roll(t, half, 1))
            return t * cos_ref[...] + sw * sin_ref[...]

        q = rot(q)
        k = rot(k)

    blk = lambda t: t.reshape(bt, tn, t.shape[-1])
    q_ref[...] = blk((q * (LOG2E / math.sqrt(HEAD_DIM))).astype(BF16))
    k_ref[...] = blk(k.astype(BF16))
    v_ref[...] = blk(v.astype(BF16))

    f0 = 2 * QK_WIDTH + ATTN_WIDTH
    for g in range(N_FOUR_GROUPS):
        fg = p[:, f0 + g * FOUR_GROUP:f0 + (g + 1) * FOUR_GROUP].astype(BF16)
        u = jnp.dot(fg, wcs_ref[g], preferred_element_type=F32)
        uc_ref[:, :, g * FOUR_GROUP:(g + 1) * FOUR_GROUP] = blk(u[:, 0:FOUR_GROUP].astype(BF16))
        us_ref[:, :, g * FOUR_GROUP:(g + 1) * FOUR_GROUP] = blk(u[:, FOUR_GROUP:].astype(BF16))


def _in_projection(x3, mod3, mod_row0, per_batch_mod, norm_g, w_in, qg, kg, gm, wcs, rope_tabs, emit_cache):
    nb, n, _ = x3.shape
    tn = min(PROJ_TILE, n)
    bt = PROJ_TILE // tn
    assert per_batch_mod is False or bt == 1
    rope = rope_tabs is not None
    tile = lambda width: pl.BlockSpec((bt, tn, width), lambda t, b: (b, t, 0))
    mod_row = (lambda t, b: (mod_row0 + b, 0, 0)) if per_batch_mod else (lambda t, b: (mod_row0, 0, 0))
    in_specs = [tile(D_MODEL),
                pl.BlockSpec((1, 1, 6 * D_MODEL), mod_row),
                _resident((1, D_MODEL)),
                _resident((D_MODEL, PROJ_WIDTH)),
                _resident((1, QK_WIDTH)),
                _resident((1, QK_WIDTH)),
                _resident((QK_WIDTH, QK_WIDTH)),
                _resident((N_FOUR_GROUPS, FOUR_GROUP, 2 * FOUR_GROUP))]
    args = [x3, mod3, norm_g, w_in, qg, kg, gm, wcs]
    if rope:
        in_specs += [pl.BlockSpec((tn, QK_WIDTH), lambda t, b: (t, 0))] * 2
        args += list(rope_tabs)
    n_bf = 5
    out_shape = [jax.ShapeDtypeStruct((nb, n, QK_WIDTH), BF16)] * n_bf
    out_specs = [tile(QK_WIDTH)] * n_bf
    if emit_cache:
        assert tn == n
        out_shape += [jax.ShapeDtypeStruct((nb, QK_WIDTH, n), F32),
                      jax.ShapeDtypeStruct((nb, n * N_HEADS, V_DIM), F32)]
        out_specs += [pl.BlockSpec((bt, QK_WIDTH, n), lambda t, b: (b, 0, 0)),
                      pl.BlockSpec((bt, n * N_HEADS, V_DIM), lambda t, b: (b, 0, 0))]
    return pl.pallas_call(
        functools.partial(_inproj_kernel, rope=rope, emit_cache=emit_cache, bt=bt, tn=tn),
        grid=(n // tn, nb // bt),
        in_specs=in_specs,
        out_specs=out_specs,
        out_shape=out_shape,
        compiler_params=_params(2),
        name="in_projection_rope" if rope else "in_projection",
    )(*args)


ATTN_Q_TILE = 256


KEY_CHUNK = 512
ONES_ROWS = 16


def _attn_kernel(*refs, has_cache, n, past):
    it = iter(refs)
    lam_ref, sg_ref, q_ref, k_ref, v_ref = (next(it) for _ in range(5))
    if has_cache:
        ck_ref, cv_ref = next(it), next(it)
    o_ref = next(it)
    kall_ref, vt_ref = next(it), next(it)
    nk = n + past
    tq = q_ref.shape[1]

    @pl.when(pl.program_id(1) == 0)
    def _per_batch_setup():
        if has_cache:
            kall_ref[0:past, :] = ck_ref[0].T.astype(BF16)
        kall_ref[past:nk, :] = k_ref[0]
        ones = jnp.ones((ONES_ROWS, nk), BF16)
        for h in range(N_HEADS):
            if has_cache:
                cvh = cv_ref[0, pl.ds(h, past, stride=N_HEADS), :]
                vt_ref[h, 0:V_DIM, 0:past] = cvh.T.astype(BF16)
            vh = v_ref[0, :, h * V_DIM:(h + 1) * V_DIM].astype(F32)
            vt_ref[h, 0:V_DIM, past:nk] = vh.T.astype(BF16)
            vt_ref[h, V_DIM:V_DIM + ONES_ROWS, :] = ones

    lv = lam_ref[...]
    l1 = jnp.exp(jnp.sum(lv[0:1] * lv[1:2], axis=-1, keepdims=True))
    l2 = jnp.exp(jnp.sum(lv[2:3] * lv[3:4], axis=-1, keepdims=True))
    lam = l1 - l2 + LAM_INIT

    q = q_ref[0].astype(F32)
    feat = lax.broadcasted_iota(jnp.int32, (V_DIM, tq), 0)
    half0 = feat < HEAD_DIM

    kc = min(KEY_CHUNK, nk)
    chunks = [slice(c * kc, (c + 1) * kc) for c in range(nk // kc)]
    chains = [(h, mhalf) for h in range(N_HEADS) for mhalf in range(2)]
    head = lambda h: slice(h * V_DIM, (h + 1) * V_DIM)
    qts = {}

    def scores(chain):
        h, mhalf = chain
        if h not in qts:
            qts[h] = q[:, head(h)].T
        qm = jnp.where(half0 if mhalf == 0 else jnp.logical_not(half0), qts[h], 0.0).astype(BF16)

        def one(rows):
            s = jnp.dot(kall_ref[rows, head(h)], qm, preferred_element_type=F32)
            return s, jnp.max(s, axis=0, keepdims=True)
        return one

    pending = [scores(chains[0])(rows) for rows in chunks]
    acc = [None, None]
    for i, (h, mhalf) in enumerate(chains):
        cur = pending
        mx = functools.reduce(jnp.maximum, [m for _, m in cur])
        nxt = scores(chains[i + 1]) if i + 1 < len(chains) else None
        pending = []
        a = None
        for c, rows in enumerate(chunks):
            if nxt is not None:
                pending.append(nxt(rows))
            e = jnp.exp2(cur[c][0] - mx).astype(BF16)
            part = jnp.dot(vt_ref[h, :, rows], e, preferred_element_type=F32)
            a = part if a is None else a + part
        acc[mhalf] = a
        if mhalf == 1:
            a0, a1 = acc
            r0 = 1.0 / a0[V_DIM:V_DIM + 1, :]
            r1 = lam / a1[V_DIM:V_DIM + 1, :]
            o = (a0[0:V_DIM, :] * r0 - a1[0:V_DIM, :] * r1).T
            ms = jnp.mean(o * o, axis=-1, keepdims=True)
            o = o * lax.rsqrt(ms + EPS) * sg_ref[...] * (1.0 - LAM_INIT)
            o_ref[0, :, head(h)] = o.astype(BF16)


def _attention(lamv, subln_g, q, k, v, cache_k=None, cache_v=None):
    nb, n, _ = q.shape
    tq = min(ATTN_Q_TILE, n)
    has_cache = cache_k is not None
    past = cache_k.shape[2] if has_cache else 0
    qtile = pl.BlockSpec((1, tq, QK_WIDTH), lambda b, t: (b, t, 0))
    per_batch = pl.BlockSpec((1, n, QK_WIDTH), lambda b, t: (b, 0, 0))
    in_specs = [_resident((4, HEAD_DIM)), _resident((1, V_DIM)), qtile, per_batch, per_batch]
    args = [lamv, subln_g, q, k, v]
    if has_cache:
        in_specs += [pl.BlockSpec((1, QK_WIDTH, past), lambda b, t: (b, 0, 0)),
                     pl.BlockSpec((1, past * N_HEADS, V_DIM), lambda b, t: (b, 0, 0))]
        args += [cache_k, cache_v]
    return pl.pallas_call(
        functools.partial(_attn_kernel, has_cache=has_cache, n=n, past=past),
        grid=(nb, n // tq),
        in_specs=in_specs,
        out_specs=qtile,
        out_shape=jax.ShapeDtypeStruct((nb, n, ATTN_WIDTH), BF16),
        scratch_shapes=[pltpu.VMEM((n + past, QK_WIDTH), BF16),
                        pltpu.VMEM((N_HEADS, V_DIM + ONES_ROWS, n + past), BF16)],
        compiler_params=_params(2),
        name="diff_attention_cached" if has_cache else "diff_attention",
    )(*args)


def _posdft_kernel(pc_ref, ps_ref, uc_ref, us_ref, o_ref):
    o = jnp.dot(pc_ref[...], uc_ref[0], preferred_element_type=F32)
    o = o + jnp.dot(ps_ref[...], us_ref[0], preferred_element_type=F32)
    o_ref[0] = o.astype(BF16)


def _position_mix(uc, us):
    nb, n, _ = uc.shape
    tr = min(512, n)
    pc, ps = _position_dft(n)
    pc = jnp.asarray(pc).astype(BF16)
    ps = jnp.asarray(ps).astype(BF16)
    rows = pl.BlockSpec((tr, n), lambda r, b: (r, 0))
    per_batch = pl.BlockSpec((1, n, FOUR_WIDTH), lambda r, b: (b, 0, 0))
    return pl.pallas_call(
        _posdft_kernel,
        grid=(n // tr, nb),
        in_specs=[rows, rows, per_batch, per_batch],
        out_specs=pl.BlockSpec((1, tr, FOUR_WIDTH), lambda r, b: (b, r, 0)),
        out_shape=jax.ShapeDtypeStruct((nb, n, FOUR_WIDTH), BF16),
        compiler_params=_params(2),
        name="position_dft",
    )(pc, ps, uc, us)


OUT_TILE = 512
FF_CHUNK = 1024


def _out_mlp_kernel(x_ref, a_ref, f_ref, mod_ref, wo_ref, g2_ref, w1_ref, w2_ref, o_ref):
    m = mod_ref[0]
    g1 = m[:, 2 * D_MODEL:3 * D_MODEL]
    sh2 = m[:, 3 * D_MODEL:4 * D_MODEL]
    sc2 = m[:, 4 * D_MODEL:5 * D_MODEL]
    g2 = m[:, 5 * D_MODEL:6 * D_MODEL]
    mix = jnp.dot(a_ref[0], wo_ref[0:ATTN_WIDTH, :], preferred_element_type=F32)
    mix = mix + jnp.dot(f_ref[0], wo_ref[ATTN_WIDTH:, :], preferred_element_type=F32)
    x1 = x_ref[0] + g1 * mix
    ms = jnp.mean(x1 * x1, axis=-1, keepdims=True)
    h = x1 * lax.rsqrt(ms + EPS) * g2_ref[...] * (1.0 + sc2) + sh2
    hb = h.astype(BF16)
    acc = jnp.zeros(x1.shape, F32)
    for c in range(D_FF // FF_CHUNK):
        t = jnp.dot(hb, w1_ref[:, c * FF_CHUNK:(c + 1) * FF_CHUNK], preferred_element_type=F32)
        t = jnp.square(jnp.maximum(t, 0.0)).astype(BF16)
        acc = acc + jnp.dot(t, w2_ref[c * FF_CHUNK:(c + 1) * FF_CHUNK, :], preferred_element_type=F32)
    o_ref[0] = x1 + g2 * acc


def _output_mlp(x3, attn, four, mod3, mod_row0, w_out, norm2_g, w1, w2):
    nb, n, _ = x3.shape
    tm = OUT_TILE
    tile = lambda width: pl.BlockSpec((1, tm, width), lambda t, b: (b, t, 0))
    return pl.pallas_call(
        _out_mlp_kernel,
        grid=(n // tm, nb),
        in_specs=[tile(D_MODEL), tile(ATTN_WIDTH), tile(FOUR_WIDTH),
                  pl.BlockSpec((1, 1, 6 * D_MODEL), lambda t, b: (mod_row0 + b, 0, 0)),
                  _resident((ATTN_WIDTH + FOUR_WIDTH, D_MODEL)),
                  _resident((1, D_MODEL)),
                  _resident((D_MODEL, D_FF)),
                  _resident((D_FF, D_MODEL))],
        out_specs=tile(D_MODEL),
        out_shape=jax.ShapeDtypeStruct((nb, n, D_MODEL), F32),
        compiler_params=_params(2),
        name="output_mlp",
    )(x3, attn, four, mod3, w_out, norm2_g, w1, w2)


def kernel(x_prompt, x_sample, c, cache_k, cache_v, c_ctx, w_mod, b_mod, norm1_g, w_in, q_norm_g, k_norm_g,
           lambda_q1, lambda_k1, lambda_q2, lambda_k2, subln_g, w_four, w_out, norm2_g, w1, w2):
    batch, seq, _ = x_prompt.shape
    dec_batch, dec_seq, _ = x_sample.shape
    past = cache_k.shape[2]
    l = 0

    cvec = jnp.concatenate([c_ctx[None, :], c, jnp.zeros((MOD_ROWS - 1 - dec_batch, D_MODEL), F32)], axis=0)
    mod3 = _modulation(cvec, w_mod[l], b_mod[l][None, :]).reshape(MOD_ROWS, 1, 6 * D_MODEL)
    wcs = _fourier_weights(w_four[l])

    w_in_b = w_in[l].astype(BF16)
    w_out_b = w_out[l].astype(BF16)
    w1_b = w1[l].astype(BF16)
    w2_b = w2[l].astype(BF16)
    n1 = norm1_g[l][None, :]
    n2 = norm2_g[l][None, :]
    qg = jnp.tile(q_norm_g[l], QK_WIDTH // HEAD_DIM)[None, :]
    kg = jnp.tile(k_norm_g[l], QK_WIDTH // HEAD_DIM)[None, :]
    sg = subln_g[l][None, :]
    lamv = jnp.stack([lambda_q1[l], lambda_k1[l], lambda_q2[l], lambda_k2[l]])
    gm = jnp.asarray(_head_mean_matrix()).astype(BF16)
    rope_tabs = tuple(jnp.asarray(t) for t in _rope_tables(dec_seq))

    q, k, v, uc, us, kt32, v32 = _in_projection(x_prompt, mod3, 0, False, n1, w_in_b, qg, kg, gm, wcs, None, True)
    attn = _attention(lamv, sg, q, k, v)
    four = _position_mix(uc, us)
    flat = lambda t: t.reshape(1, batch * seq, t.shape[-1])
    yp = _output_mlp(flat(x_prompt), flat(attn), flat(four), mod3, 0, w_out_b, n2, w1_b, w2_b)
    y_prompt = yp.reshape(batch, seq, D_MODEL)
    new_cache_k = kt32.reshape(batch, N_HEADS, 2, HEAD_DIM, seq).transpose(0, 4, 1, 2, 3)[:, None]
    new_cache_v = v32.reshape(batch, 1, seq, N_HEADS, V_DIM)

    q, k, v, uc, us = _in_projection(x_sample, mod3, 1, True, n1, w_in_b, qg, kg, gm, wcs, rope_tabs, False)
    ck = cache_k[:, l].transpose(0, 2, 3, 4, 1).reshape(dec_batch, QK_WIDTH, past)
    cv = cache_v[:, l].reshape(dec_batch, past * N_HEADS, V_DIM)
    attn = _attention(lamv, sg, q, k, v, ck, cv)
    four = _position_mix(uc, us)
    y_sample = _output_mlp(x_sample, attn, four, mod3, 1, w_out_b, n2, w1_b, w2_b)

    return (y_prompt, y_sample, new_cache_k, new_cache_v)
```

```python
import functools
import math

import jax
import jax.numpy as jnp
import numpy as np
from jax import lax
from jax.experimental import pallas as pl
from jax.experimental.pallas import tpu as pltpu

D_MODEL = 1024
N_HEADS = 4
HEAD_DIM = 64
V_DIM = 2 * HEAD_DIM
QK_WIDTH = N_HEADS * 2 * HEAD_DIM
ATTN_WIDTH = N_HEADS * V_DIM
N_FOUR_GROUPS = 4
FOUR_GROUP = 128
FOUR_WIDTH = N_FOUR_GROUPS * FOUR_GROUP
PROJ_WIDTH = 2 * QK_WIDTH + ATTN_WIDTH + FOUR_WIDTH
D_FF = 4 * D_MODEL
GRID_W = 64
ROPE_BASE = 10000.0
ROPE_AXIS_DIM = HEAD_DIM // 2
EPS = 1e-6
LAM_INIT = 0.8 - 0.6 * math.exp(-0.3 * 0)
LOG2E = 1.4426950408889634
MOD_ROWS = 8

F32 = jnp.float32
BF16 = jnp.bfloat16

VMEM_LIMIT_BYTES = 56 * 1024 * 1024


def _params(n_axes):
    return pltpu.CompilerParams(dimension_semantics=("arbitrary",) * n_axes,
                                vmem_limit_bytes=VMEM_LIMIT_BYTES)


def _resident(shape):
    nd = len(shape)
    return pl.BlockSpec(shape, lambda *_: (0,) * nd, pipeline_mode=pl.Buffered(1))


@functools.lru_cache(maxsize=None)
def _channel_dft():
    idx = np.arange(FOUR_GROUP)
    ang = 2.0 * np.pi * ((idx[:, None] * idx[None, :]) % FOUR_GROUP) / FOUR_GROUP
    s = 1.0 / np.sqrt(FOUR_GROUP)
    return np.stack([np.cos(ang) * s, np.sin(ang) * s]).astype(np.float32)


@functools.lru_cache(maxsize=None)
def _position_dft(n):
    idx = np.arange(n)
    ang = 2.0 * np.pi * ((idx[:, None] * idx[None, :]) % n) / n
    s = 1.0 / np.sqrt(n)
    return (np.cos(ang) * s).astype(np.float32), (-np.sin(ang) * s).astype(np.float32)


@functools.lru_cache(maxsize=None)
def _rope_tables(n):
    rows = n // GRID_W
    row = np.repeat(np.arange(rows), GRID_W).astype(np.float64)
    col = np.tile(np.arange(GRID_W), rows).astype(np.float64)
    inv = ROPE_BASE ** (-np.arange(0, ROPE_AXIS_DIM, 2, dtype=np.float64) / ROPE_AXIS_DIM)
    d = np.arange(QK_WIDTH) % HEAD_DIM
    part = d // ROPE_AXIS_DIM
    i = d % ROPE_AXIS_DIM
    first = i < ROPE_AXIS_DIM // 2
    pos = np.where(part[None, :] == 0, row[:, None], col[:, None])
    ang = pos * inv[i % (ROPE_AXIS_DIM // 2)][None, :]
    cos = np.cos(ang)
    sin = np.where(first[None, :], -np.sin(ang), np.sin(ang))
    return cos.astype(np.float32), sin.astype(np.float32)


@functools.lru_cache(maxsize=None)
def _head_mean_matrix():
    g = np.kron(np.eye(QK_WIDTH // HEAD_DIM), np.ones((HEAD_DIM, HEAD_DIM))) / HEAD_DIM
    return g.astype(np.float32)


MOD_TILE = 1024


def _mod_kernel(c_ref, w_ref, b_ref, o_ref):
    c = c_ref[...]
    s = c / (1.0 + jnp.exp(-c))
    o_ref[...] = jnp.dot(s.astype(BF16), w_ref[...].astype(BF16),
                         preferred_element_type=F32) + b_ref[...]


def _modulation(cvec, w_mod, b_mod):
    n = w_mod.shape[1]
    return pl.pallas_call(
        _mod_kernel,
        grid=(n // MOD_TILE,),
        in_specs=[pl.BlockSpec((MOD_ROWS, D_MODEL), lambda j: (0, 0)),
                  pl.BlockSpec((D_MODEL, MOD_TILE), lambda j: (0, j)),
                  pl.BlockSpec((1, MOD_TILE), lambda j: (0, j))],
        out_specs=pl.BlockSpec((MOD_ROWS, MOD_TILE), lambda j: (0, j)),
        out_shape=jax.ShapeDtypeStruct((MOD_ROWS, n), F32),
        compiler_params=_params(1),
        name="modulation",
    )(cvec, w_mod, b_mod)


def _wcs_kernel(dft_ref, w_ref, o_ref):
    for g in range(N_FOUR_GROUPS):
        w = w_ref[g]
        wc = jnp.dot(dft_ref[0], w, precision=lax.Precision.HIGHEST, preferred_element_type=F32)
        ws = jnp.dot(dft_ref[1], w, precision=lax.Precision.HIGHEST, preferred_element_type=F32)
        o_ref[g] = jnp.concatenate([wc, ws], axis=-1).astype(BF16)


def _fourier_weights(w_four):
    dft = jnp.asarray(_channel_dft())
    return pl.pallas_call(
        _wcs_kernel,
        out_shape=jax.ShapeDtypeStruct((N_FOUR_GROUPS, FOUR_GROUP, 2 * FOUR_GROUP), BF16),
        name="fourier_weights",
    )(dft, w_four)


PROJ_TILE = 512


def _inproj_kernel(*refs, rope, emit_cache, bt, tn):
    it = iter(refs)
    x_ref, mod_ref, g1_ref, w_ref, qg_ref, kg_ref, gm_ref, wcs_ref = (next(it) for _ in range(8))
    if rope:
        cos_ref, sin_ref = next(it), next(it)
    q_ref, k_ref, v_ref, uc_ref, us_ref = (next(it) for _ in range(5))
    if emit_cache:
        k32_ref, v32_ref = next(it), next(it)

    x = x_ref[...].reshape(bt * tn, D_MODEL)
    m = mod_ref[0]
    sh1 = m[:, 0:D_MODEL]
    sc1 = m[:, D_MODEL:2 * D_MODEL]
    ms = jnp.mean(x * x, axis=-1, keepdims=True)
    y = x * lax.rsqrt(ms + EPS) * g1_ref[...]
    h = y * (1.0 + sc1) + sh1
    p = jnp.dot(h.astype(BF16), w_ref[...], preferred_element_type=F32)
    q = p[:, 0:QK_WIDTH]
    k = p[:, QK_WIDTH:2 * QK_WIDTH]
    v = p[:, 2 * QK_WIDTH:2 * QK_WIDTH + ATTN_WIDTH]

    def head_norm(t, g):
        msq = jnp.dot((t * t).astype(BF16), gm_ref[...], preferred_element_type=F32)
        return t * lax.rsqrt(msq + EPS) * g

    q = head_norm(q, qg_ref[...])
    k = head_norm(k, kg_ref[...])
    if emit_cache:
        for i in range(bt):
            rows = slice(i * tn, (i + 1) * tn)
            k32_ref[i] = k[rows, :].T
            for hd in range(N_HEADS):
                v32_ref[i, pl.ds(hd, tn, stride=N_HEADS), :] = v[rows, hd * V_DIM:(hd + 1) * V_DIM]

    if rope:
        lane = lax.broadcasted_iota(jnp.int32, q.shape, 1)
        first = (lane % ROPE_AXIS_DIM) < (ROPE_AXIS_DIM // 2)
        half = ROPE_AXIS_DIM // 2

        def rot(t):
            sw = jnp.where(first, pltpu.roll(t, QK_WIDTH - half, 1), pltpu.roll(t, half, 1))
            return t * cos_ref[...] + sw * sin_ref[...]

        q = rot(q)
        k = rot(k)

    blk = lambda t: t.reshape(bt, tn, t.shape[-1])
    q_ref[...] = blk((q * (LOG2E / math.sqrt(HEAD_DIM))).astype(BF16))
    k_ref[...] = blk(k.astype(BF16))
    v_ref[...] = blk(v.astype(BF16))

    f0 = 2 * QK_WIDTH + ATTN_WIDTH
    for g in range(N_FOUR_GROUPS):
        fg = p[:, f0 + g * FOUR_GROUP:f0 + (g + 1) * FOUR_GROUP].astype(BF16)
        u = jnp.dot(fg, wcs_ref[g], preferred_element_type=F32)
        uc_ref[:, :, g * FOUR_GROUP:(g + 1) * FOUR_GROUP] = blk(u[:, 0:FOUR_GROUP].astype(BF16))
        us_ref[:, :, g * FOUR_GROUP:(g + 1) * FOUR_GROUP] = blk(u[:, FOUR_GROUP:].astype(BF16))


def _in_projection(x3, mod3, mod_row0, per_batch_mod, norm_g, w_in, qg, kg, gm, wcs, rope_tabs, emit_cache):
    nb, n, _ = x3.shape
    tn = min(PROJ_TILE, n)
    bt = PROJ_TILE // tn
    assert per_batch_mod is False or bt == 1
    rope = rope_tabs is not None
    tile = lambda width: pl.BlockSpec((bt, tn, width), lambda t, b: (b, t, 0))
    mod_row = (lambda t, b: (mod_row0 + b, 0, 0)) if per_batch_mod else (lambda t, b: (mod_row0, 0, 0))
    in_specs = [tile(D_MODEL),
                pl.BlockSpec((1, 1, 6 * D_MODEL), mod_row),
                _resident((1, D_MODEL)),
                _resident((D_MODEL, PROJ_WIDTH)),
                _resident((1, QK_WIDTH)),
                _resident((1, QK_WIDTH)),
                _resident((QK_WIDTH, QK_WIDTH)),
                _resident((N_FOUR_GROUPS, FOUR_GROUP, 2 * FOUR_GROUP))]
    args = [x3, mod3, norm_g, w_in, qg, kg, gm, wcs]
    if rope:
        in_specs += [pl.BlockSpec((tn, QK_WIDTH), lambda t, b: (t, 0))] * 2
        args += list(rope_tabs)
    n_bf = 5
    out_shape = [jax.ShapeDtypeStruct((nb, n, QK_WIDTH), BF16)] * n_bf
    out_specs = [tile(QK_WIDTH)] * n_bf
    if emit_cache:
        assert tn == n
        out_shape += [jax.ShapeDtypeStruct((nb, QK_WIDTH, n), F32),
                      jax.ShapeDtypeStruct((nb, n * N_HEADS, V_DIM), F32)]
        out_specs += [pl.BlockSpec((bt, QK_WIDTH, n), lambda t, b: (b, 0, 0)),
                      pl.BlockSpec((bt, n * N_HEADS, V_DIM), lambda t, b: (b, 0, 0))]
    return pl.pallas_call(
        functools.partial(_inproj_kernel, rope=rope, emit_cache=emit_cache, bt=bt, tn=tn),
        grid=(n // tn, nb // bt),
        in_specs=in_specs,
        out_specs=out_specs,
        out_shape=out_shape,
        compiler_params=_params(2),
        name="in_projection_rope" if rope else "in_projection",
    )(*args)


ATTN_Q_TILE = 256


KEY_CHUNK = 1280
ONES_ROWS = 16


PREFETCH_KEYS = 1024
ATTN_STEP_KEYS = 1024


def _attn_kernel(*refs, has_cache, n, past, bt):
    it = iter(refs)
    lam_ref, sg_ref, q_ref, k_ref, v_ref = (next(it) for _ in range(5))
    if has_cache:
        ck_ref, cv_ref = next(it), next(it)
    o_ref = next(it)
    kall_ref, vt_ref = next(it), next(it)
    nk = n + past
    tq = q_ref.shape[1]

    @pl.when(pl.program_id(1) == 0)
    def _per_batch_setup():
        ones = jnp.ones((ONES_ROWS, nk), BF16)
        for b in range(bt):
            if has_cache:
                kall_ref[b, 0:past, :] = ck_ref[b].T.astype(BF16)
            kall_ref[b, past:nk, :] = k_ref[b]
            for h in range(N_HEADS):
                if has_cache:
                    cvh = cv_ref[b, pl.ds(h, past, stride=N_HEADS), :]
                    vt_ref[b, h, 0:V_DIM, 0:past] = cvh.T.astype(BF16)
                vh = v_ref[b, :, h * V_DIM:(h + 1) * V_DIM].astype(F32)
                vt_ref[b, h, 0:V_DIM, past:nk] = vh.T.astype(BF16)
                vt_ref[b, h, V_DIM:V_DIM + ONES_ROWS, :] = ones

    lv = lam_ref[...]
    l1 = jnp.exp(jnp.sum(lv[0:1] * lv[1:2], axis=-1, keepdims=True))
    l2 = jnp.exp(jnp.sum(lv[2:3] * lv[3:4], axis=-1, keepdims=True))
    lam = l1 - l2 + LAM_INIT

    feat = lax.broadcasted_iota(jnp.int32, (V_DIM, tq), 0)
    half0 = feat < HEAD_DIM

    kc = min(KEY_CHUNK, nk)
    chunks = [slice(c * kc, (c + 1) * kc) for c in range(nk // kc)]
    chains = [(b, h, mhalf) for b in range(bt) for h in range(N_HEADS) for mhalf in range(2)]
    head = lambda h: slice(h * V_DIM, (h + 1) * V_DIM)
    qts = {}

    def scores(chain):
        b, h, mhalf = chain
        if (b, h) not in qts:
            qts[b, h] = q_ref[b, :, head(h)].astype(F32).T
        qm = jnp.where(half0 if mhalf == 0 else jnp.logical_not(half0), qts[b, h], 0.0).astype(BF16)

        def one(rows):
            s = jnp.dot(kall_ref[b, rows, head(h)], qm, preferred_element_type=F32)
            return s, jnp.max(s, axis=0, keepdims=True)
        return one

    depth = max(1, min(len(chains), PREFETCH_KEYS // nk))
    queue = [[scores(chain)(rows) for rows in chunks] for chain in chains[:depth]]
    acc = [None, None]
    for i, (b, h, mhalf) in enumerate(chains):
        cur = queue.pop(0)
        mx = functools.reduce(jnp.maximum, [m for _, m in cur])
        nxt = scores(chains[i + depth]) if i + depth < len(chains) else None
        issued = []
        a = None
        for c, rows in enumerate(chunks):
            if nxt is not None:
                issued.append(nxt(rows))
            e = jnp.exp2(cur[c][0] - mx).astype(BF16)
            part = jnp.dot(vt_ref[b, h, :, rows], e, preferred_element_type=F32)
            a = part if a is None else a + part
        if nxt is not None:
            queue.append(issued)
        acc[mhalf] = a
        if mhalf == 1:
            a0, a1 = acc
            r0 = 1.0 / a0[V_DIM:V_DIM + 1, :]
            r1 = lam / a1[V_DIM:V_DIM + 1, :]
            o = (a0[0:V_DIM, :] * r0 - a1[0:V_DIM, :] * r1).T
            ms = jnp.mean(o * o, axis=-1, keepdims=True)
            o = o * lax.rsqrt(ms + EPS) * sg_ref[...] * (1.0 - LAM_INIT)
            o_ref[b, :, head(h)] = o.astype(BF16)


def _attention(lamv, subln_g, q, k, v, cache_k=None, cache_v=None):
    nb, n, _ = q.shape
    tq = min(ATTN_Q_TILE, n)
    has_cache = cache_k is not None
    past = cache_k.shape[2] if has_cache else 0
    nk = n + past
    bt = max(1, min(nb, ATTN_STEP_KEYS // nk))
    qtile = pl.BlockSpec((bt, tq, QK_WIDTH), lambda b, t: (b, t, 0))
    per_batch = pl.BlockSpec((bt, n, QK_WIDTH), lambda b, t: (b, 0, 0))
    in_specs = [_resident((4, HEAD_DIM)), _resident((1, V_DIM)), qtile, per_batch, per_batch]
    args = [lamv, subln_g, q, k, v]
    if has_cache:
        in_specs += [pl.BlockSpec((bt, QK_WIDTH, past), lambda b, t: (b, 0, 0)),
                     pl.BlockSpec((bt, past * N_HEADS, V_DIM), lambda b, t: (b, 0, 0))]
        args += [cache_k, cache_v]
    return pl.pallas_call(
        functools.partial(_attn_kernel, has_cache=has_cache, n=n, past=past, bt=bt),
        grid=(nb // bt, n // tq),
        in_specs=in_specs,
        out_specs=qtile,
        out_shape=jax.ShapeDtypeStruct((nb, n, ATTN_WIDTH), BF16),
        scratch_shapes=[pltpu.VMEM((bt, nk, QK_WIDTH), BF16),
                        pltpu.VMEM((bt, N_HEADS, V_DIM + ONES_ROWS, nk), BF16)],
        compiler_params=_params(2),
        name="diff_attention_cached" if has_cache else "diff_attention",
    )(*args)


POSDFT_ROWS = 512
POSDFT_STEP_ROWS = 2048


def _posdft_kernel(pc_ref, ps_ref, uc_ref, us_ref, o_ref):
    for i in range(o_ref.shape[0]):
        o = jnp.dot(pc_ref[...], uc_ref[i], preferred_element_type=F32)
        o = o + jnp.dot(ps_ref[...], us_ref[i], preferred_element_type=F32)
        o_ref[i] = o.astype(BF16)


def _position_mix(uc, us):
    nb, n, _ = uc.shape
    tr = min(POSDFT_ROWS, n)
    bt = max(1, min(nb, POSDFT_STEP_ROWS // n))
    pc, ps = _position_dft(n)
    pc = jnp.asarray(pc).astype(BF16)
    ps = jnp.asarray(ps).astype(BF16)
    rows = pl.BlockSpec((tr, n), lambda r, b: (r, 0))
    per_batch = pl.BlockSpec((bt, n, FOUR_WIDTH), lambda r, b: (b, 0, 0))
    return pl.pallas_call(
        _posdft_kernel,
        grid=(n // tr, nb // bt),
        in_specs=[rows, rows, per_batch, per_batch],
        out_specs=pl.BlockSpec((bt, tr, FOUR_WIDTH), lambda r, b: (b, r, 0)),
        out_shape=jax.ShapeDtypeStruct((nb, n, FOUR_WIDTH), BF16),
        compiler_params=_params(2),
        name="position_dft",
    )(pc, ps, uc, us)


OUT_TILE = 512
FF_CHUNK = 1024


def _out_mlp_kernel(x_ref, a_ref, f_ref, mod_ref, wo_ref, g2_ref, w1_ref, w2_ref, o_ref):
    m = mod_ref[0]
    g1 = m[:, 2 * D_MODEL:3 * D_MODEL]
    sh2 = m[:, 3 * D_MODEL:4 * D_MODEL]
    sc2 = m[:, 4 * D_MODEL:5 * D_MODEL]
    g2 = m[:, 5 * D_MODEL:6 * D_MODEL]
    mix = jnp.dot(a_ref[0], wo_ref[0:ATTN_WIDTH, :], preferred_element_type=F32)
    mix = mix + jnp.dot(f_ref[0], wo_ref[ATTN_WIDTH:, :], preferred_element_type=F32)
    x1 = x_ref[0] + g1 * mix
    ms = jnp.mean(x1 * x1, axis=-1, keepdims=True)
    h = x1 * lax.rsqrt(ms + EPS) * g2_ref[...] * (1.0 + sc2) + sh2
    hb = h.astype(BF16)
    acc = jnp.zeros(x1.shape, F32)
    for c in range(D_FF // FF_CHUNK):
        t = jnp.dot(hb, w1_ref[:, c * FF_CHUNK:(c + 1) * FF_CHUNK], preferred_element_type=F32)
        t = jnp.square(jnp.maximum(t, 0.0)).astype(BF16)
        acc = acc + jnp.dot(t, w2_ref[c * FF_CHUNK:(c + 1) * FF_CHUNK, :], preferred_element_type=F32)
    o_ref[0] = x1 + g2 * acc


def _output_mlp(x3, attn, four, mod3, mod_row0, w_out, norm2_g, w1, w2):
    nb, n, _ = x3.shape
    tm = OUT_TILE
    tile = lambda width: pl.BlockSpec((1, tm, width), lambda t, b: (b, t, 0))
    return pl.pallas_call(
        _out_mlp_kernel,
        grid=(n // tm, nb),
        in_specs=[tile(D_MODEL), tile(ATTN_WIDTH), tile(FOUR_WIDTH),
                  pl.BlockSpec((1, 1, 6 * D_MODEL), lambda t, b: (mod_row0 + b, 0, 0)),
                  _resident((ATTN_WIDTH + FOUR_WIDTH, D_MODEL)),
                  _resident((1, D_MODEL)),
                  _resident((D_MODEL, D_FF)),
                  _resident((D_FF, D_MODEL))],
        out_specs=tile(D_MODEL),
        out_shape=jax.ShapeDtypeStruct((nb, n, D_MODEL), F32),
        compiler_params=_params(2),
        name="output_mlp",
    )(x3, attn, four, mod3, w_out, norm2_g, w1, w2)


def kernel(x_prompt, x_sample, c, cache_k, cache_v, c_ctx, w_mod, b_mod, norm1_g, w_in, q_norm_g, k_norm_g,
           lambda_q1, lambda_k1, lambda_q2, lambda_k2, subln_g, w_four, w_out, norm2_g, w1, w2):
    batch, seq, _ = x_prompt.shape
    dec_batch, dec_seq, _ = x_sample.shape
    past = cache_k.shape[2]
    l = 0

    cvec = jnp.concatenate([c_ctx[None, :], c, jnp.zeros((MOD_ROWS - 1 - dec_batch, D_MODEL), F32)], axis=0)
    mod3 = _modulation(cvec, w_mod[l], b_mod[l][None, :]).reshape(MOD_ROWS, 1, 6 * D_MODEL)
    wcs = _fourier_weights(w_four[l])

    w_in_b = w_in[l].astype(BF16)
    w_out_b = w_out[l].astype(BF16)
    w1_b = w1[l].astype(BF16)
    w2_b = w2[l].astype(BF16)
    n1 = norm1_g[l][None, :]
    n2 = norm2_g[l][None, :]
    qg = jnp.tile(q_norm_g[l], QK_WIDTH // HEAD_DIM)[None, :]
    kg = jnp.tile(k_norm_g[l], QK_WIDTH // HEAD_DIM)[None, :]
    sg = subln_g[l][None, :]
    lamv = jnp.stack([lambda_q1[l], lambda_k1[l], lambda_q2[l], lambda_k2[l]])
    gm = jnp.asarray(_head_mean_matrix()).astype(BF16)
    rope_tabs = tuple(jnp.asarray(t) for t in _rope_tables(dec_seq))

    q, k, v, uc, us, kt32, v32 = _in_projection(x_prompt, mod3, 0, False, n1, w_in_b, qg, kg, gm, wcs, None, True)
    attn = _attention(lamv, sg, q, k, v)
    four = _position_mix(uc, us)
    flat = lambda t: t.reshape(1, batch * seq, t.shape[-1])
    yp = _output_mlp(flat(x_prompt), flat(attn), flat(four), mod3, 0, w_out_b, n2, w1_b, w2_b)
    y_prompt = yp.reshape(batch, seq, D_MODEL)
    new_cache_k = kt32.reshape(batch, N_HEADS, 2, HEAD_DIM, seq).transpose(0, 4, 1, 2, 3)[:, None]
    new_cache_v = v32.reshape(batch, 1, seq, N_HEADS, V_DIM)

    q, k, v, uc, us = _in_projection(x_sample, mod3, 1, True, n1, w_in_b, qg, kg, gm, wcs, rope_tabs, False)
    ck = cache_k[:, l].transpose(0, 2, 3, 4, 1).reshape(dec_batch, QK_WIDTH, past)
    cv = cache_v[:, l].reshape(dec_batch, past * N_HEADS, V_DIM)
    attn = _attention(lamv, sg, q, k, v, ck, cv)
    four = _position_mix(uc, us)
    y_sample = _output_mlp(x_sample, attn, four, mod3, 1, w_out_b, n2, w1_b, w2_b)

    return (y_prompt, y_sample, new_cache_k, new_cache_v)
```

```python
import functools
import math

import jax
import jax.numpy as jnp
import numpy as np
from jax import lax
from jax.experimental import pallas as pl
from jax.experimental.pallas import tpu as pltpu

D_MODEL = 1024
N_HEADS = 4
HEAD_DIM = 64
V_DIM = 2 * HEAD_DIM
QK_WIDTH = N_HEADS * 2 * HEAD_DIM
ATTN_WIDTH = N_HEADS * V_DIM
N_FOUR_GROUPS = 4
FOUR_GROUP = 128
FOUR_WIDTH = N_FOUR_GROUPS * FOUR_GROUP
PROJ_WIDTH = 2 * QK_WIDTH + ATTN_WIDTH + FOUR_WIDTH
D_FF = 4 * D_MODEL
GRID_W = 64
ROPE_BASE = 10000.0
ROPE_AXIS_DIM = HEAD_DIM // 2
EPS = 1e-6
LAM_INIT = 0.8 - 0.6 * math.exp(-0.3 * 0)
LOG2E = 1.4426950408889634
MOD_ROWS = 8

F32 = jnp.float32
BF16 = jnp.bfloat16

VMEM_LIMIT_BYTES = 56 * 1024 * 1024


def _params(n_axes):
    return pltpu.CompilerParams(dimension_semantics=("arbitrary",) * n_axes,
                                vmem_limit_bytes=VMEM_LIMIT_BYTES)


def _resident(shape):
    nd = len(shape)
    return pl.BlockSpec(shape, lambda *_: (0,) * nd, pipeline_mode=pl.Buffered(1))


@functools.lru_cache(maxsize=None)
def _channel_dft():
    idx = np.arange(FOUR_GROUP)
    ang = 2.0 * np.pi * ((idx[:, None] * idx[None, :]) % FOUR_GROUP) / FOUR_GROUP
    s = 1.0 / np.sqrt(FOUR_GROUP)
    return np.stack([np.cos(ang) * s, np.sin(ang) * s]).astype(np.float32)


@functools.lru_cache(maxsize=None)
def _position_dft(n):
    idx = np.arange(n)
    ang = 2.0 * np.pi * ((idx[:, None] * idx[None, :]) % n) / n
    s = 1.0 / np.sqrt(n)
    return (np.cos(ang) * s).astype(np.float32), (-np.sin(ang) * s).astype(np.float32)


@functools.lru_cache(maxsize=None)
def _rope_tables(n):
    rows = n // GRID_W
    row = np.repeat(np.arange(rows), GRID_W).astype(np.float64)
    col = np.tile(np.arange(GRID_W), rows).astype(np.float64)
    inv = ROPE_BASE ** (-np.arange(0, ROPE_AXIS_DIM, 2, dtype=np.float64) / ROPE_AXIS_DIM)
    d = np.arange(QK_WIDTH) % HEAD_DIM
    part = d // ROPE_AXIS_DIM
    i = d % ROPE_AXIS_DIM
    first = i < ROPE_AXIS_DIM // 2
    pos = np.where(part[None, :] == 0, row[:, None], col[:, None])
    ang = pos * inv[i % (ROPE_AXIS_DIM // 2)][None, :]
    cos = np.cos(ang)
    sin = np.where(first[None, :], -np.sin(ang), np.sin(ang))
    return cos.astype(np.float32), sin.astype(np.float32)


@functools.lru_cache(maxsize=None)
def _head_mean_matrix():
    g = np.kron(np.eye(QK_WIDTH // HEAD_DIM), np.ones((HEAD_DIM, HEAD_DIM))) / HEAD_DIM
    return g.astype(np.float32)


MOD_TILE = 1024


def _mod_kernel(c_ref, w_ref, b_ref, o_ref):
    c = c_ref[...]
    s = c / (1.0 + jnp.exp(-c))
    o_ref[...] = jnp.dot(s.astype(BF16), w_ref[...].astype(BF16),
                         preferred_element_type=F32) + b_ref[...]


def _modulation(cvec, w_mod, b_mod):
    n = w_mod.shape[1]
    return pl.pallas_call(
        _mod_kernel,
        grid=(n // MOD_TILE,),
        in_specs=[pl.BlockSpec((MOD_ROWS, D_MODEL), lambda j: (0, 0)),
                  pl.BlockSpec((D_MODEL, MOD_TILE), lambda j: (0, j)),
                  pl.BlockSpec((1, MOD_TILE), lambda j: (0, j))],
        out_specs=pl.BlockSpec((MOD_ROWS, MOD_TILE), lambda j: (0, j)),
        out_shape=jax.ShapeDtypeStruct((MOD_ROWS, n), F32),
        compiler_params=_params(1),
        name="modulation",
    )(cvec, w_mod, b_mod)


def _wcs_kernel(dft_ref, w_ref, o_ref):
    for g in range(N_FOUR_GROUPS):
        w = w_ref[g]
        wc = jnp.dot(dft_ref[0], w, precision=lax.Precision.HIGHEST, preferred_element_type=F32)
        ws = jnp.dot(dft_ref[1], w, precision=lax.Precision.HIGHEST, preferred_element_type=F32)
        o_ref[g] = jnp.concatenate([wc, ws], axis=-1).astype(BF16)


def _fourier_weights(w_four):
    dft = jnp.asarray(_channel_dft())
    return pl.pallas_call(
        _wcs_kernel,
        out_shape=jax.ShapeDtypeStruct((N_FOUR_GROUPS, FOUR_GROUP, 2 * FOUR_GROUP), BF16),
        name="fourier_weights",
    )(dft, w_four)


PROJ_TILE = 512


def _inproj_kernel(*refs, rope, emit_cache, bt, tn):
    it = iter(refs)
    x_ref, mod_ref, g1_ref, w_ref, qg_ref, kg_ref, gm_ref, wcs_ref = (next(it) for _ in range(8))
    if rope:
        cos_ref, sin_ref = next(it), next(it)
    q_ref, k_ref, v_ref, uc_ref, us_ref = (next(it) for _ in range(5))
    if emit_cache:
        k32_ref, v32_ref = next(it), next(it)

    x = x_ref[...].reshape(bt * tn, D_MODEL)
    m = mod_ref[0]
    sh1 = m[:, 0:D_MODEL]
    sc1 = m[:, D_MODEL:2 * D_MODEL]
    ms = jnp.mean(x * x, axis=-1, keepdims=True)
    y = x * lax.rsqrt(ms + EPS) * g1_ref[...]
    h = y * (1.0 + sc1) + sh1
    p = jnp.dot(h.astype(BF16), w_ref[...], preferred_element_type=F32)
    q = p[:, 0:QK_WIDTH]
    k = p[:, QK_WIDTH:2 * QK_WIDTH]
    v = p[:, 2 * QK_WIDTH:2 * QK_WIDTH + ATTN_WIDTH]

    def head_norm(t, g):
        msq = jnp.dot((t * t).astype(BF16), gm_ref[...], preferred_element_type=F32)
        return t * lax.rsqrt(msq + EPS) * g

    q = head_norm(q, qg_ref[...])
    k = head_norm(k, kg_ref[...])
    if emit_cache:
        for i in range(bt):
            rows = slice(i * tn, (i + 1) * tn)
            k32_ref[i] = k[rows, :].T
            for hd in range(N_HEADS):
                v32_ref[i, pl.ds(hd, tn, stride=N_HEADS), :] = v[rows, hd * V_DIM:(hd + 1) * V_DIM]

    if rope:
        lane = lax.broadcasted_iota(jnp.int32, q.shape, 1)
        first = (lane % ROPE_AXIS_DIM) < (ROPE_AXIS_DIM // 2)
        half = ROPE_AXIS_DIM // 2

        def rot(t):
            sw = jnp.where(first, pltpu.roll(t, QK_WIDTH - half, 1), pltpu.roll(t, half, 1))
            return t * cos_ref[...] + sw * sin_ref[...]

        q = rot(q)
        k = rot(k)

    blk = lambda t: t.reshape(bt, tn, t.shape[-1])
    q_ref[...] = blk((q * (LOG2E / math.sqrt(HEAD_DIM))).astype(BF16))
    k_ref[...] = blk(k.astype(BF16))
    v_ref[...] = blk(v.astype(BF16))

    f0 = 2 * QK_WIDTH + ATTN_WIDTH
    for g in range(N_FOUR_GROUPS):
        fg = p[:, f0 + g * FOUR_GROUP:f0 + (g + 1) * FOUR_GROUP].astype(BF16)
        u = jnp.dot(fg, wcs_ref[g], preferred_element_type=F32)
        uc_ref[:, :, g * FOUR_GROUP:(g + 1) * FOUR_GROUP] = blk(u[:, 0:FOUR_GROUP].astype(BF16))
        us_ref[:, :, g * FOUR_GROUP:(g + 1) * FOUR_GROUP] = blk(u[:, FOUR_GROUP:].astype(BF16))


def _in_projection(x3, mod3, mod_row0, per_batch_mod, norm_g, w_in, qg, kg, gm, wcs, rope_tabs, emit_cache):
    nb, n, _ = x3.shape
    tn = min(PROJ_TILE, n)
    bt = PROJ_TILE // tn
    assert per_batch_mod is False or bt == 1
    rope = rope_tabs is not None
    tile = lambda width: pl.BlockSpec((bt, tn, width), lambda t, b: (b, t, 0))
    mod_row = (lambda t, b: (mod_row0 + b, 0, 0)) if per_batch_mod else (lambda t, b: (mod_row0, 0, 0))
    in_specs = [tile(D_MODEL),
                pl.BlockSpec((1, 1, 6 * D_MODEL), mod_row),
                _resident((1, D_MODEL)),
                _resident((D_MODEL, PROJ_WIDTH)),
                _resident((1, QK_WIDTH)),
                _resident((1, QK_WIDTH)),
                _resident((QK_WIDTH, QK_WIDTH)),
                _resident((N_FOUR_GROUPS, FOUR_GROUP, 2 * FOUR_GROUP))]
    args = [x3, mod3, norm_g, w_in, qg, kg, gm, wcs]
    if rope:
        in_specs += [pl.BlockSpec((tn, QK_WIDTH), lambda t, b: (t, 0))] * 2
        args += list(rope_tabs)
    n_bf = 5
    out_shape = [jax.ShapeDtypeStruct((nb, n, QK_WIDTH), BF16)] * n_bf
    out_specs = [tile(QK_WIDTH)] * n_bf
    if emit_cache:
        assert tn == n
        out_shape += [jax.ShapeDtypeStruct((nb, QK_WIDTH, n), F32),
                      jax.ShapeDtypeStruct((nb, n * N_HEADS, V_DIM), F32)]
        out_specs += [pl.BlockSpec((bt, QK_WIDTH, n), lambda t, b: (b, 0, 0)),
                      pl.BlockSpec((bt, n * N_HEADS, V_DIM), lambda t, b: (b, 0, 0))]
    return pl.pallas_call(
        functools.partial(_inproj_kernel, rope=rope, emit_cache=emit_cache, bt=bt, tn=tn),
        grid=(n // tn, nb // bt),
        in_specs=in_specs,
        out_specs=out_specs,
        out_shape=out_shape,
        compiler_params=_params(2),
        name="in_projection_rope" if rope else "in_projection",
    )(*args)


ATTN_Q_TILE = 512


KEY_CHUNK = 1280
ONES_ROWS = 16


PREFETCH_KEYS = 1024
ATTN_STEP_KEYS = 1024


def _attn_kernel(*refs, has_cache, n, past, bt):
    it = iter(refs)
    lam_ref, sg_ref, q_ref, k_ref, v_ref = (next(it) for _ in range(5))
    if has_cache:
        ck_ref, cv_ref = next(it), next(it)
    o_ref = next(it)
    kall_ref, vt_ref = next(it), next(it)
    nk = n + past
    tq = q_ref.shape[1]

    @pl.when(pl.program_id(1) == 0)
    def _per_batch_setup():
        ones = jnp.ones((ONES_ROWS, nk), BF16)
        for b in range(bt):
            if has_cache:
                kall_ref[b, 0:past, :] = ck_ref[b].T.astype(BF16)
            kall_ref[b, past:nk, :] = k_ref[b]
            for h in range(N_HEADS):
                if has_cache:
                    cvh = cv_ref[b, pl.ds(h, past, stride=N_HEADS), :]
                    vt_ref[b, h, 0:V_DIM, 0:past] = cvh.T.astype(BF16)
                vh = v_ref[b, :, h * V_DIM:(h + 1) * V_DIM].astype(F32)
                vt_ref[b, h, 0:V_DIM, past:nk] = vh.T.astype(BF16)
                vt_ref[b, h, V_DIM:V_DIM + ONES_ROWS, :] = ones

    lv = lam_ref[...]
    l1 = jnp.exp(jnp.sum(lv[0:1] * lv[1:2], axis=-1, keepdims=True))
    l2 = jnp.exp(jnp.sum(lv[2:3] * lv[3:4], axis=-1, keepdims=True))
    lam = l1 - l2 + LAM_INIT

    feat = lax.broadcasted_iota(jnp.int32, (V_DIM, tq), 0)
    half0 = feat < HEAD_DIM

    kc = min(KEY_CHUNK, nk)
    chunks = [slice(c * kc, (c + 1) * kc) for c in range(nk // kc)]
    chains = [(b, h, mhalf) for b in range(bt) for h in range(N_HEADS) for mhalf in range(2)]
    head = lambda h: slice(h * V_DIM, (h + 1) * V_DIM)
    qts = {}

    def scores(chain):
        b, h, mhalf = chain
        if (b, h) not in qts:
            qts[b, h] = q_ref[b, :, head(h)].astype(F32).T
        qm = jnp.where(half0 if mhalf == 0 else jnp.logical_not(half0), qts[b, h], 0.0).astype(BF16)

        def one(rows):
            s = jnp.dot(kall_ref[b, rows, head(h)], qm, preferred_element_type=F32)
            return s, jnp.max(s, axis=0, keepdims=True)
        return one

    depth = max(1, min(len(chains), PREFETCH_KEYS // nk))
    queue = [[scores(chain)(rows) for rows in chunks] for chain in chains[:depth]]
    acc = [None, None]
    for i, (b, h, mhalf) in enumerate(chains):
        cur = queue.pop(0)
        mx = functools.reduce(jnp.maximum, [m for _, m in cur])
        nxt = scores(chains[i + depth]) if i + depth < len(chains) else None
        issued = []
        a = None
        for c, rows in enumerate(chunks):
            if nxt is not None:
                issued.append(nxt(rows))
            e = jnp.exp2(cur[c][0] - mx).astype(BF16)
            part = jnp.dot(vt_ref[b, h, :, rows], e, preferred_element_type=F32)
            a = part if a is None else a + part
        if nxt is not None:
            queue.append(issued)
        acc[mhalf] = a
        if mhalf == 1:
            a0, a1 = acc
            r0 = 1.0 / a0[V_DIM:V_DIM + 1, :]
            r1 = lam / a1[V_DIM:V_DIM + 1, :]
            o = (a0[0:V_DIM, :] * r0 - a1[0:V_DIM, :] * r1).T
            ms = jnp.mean(o * o, axis=-1, keepdims=True)
            o = o * lax.rsqrt(ms + EPS) * sg_ref[...] * (1.0 - LAM_INIT)
            o_ref[b, :, head(h)] = o.astype(BF16)


def _attention(lamv, subln_g, q, k, v, cache_k=None, cache_v=None):
    nb, n, _ = q.shape
    tq = min(ATTN_Q_TILE, n)
    has_cache = cache_k is not None
    past = cache_k.shape[2] if has_cache else 0
    nk = n + past
    bt = max(1, min(nb, ATTN_STEP_KEYS // nk))
    qtile = pl.BlockSpec((bt, tq, QK_WIDTH), lambda b, t: (b, t, 0))
    per_batch = pl.BlockSpec((bt, n, QK_WIDTH), lambda b, t: (b, 0, 0))
    in_specs = [_resident((4, HEAD_DIM)), _resident((1, V_DIM)), qtile, per_batch, per_batch]
    args = [lamv, subln_g, q, k, v]
    if has_cache:
        in_specs += [pl.BlockSpec((bt, QK_WIDTH, past), lambda b, t: (b, 0, 0)),
                     pl.BlockSpec((bt, past * N_HEADS, V_DIM), lambda b, t: (b, 0, 0))]
        args += [cache_k, cache_v]
    return pl.pallas_call(
        functools.partial(_attn_kernel, has_cache=has_cache, n=n, past=past, bt=bt),
        grid=(nb // bt, n // tq),
        in_specs=in_specs,
        out_specs=qtile,
        out_shape=jax.ShapeDtypeStruct((nb, n, ATTN_WIDTH), BF16),
        scratch_shapes=[pltpu.VMEM((bt, nk, QK_WIDTH), BF16),
                        pltpu.VMEM((bt, N_HEADS, V_DIM + ONES_ROWS, nk), BF16)],
        compiler_params=_params(2),
        name="diff_attention_cached" if has_cache else "diff_attention",
    )(*args)


POSDFT_ROWS = 512
POSDFT_STEP_ROWS = 2048


def _posdft_kernel(pc_ref, ps_ref, uc_ref, us_ref, o_ref):
    for i in range(o_ref.shape[0]):
        o = jnp.dot(pc_ref[...], uc_ref[i], preferred_element_type=F32)
        o = o + jnp.dot(ps_ref[...], us_ref[i], preferred_element_type=F32)
        o_ref[i] = o.astype(BF16)


def _position_mix(uc, us):
    nb, n, _ = uc.shape
    tr = min(POSDFT_ROWS, n)
    bt = max(1, min(nb, POSDFT_STEP_ROWS // n))
    pc, ps = _position_dft(n)
    pc = jnp.asarray(pc).astype(BF16)
    ps = jnp.asarray(ps).astype(BF16)
    rows = pl.BlockSpec((tr, n), lambda r, b: (r, 0))
    per_batch = pl.BlockSpec((bt, n, FOUR_WIDTH), lambda r, b: (b, 0, 0))
    return pl.pallas_call(
        _posdft_kernel,
        grid=(n // tr, nb // bt),
        in_specs=[rows, rows, per_batch, per_batch],
        out_specs=pl.BlockSpec((bt, tr, FOUR_WIDTH), lambda r, b: (b, r, 0)),
        out_shape=jax.ShapeDtypeStruct((nb, n, FOUR_WIDTH), BF16),
        compiler_params=_params(2),
        name="position_dft",
    )(pc, ps, uc, us)


OUT_TILE = 512
FF_CHUNK = 1024


def _out_mlp_kernel(x_ref, a_ref, f_ref, mod_ref, wo_ref, g2_ref, w1_ref, w2_ref, o_ref):
    m = mod_ref[0]
    g1 = m[:, 2 * D_MODEL:3 * D_MODEL]
    sh2 = m[:, 3 * D_MODEL:4 * D_MODEL]
    sc2 = m[:, 4 * D_MODEL:5 * D_MODEL]
    g2 = m[:, 5 * D_MODEL:6 * D_MODEL]
    mix = jnp.dot(a_ref[0], wo_ref[0:ATTN_WIDTH, :], preferred_element_type=F32)
    mix = mix + jnp.dot(f_ref[0], wo_ref[ATTN_WIDTH:, :], preferred_element_type=F32)
    x1 = x_ref[0] + g1 * mix
    ms = jnp.mean(x1 * x1, axis=-1, keepdims=True)
    h = x1 * lax.rsqrt(ms + EPS) * g2_ref[...] * (1.0 + sc2) + sh2
    hb = h.astype(BF16)
    acc = jnp.zeros(x1.shape, F32)
    for c in range(D_FF // FF_CHUNK):
        t = jnp.dot(hb, w1_ref[:, c * FF_CHUNK:(c + 1) * FF_CHUNK], preferred_element_type=F32)
        t = jnp.square(jnp.maximum(t, 0.0)).astype(BF16)
        acc = acc + jnp.dot(t, w2_ref[c * FF_CHUNK:(c + 1) * FF_CHUNK, :], preferred_element_type=F32)
    o_ref[0] = x1 + g2 * acc


def _output_mlp(x3, attn, four, mod3, mod_row0, w_out, norm2_g, w1, w2):
    nb, n, _ = x3.shape
    tm = OUT_TILE
    tile = lambda width: pl.BlockSpec((1, tm, width), lambda t, b: (b, t, 0))
    return pl.pallas_call(
        _out_mlp_kernel,
        grid=(n // tm, nb),
        in_specs=[tile(D_MODEL), tile(ATTN_WIDTH), tile(FOUR_WIDTH),
                  pl.BlockSpec((1, 1, 6 * D_MODEL), lambda t, b: (mod_row0 + b, 0, 0)),
                  _resident((ATTN_WIDTH + FOUR_WIDTH, D_MODEL)),
                  _resident((1, D_MODEL)),
                  _resident((D_MODEL, D_FF)),
                  _resident((D_FF, D_MODEL))],
        out_specs=tile(D_MODEL),
        out_shape=jax.ShapeDtypeStruct((nb, n, D_MODEL), F32),
        compiler_params=_params(2),
        name="output_mlp",
    )(x3, attn, four, mod3, w_out, norm2_g, w1, w2)


def kernel(x_prompt, x_sample, c, cache_k, cache_v, c_ctx, w_mod, b_mod, norm1_g, w_in, q_norm_g, k_norm_g,
           lambda_q1, lambda_k1, lambda_q2, lambda_k2, subln_g, w_four, w_out, norm2_g, w1, w2):
    batch, seq, _ = x_prompt.shape
    dec_batch, dec_seq, _ = x_sample.shape
    past = cache_k.shape[2]
    l = 0

    cvec = jnp.concatenate([c_ctx[None, :], c, jnp.zeros((MOD_ROWS - 1 - dec_batch, D_MODEL), F32)], axis=0)
    mod3 = _modulation(cvec, w_mod[l], b_mod[l][None, :]).reshape(MOD_ROWS, 1, 6 * D_MODEL)
    wcs = _fourier_weights(w_four[l])

    w_in_b = w_in[l].astype(BF16)
    w_out_b = w_out[l].astype(BF16)
    w1_b = w1[l].astype(BF16)
    w2_b = w2[l].astype(BF16)
    n1 = norm1_g[l][None, :]
    n2 = norm2_g[l][None, :]
    qg = jnp.tile(q_norm_g[l], QK_WIDTH // HEAD_DIM)[None, :]
    kg = jnp.tile(k_norm_g[l], QK_WIDTH // HEAD_DIM)[None, :]
    sg = subln_g[l][None, :]
    lamv = jnp.stack([lambda_q1[l], lambda_k1[l], lambda_q2[l], lambda_k2[l]])
    gm = jnp.asarray(_head_mean_matrix()).astype(BF16)
    rope_tabs = tuple(jnp.asarray(t) for t in _rope_tables(dec_seq))

    q, k, v, uc, us, kt32, v32 = _in_projection(x_prompt, mod3, 0, False, n1, w_in_b, qg, kg, gm, wcs, None, True)
    attn = _attention(lamv, sg, q, k, v)
    four = _position_mix(uc, us)
    flat = lambda t: t.reshape(1, batch * seq, t.shape[-1])
    yp = _output_mlp(flat(x_prompt), flat(attn), flat(four), mod3, 0, w_out_b, n2, w1_b, w2_b)
    y_prompt = yp.reshape(batch, seq, D_MODEL)
    new_cache_k = kt32.reshape(batch, N_HEADS, 2, HEAD_DIM, seq).transpose(0, 4, 1, 2, 3)[:, None]
    new_cache_v = v32.reshape(batch, 1, seq, N_HEADS, V_DIM)

    q, k, v, uc, us = _in_projection(x_sample, mod3, 1, True, n1, w_in_b, qg, kg, gm, wcs, rope_tabs, False)
    ck = cache_k[:, l].transpose(0, 2, 3, 4, 1).reshape(dec_batch, QK_WIDTH, past)
    cv = cache_v[:, l].reshape(dec_batch, past * N_HEADS, V_DIM)
    attn = _attention(lamv, sg, q, k, v, ck, cv)
    four = _position_mix(uc, us)
    y_sample = _output_mlp(x_sample, attn, four, mod3, 1, w_out_b, n2, w1_b, w2_b)

    return (y_prompt, y_sample, new_cache_k, new_cache_v)
```

```python
import functools
import math

import jax
import jax.numpy as jnp
import numpy as np
from jax import lax
from jax.experimental import pallas as pl
from jax.experimental.pallas import tpu as pltpu

D_MODEL = 1024
N_HEADS = 4
HEAD_DIM = 64
V_DIM = 2 * HEAD_DIM
QK_WIDTH = N_HEADS * 2 * HEAD_DIM
ATTN_WIDTH = N_HEADS * V_DIM
N_FOUR_GROUPS = 4
FOUR_GROUP = 128
FOUR_WIDTH = N_FOUR_GROUPS * FOUR_GROUP
PROJ_WIDTH = 2 * QK_WIDTH + ATTN_WIDTH + FOUR_WIDTH
D_FF = 4 * D_MODEL
GRID_W = 64
ROPE_BASE = 10000.0
ROPE_AXIS_DIM = HEAD_DIM // 2
EPS = 1e-6
LAM_INIT = 0.8 - 0.6 * math.exp(-0.3 * 0)
LOG2E = 1.4426950408889634
MOD_ROWS = 8

F32 = jnp.float32
BF16 = jnp.bfloat16

VMEM_LIMIT_BYTES = 56 * 1024 * 1024


def _params(n_axes):
    return pltpu.CompilerParams(dimension_semantics=("arbitrary",) * n_axes,
                                vmem_limit_bytes=VMEM_LIMIT_BYTES)


def _resident(shape):
    nd = len(shape)
    return pl.BlockSpec(shape, lambda *_: (0,) * nd, pipeline_mode=pl.Buffered(1))


@functools.lru_cache(maxsize=None)
def _channel_dft():
    idx = np.arange(FOUR_GROUP)
    ang = 2.0 * np.pi * ((idx[:, None] * idx[None, :]) % FOUR_GROUP) / FOUR_GROUP
    s = 1.0 / np.sqrt(FOUR_GROUP)
    return np.stack([np.cos(ang) * s, np.sin(ang) * s]).astype(np.float32)


@functools.lru_cache(maxsize=None)
def _position_dft(n):
    idx = np.arange(n)
    ang = 2.0 * np.pi * ((idx[:, None] * idx[None, :]) % n) / n
    s = 1.0 / np.sqrt(n)
    return (np.cos(ang) * s).astype(np.float32), (-np.sin(ang) * s).astype(np.float32)


@functools.lru_cache(maxsize=None)
def _rope_tables(n):
    rows = n // GRID_W
    row = np.repeat(np.arange(rows), GRID_W).astype(np.float64)
    col = np.tile(np.arange(GRID_W), rows).astype(np.float64)
    inv = ROPE_BASE ** (-np.arange(0, ROPE_AXIS_DIM, 2, dtype=np.float64) / ROPE_AXIS_DIM)
    d = np.arange(QK_WIDTH) % HEAD_DIM
    part = d // ROPE_AXIS_DIM
    i = d % ROPE_AXIS_DIM
    first = i < ROPE_AXIS_DIM // 2
    pos = np.where(part[None, :] == 0, row[:, None], col[:, None])
    ang = pos * inv[i % (ROPE_AXIS_DIM // 2)][None, :]
    cos = np.cos(ang)
    sin = np.where(first[None, :], -np.sin(ang), np.sin(ang))
    return cos.astype(np.float32), sin.astype(np.float32)


@functools.lru_cache(maxsize=None)
def _head_mean_matrix():
    g = np.kron(np.eye(QK_WIDTH // HEAD_DIM), np.ones((HEAD_DIM, HEAD_DIM))) / HEAD_DIM
    return g.astype(np.float32)


MOD_TILE = 1024


def _mod_kernel(c_ref, w_ref, b_ref, o_ref):
    c = c_ref[...]
    s = c / (1.0 + jnp.exp(-c))
    o_ref[...] = jnp.dot(s.astype(BF16), w_ref[...].astype(BF16),
                         preferred_element_type=F32) + b_ref[...]


def _modulation(cvec, w_mod, b_mod):
    n = w_mod.shape[1]
    return pl.pallas_call(
        _mod_kernel,
        grid=(n // MOD_TILE,),
        in_specs=[pl.BlockSpec((MOD_ROWS, D_MODEL), lambda j: (0, 0)),
                  pl.BlockSpec((D_MODEL, MOD_TILE), lambda j: (0, j)),
                  pl.BlockSpec((1, MOD_TILE), lambda j: (0, j))],
        out_specs=pl.BlockSpec((MOD_ROWS, MOD_TILE), lambda j: (0, j)),
        out_shape=jax.ShapeDtypeStruct((MOD_ROWS, n), F32),
        compiler_params=_params(1),
        name="modulation",
    )(cvec, w_mod, b_mod)


def _wcs_kernel(dft_ref, w_ref, o_ref):
    for g in range(N_FOUR_GROUPS):
        w = w_ref[g]
        wc = jnp.dot(dft_ref[0], w, precision=lax.Precision.HIGHEST, preferred_element_type=F32)
        ws = jnp.dot(dft_ref[1], w, precision=lax.Precision.HIGHEST, preferred_element_type=F32)
        o_ref[g] = jnp.concatenate([wc, ws], axis=-1).astype(BF16)


def _fourier_weights(w_four):
    dft = jnp.asarray(_channel_dft())
    return pl.pallas_call(
        _wcs_kernel,
        out_shape=jax.ShapeDtypeStruct((N_FOUR_GROUPS, FOUR_GROUP, 2 * FOUR_GROUP), BF16),
        name="fourier_weights",
    )(dft, w_four)


PROJ_TILE = 512


def _inproj_kernel(*refs, rope, emit_cache, bt, tn):
    it = iter(refs)
    x_ref, mod_ref, g1_ref, w_ref, qg_ref, kg_ref, gm_ref, wcs_ref = (next(it) for _ in range(8))
    if rope:
        cos_ref, sin_ref = next(it), next(it)
    q_ref, k_ref, v_ref, uc_ref, us_ref = (next(it) for _ in range(5))
    if emit_cache:
        k32_ref, v32_ref = next(it), next(it)

    x = x_ref[...].reshape(bt * tn, D_MODEL)
    m = mod_ref[0]
    sh1 = m[:, 0:D_MODEL]
    sc1 = m[:, D_MODEL:2 * D_MODEL]
    ms = jnp.mean(x * x, axis=-1, keepdims=True)
    y = x * lax.rsqrt(ms + EPS) * g1_ref[...]
    h = y * (1.0 + sc1) + sh1
    p = jnp.dot(h.astype(BF16), w_ref[...], preferred_element_type=F32)
    q = p[:, 0:QK_WIDTH]
    k = p[:, QK_WIDTH:2 * QK_WIDTH]
    v = p[:, 2 * QK_WIDTH:2 * QK_WIDTH + ATTN_WIDTH]

    def head_norm(t, g):
        msq = jnp.dot((t * t).astype(BF16), gm_ref[...], preferred_element_type=F32)
        return t * lax.rsqrt(msq + EPS) * g

    q = head_norm(q, qg_ref[...])
    k = head_norm(k, kg_ref[...])
    if emit_cache:
        for i in range(bt):
            rows = slice(i * tn, (i + 1) * tn)
            k32_ref[i] = k[rows, :].T
            for hd in range(N_HEADS):
                v32_ref[i, pl.ds(hd, tn, stride=N_HEADS), :] = v[rows, hd * V_DIM:(hd + 1) * V_DIM]

    if rope:
        lane = lax.broadcasted_iota(jnp.int32, q.shape, 1)
        first = (lane % ROPE_AXIS_DIM) < (ROPE_AXIS_DIM // 2)
        half = ROPE_AXIS_DIM // 2

        def rot(t):
            sw = jnp.where(first, pltpu.roll(t, QK_WIDTH - half, 1), pltpu.roll(t, half, 1))
            return t * cos_ref[...] + sw * sin_ref[...]

        q = rot(q)
        k = rot(k)

    blk = lambda t: t.reshape(bt, tn, t.shape[-1])
    q_ref[...] = blk((q * (LOG2E / math.sqrt(HEAD_DIM))).astype(BF16))
    k_ref[...] = blk(k.astype(BF16))
    v_ref[...] = blk(v.astype(BF16))

    f0 = 2 * QK_WIDTH + ATTN_WIDTH
    for g in range(N_FOUR_GROUPS):
        fg = p[:, f0 + g * FOUR_GROUP:f0 + (g + 1) * FOUR_GROUP].astype(BF16)
        u = jnp.dot(fg, wcs_ref[g], preferred_element_type=F32)
        uc_ref[:, :, g * FOUR_GROUP:(g + 1) * FOUR_GROUP] = blk(u[:, 0:FOUR_GROUP].astype(BF16))
        us_ref[:, :, g * FOUR_GROUP:(g + 1) * FOUR_GROUP] = blk(u[:, FOUR_GROUP:].astype(BF16))


def _in_projection(x3, mod3, mod_row0, per_batch_mod, norm_g, w_in, qg, kg, gm, wcs, rope_tabs, emit_cache):
    nb, n, _ = x3.shape
    tn = min(PROJ_TILE, n)
    bt = PROJ_TILE // tn
    assert per_batch_mod is False or bt == 1
    rope = rope_tabs is not None
    tile = lambda width: pl.BlockSpec((bt, tn, width), lambda t, b: (b, t, 0))
    mod_row = (lambda t, b: (mod_row0 + b, 0, 0)) if per_batch_mod else (lambda t, b: (mod_row0, 0, 0))
    in_specs = [tile(D_MODEL),
                pl.BlockSpec((1, 1, 6 * D_MODEL), mod_row),
                _resident((1, D_MODEL)),
                _resident((D_MODEL, PROJ_WIDTH)),
                _resident((1, QK_WIDTH)),
                _resident((1, QK_WIDTH)),
                _resident((QK_WIDTH, QK_WIDTH)),
                _resident((N_FOUR_GROUPS, FOUR_GROUP, 2 * FOUR_GROUP))]
    args = [x3, mod3, norm_g, w_in, qg, kg, gm, wcs]
    if rope:
        in_specs += [pl.BlockSpec((tn, QK_WIDTH), lambda t, b: (t, 0))] * 2
        args += list(rope_tabs)
    n_bf = 5
    out_shape = [jax.ShapeDtypeStruct((nb, n, QK_WIDTH), BF16)] * n_bf
    out_specs = [tile(QK_WIDTH)] * n_bf
    if emit_cache:
        assert tn == n
        out_shape += [jax.ShapeDtypeStruct((nb, QK_WIDTH, n), F32),
                      jax.ShapeDtypeStruct((nb, n * N_HEADS, V_DIM), F32)]
        out_specs += [pl.BlockSpec((bt, QK_WIDTH, n), lambda t, b: (b, 0, 0)),
                      pl.BlockSpec((bt, n * N_HEADS, V_DIM), lambda t, b: (b, 0, 0))]
    return pl.pallas_call(
        functools.partial(_inproj_kernel, rope=rope, emit_cache=emit_cache, bt=bt, tn=tn),
        grid=(n // tn, nb // bt),
        in_specs=in_specs,
        out_specs=out_specs,
        out_shape=out_shape,
        compiler_params=_params(2),
        name="in_projection_rope" if rope else "in_projection",
    )(*args)


ATTN_Q_TILE = 512


KEY_CHUNK = 1280
ONES_ROWS = 16


FOURIER_PIECE_ROWS = 128
PREFETCH_KEYS = 1024
ATTN_STEP_KEYS = 1024


def _attn_kernel(*refs, has_cache, n, past, bt):
    it = iter(refs)
    lam_ref, sg_ref, q_ref, k_ref, v_ref = (next(it) for _ in range(5))
    if has_cache:
        ck_ref, cv_ref = next(it), next(it)
    pc_ref, ps_ref, uc_ref, us_ref = (next(it) for _ in range(4))
    o_ref, four_ref = next(it), next(it)
    kall_ref, vt_ref = next(it), next(it)
    nk = n + past
    tq = q_ref.shape[1]

    @pl.when(pl.program_id(1) == 0)
    def _per_batch_setup():
        ones = jnp.ones((ONES_ROWS, nk), BF16)
        for b in range(bt):
            if has_cache:
                kall_ref[b, 0:past, :] = ck_ref[b].T.astype(BF16)
            kall_ref[b, past:nk, :] = k_ref[b]
            for h in range(N_HEADS):
                if has_cache:
                    cvh = cv_ref[b, pl.ds(h, past, stride=N_HEADS), :]
                    vt_ref[b, h, 0:V_DIM, 0:past] = cvh.T.astype(BF16)
                vh = v_ref[b, :, h * V_DIM:(h + 1) * V_DIM].astype(F32)
                vt_ref[b, h, 0:V_DIM, past:nk] = vh.T.astype(BF16)
                vt_ref[b, h, V_DIM:V_DIM + ONES_ROWS, :] = ones

    lv = lam_ref[...]
    l1 = jnp.exp(jnp.sum(lv[0:1] * lv[1:2], axis=-1, keepdims=True))
    l2 = jnp.exp(jnp.sum(lv[2:3] * lv[3:4], axis=-1, keepdims=True))
    lam = l1 - l2 + LAM_INIT

    feat = lax.broadcasted_iota(jnp.int32, (V_DIM, tq), 0)
    half0 = feat < HEAD_DIM

    kc = min(KEY_CHUNK, nk)
    chunks = [slice(c * kc, (c + 1) * kc) for c in range(nk // kc)]
    chains = [(b, h, mhalf) for b in range(bt) for h in range(N_HEADS) for mhalf in range(2)]
    head = lambda h: slice(h * V_DIM, (h + 1) * V_DIM)
    qts = {}

    def scores(chain):
        b, h, mhalf = chain
        if (b, h) not in qts:
            qts[b, h] = q_ref[b, :, head(h)].astype(F32).T
        qm = jnp.where(half0 if mhalf == 0 else jnp.logical_not(half0), qts[b, h], 0.0).astype(BF16)

        def one(rows):
            s = jnp.dot(kall_ref[b, rows, head(h)], qm, preferred_element_type=F32)
            return s, jnp.max(s, axis=0, keepdims=True)
        return one

    def fourier_piece(b, rows):
        def emit():
            o = jnp.dot(pc_ref[rows, :], uc_ref[b], preferred_element_type=F32)
            o = o + jnp.dot(ps_ref[rows, :], us_ref[b], preferred_element_type=F32)
            four_ref[b, rows, :] = o.astype(BF16)
        return emit

    fr = min(FOURIER_PIECE_ROWS, tq)
    pieces = [fourier_piece(b, slice(r * fr, (r + 1) * fr)) for b in range(bt) for r in range(tq // fr)]
    piece_every = len(chains) // len(pieces)

    depth = max(1, min(len(chains), PREFETCH_KEYS // nk))
    queue = [[scores(chain)(rows) for rows in chunks] for chain in chains[:depth]]
    acc = [None, None]
    for i, (b, h, mhalf) in enumerate(chains):
        cur = queue.pop(0)
        mx = functools.reduce(jnp.maximum, [m for _, m in cur])
        nxt = scores(chains[i + depth]) if i + depth < len(chains) else None
        issued = []
        a = None
        for c, rows in enumerate(chunks):
            if nxt is not None:
                issued.append(nxt(rows))
            e = jnp.exp2(cur[c][0] - mx).astype(BF16)
            part = jnp.dot(vt_ref[b, h, :, rows], e, preferred_element_type=F32)
            a = part if a is None else a + part
        if nxt is not None:
            queue.append(issued)
        if i % piece_every == 0:
            pieces[i // piece_every]()
        acc[mhalf] = a
        if mhalf == 1:
            a0, a1 = acc
            r0 = 1.0 / a0[V_DIM:V_DIM + 1, :]
            r1 = lam / a1[V_DIM:V_DIM + 1, :]
            o = (a0[0:V_DIM, :] * r0 - a1[0:V_DIM, :] * r1).T
            ms = jnp.mean(o * o, axis=-1, keepdims=True)
            o = o * lax.rsqrt(ms + EPS) * sg_ref[...] * (1.0 - LAM_INIT)
            o_ref[b, :, head(h)] = o.astype(BF16)


def _attention(lamv, subln_g, q, k, v, uc, us, cache_k=None, cache_v=None):
    nb, n, _ = q.shape
    tq = min(ATTN_Q_TILE, n)
    has_cache = cache_k is not None
    past = cache_k.shape[2] if has_cache else 0
    nk = n + past
    bt = max(1, min(nb, ATTN_STEP_KEYS // nk))
    qtile = pl.BlockSpec((bt, tq, QK_WIDTH), lambda b, t: (b, t, 0))
    per_batch = pl.BlockSpec((bt, n, QK_WIDTH), lambda b, t: (b, 0, 0))
    in_specs = [_resident((4, HEAD_DIM)), _resident((1, V_DIM)), qtile, per_batch, per_batch]
    args = [lamv, subln_g, q, k, v]
    if has_cache:
        in_specs += [pl.BlockSpec((bt, QK_WIDTH, past), lambda b, t: (b, 0, 0)),
                     pl.BlockSpec((bt, past * N_HEADS, V_DIM), lambda b, t: (b, 0, 0))]
        args += [cache_k, cache_v]
    pc, ps = _position_dft(n)
    dft_rows = pl.BlockSpec((tq, n), lambda b, t: (t, 0))
    in_specs += [dft_rows, dft_rows, per_batch, per_batch]
    args += [jnp.asarray(pc).astype(BF16), jnp.asarray(ps).astype(BF16), uc, us]
    return pl.pallas_call(
        functools.partial(_attn_kernel, has_cache=has_cache, n=n, past=past, bt=bt),
        grid=(nb // bt, n // tq),
        in_specs=in_specs,
        out_specs=[qtile, qtile],
        out_shape=[jax.ShapeDtypeStruct((nb, n, ATTN_WIDTH), BF16),
                   jax.ShapeDtypeStruct((nb, n, FOUR_WIDTH), BF16)],
        scratch_shapes=[pltpu.VMEM((bt, nk, QK_WIDTH), BF16),
                        pltpu.VMEM((bt, N_HEADS, V_DIM + ONES_ROWS, nk), BF16)],
        compiler_params=_params(2),
        name="diff_attention_cached" if has_cache else "diff_attention",
    )(*args)


POSDFT_ROWS = 512
POSDFT_STEP_ROWS = 2048


def _posdft_kernel(pc_ref, ps_ref, uc_ref, us_ref, o_ref):
    for i in range(o_ref.shape[0]):
        o = jnp.dot(pc_ref[...], uc_ref[i], preferred_element_type=F32)
        o = o + jnp.dot(ps_ref[...], us_ref[i], preferred_element_type=F32)
        o_ref[i] = o.astype(BF16)


def _position_mix(uc, us):
    nb, n, _ = uc.shape
    tr = min(POSDFT_ROWS, n)
    bt = max(1, min(nb, POSDFT_STEP_ROWS // n))
    pc, ps = _position_dft(n)
    pc = jnp.asarray(pc).astype(BF16)
    ps = jnp.asarray(ps).astype(BF16)
    rows = pl.BlockSpec((tr, n), lambda r, b: (r, 0))
    per_batch = pl.BlockSpec((bt, n, FOUR_WIDTH), lambda r, b: (b, 0, 0))
    return pl.pallas_call(
        _posdft_kernel,
        grid=(n // tr, nb // bt),
        in_specs=[rows, rows, per_batch, per_batch],
        out_specs=pl.BlockSpec((bt, tr, FOUR_WIDTH), lambda r, b: (b, r, 0)),
        out_shape=jax.ShapeDtypeStruct((nb, n, FOUR_WIDTH), BF16),
        compiler_params=_params(2),
        name="position_dft",
    )(pc, ps, uc, us)


OUT_TILE = 512
FF_CHUNK = 1024


def _out_mlp_kernel(x_ref, a_ref, f_ref, mod_ref, wo_ref, g2_ref, w1_ref, w2_ref, o_ref):
    m = mod_ref[0]
    g1 = m[:, 2 * D_MODEL:3 * D_MODEL]
    sh2 = m[:, 3 * D_MODEL:4 * D_MODEL]
    sc2 = m[:, 4 * D_MODEL:5 * D_MODEL]
    g2 = m[:, 5 * D_MODEL:6 * D_MODEL]
    mix = jnp.dot(a_ref[0], wo_ref[0:ATTN_WIDTH, :], preferred_element_type=F32)
    mix = mix + jnp.dot(f_ref[0], wo_ref[ATTN_WIDTH:, :], preferred_element_type=F32)
    x1 = x_ref[0] + g1 * mix
    ms = jnp.mean(x1 * x1, axis=-1, keepdims=True)
    h = x1 * lax.rsqrt(ms + EPS) * g2_ref[...] * (1.0 + sc2) + sh2
    hb = h.astype(BF16)
    acc = jnp.zeros(x1.shape, F32)
    for c in range(D_FF // FF_CHUNK):
        t = jnp.dot(hb, w1_ref[:, c * FF_CHUNK:(c + 1) * FF_CHUNK], preferred_element_type=F32)
        t = jnp.square(jnp.maximum(t, 0.0)).astype(BF16)
        acc = acc + jnp.dot(t, w2_ref[c * FF_CHUNK:(c + 1) * FF_CHUNK, :], preferred_element_type=F32)
    o_ref[0] = x1 + g2 * acc


def _output_mlp(x3, attn, four, mod3, mod_row0, w_out, norm2_g, w1, w2):
    nb, n, _ = x3.shape
    tm = OUT_TILE
    tile = lambda width: pl.BlockSpec((1, tm, width), lambda t, b: (b, t, 0))
    return pl.pallas_call(
        _out_mlp_kernel,
        grid=(n // tm, nb),
        in_specs=[tile(D_MODEL), tile(ATTN_WIDTH), tile(FOUR_WIDTH),
                  pl.BlockSpec((1, 1, 6 * D_MODEL), lambda t, b: (mod_row0 + b, 0, 0)),
                  _resident((ATTN_WIDTH + FOUR_WIDTH, D_MODEL)),
                  _resident((1, D_MODEL)),
                  _resident((D_MODEL, D_FF)),
                  _resident((D_FF, D_MODEL))],
        out_specs=tile(D_MODEL),
        out_shape=jax.ShapeDtypeStruct((nb, n, D_MODEL), F32),
        compiler_params=_params(2),
        name="output_mlp",
    )(x3, attn, four, mod3, w_out, norm2_g, w1, w2)


def kernel(x_prompt, x_sample, c, cache_k, cache_v, c_ctx, w_mod, b_mod, norm1_g, w_in, q_norm_g, k_norm_g,
           lambda_q1, lambda_k1, lambda_q2, lambda_k2, subln_g, w_four, w_out, norm2_g, w1, w2):
    batch, seq, _ = x_prompt.shape
    dec_batch, dec_seq, _ = x_sample.shape
    past = cache_k.shape[2]
    l = 0

    cvec = jnp.concatenate([c_ctx[None, :], c, jnp.zeros((MOD_ROWS - 1 - dec_batch, D_MODEL), F32)], axis=0)
    mod3 = _modulation(cvec, w_mod[l], b_mod[l][None, :]).reshape(MOD_ROWS, 1, 6 * D_MODEL)
    wcs = _fourier_weights(w_four[l])

    w_in_b = w_in[l].astype(BF16)
    w_out_b = w_out[l].astype(BF16)
    w1_b = w1[l].astype(BF16)
    w2_b = w2[l].astype(BF16)
    n1 = norm1_g[l][None, :]
    n2 = norm2_g[l][None, :]
    qg = jnp.tile(q_norm_g[l], QK_WIDTH // HEAD_DIM)[None, :]
    kg = jnp.tile(k_norm_g[l], QK_WIDTH // HEAD_DIM)[None, :]
    sg = subln_g[l][None, :]
    lamv = jnp.stack([lambda_q1[l], lambda_k1[l], lambda_q2[l], lambda_k2[l]])
    gm = jnp.asarray(_head_mean_matrix()).astype(BF16)
    rope_tabs = tuple(jnp.asarray(t) for t in _rope_tables(dec_seq))

    q, k, v, uc, us, kt32, v32 = _in_projection(x_prompt, mod3, 0, False, n1, w_in_b, qg, kg, gm, wcs, None, True)
    attn, four = _attention(lamv, sg, q, k, v, uc, us)
    flat = lambda t: t.reshape(1, batch * seq, t.shape[-1])
    yp = _output_mlp(flat(x_prompt), flat(attn), flat(four), mod3, 0, w_out_b, n2, w1_b, w2_b)
    y_prompt = yp.reshape(batch, seq, D_MODEL)
    new_cache_k = kt32.reshape(batch, N_HEADS, 2, HEAD_DIM, seq).transpose(0, 4, 1, 2, 3)[:, None]
    new_cache_v = v32.reshape(batch, 1, seq, N_HEADS, V_DIM)

    q, k, v, uc, us = _in_projection(x_sample, mod3, 1, True, n1, w_in_b, qg, kg, gm, wcs, rope_tabs, False)
    ck = cache_k[:, l].transpose(0, 2, 3, 4, 1).reshape(dec_batch, QK_WIDTH, past)
    cv = cache_v[:, l].reshape(dec_batch, past * N_HEADS, V_DIM)
    attn, four = _attention(lamv, sg, q, k, v, uc, us, ck, cv)
    y_sample = _output_mlp(x_sample, attn, four, mod3, 1, w_out_b, n2, w1_b, w2_b)

    return (y_prompt, y_sample, new_cache_k, new_cache_v)
```

```python
import functools
import math

import jax
import jax.numpy as jnp
import numpy as np
from jax import lax
from jax.experimental import pallas as pl
from jax.experimental.pallas import tpu as pltpu

D_MODEL = 1024
N_HEADS = 4
HEAD_DIM = 64
V_DIM = 2 * HEAD_DIM
QK_WIDTH = N_HEADS * 2 * HEAD_DIM
ATTN_WIDTH = N_HEADS * V_DIM
N_FOUR_GROUPS = 4
FOUR_GROUP = 128
FOUR_WIDTH = N_FOUR_GROUPS * FOUR_GROUP
PROJ_WIDTH = 2 * QK_WIDTH + ATTN_WIDTH + FOUR_WIDTH
D_FF = 4 * D_MODEL
GRID_W = 64
ROPE_BASE = 10000.0
ROPE_AXIS_DIM = HEAD_DIM // 2
EPS = 1e-6
LAM_INIT = 0.8 - 0.6 * math.exp(-0.3 * 0)
LOG2E = 1.4426950408889634
MOD_ROWS = 8

F32 = jnp.float32
BF16 = jnp.bfloat16

VMEM_LIMIT_BYTES = 56 * 1024 * 1024


def _params(n_axes):
    return pltpu.CompilerParams(dimension_semantics=("arbitrary",) * n_axes,
                                vmem_limit_bytes=VMEM_LIMIT_BYTES)


def _resident(shape):
    nd = len(shape)
    return pl.BlockSpec(shape, lambda *_: (0,) * nd, pipeline_mode=pl.Buffered(1))


@functools.lru_cache(maxsize=None)
def _channel_dft():
    idx = np.arange(FOUR_GROUP)
    ang = 2.0 * np.pi * ((idx[:, None] * idx[None, :]) % FOUR_GROUP) / FOUR_GROUP
    s = 1.0 / np.sqrt(FOUR_GROUP)
    return np.stack([np.cos(ang) * s, np.sin(ang) * s]).astype(np.float32)


@functools.lru_cache(maxsize=None)
def _position_dft(n):
    idx = np.arange(n)
    ang = 2.0 * np.pi * ((idx[:, None] * idx[None, :]) % n) / n
    s = 1.0 / np.sqrt(n)
    return (np.cos(ang) * s).astype(np.float32), (-np.sin(ang) * s).astype(np.float32)


@functools.lru_cache(maxsize=None)
def _rope_tables(n):
    rows = n // GRID_W
    row = np.repeat(np.arange(rows), GRID_W).astype(np.float64)
    col = np.tile(np.arange(GRID_W), rows).astype(np.float64)
    inv = ROPE_BASE ** (-np.arange(0, ROPE_AXIS_DIM, 2, dtype=np.float64) / ROPE_AXIS_DIM)
    d = np.arange(QK_WIDTH) % HEAD_DIM
    part = d // ROPE_AXIS_DIM
    i = d % ROPE_AXIS_DIM
    first = i < ROPE_AXIS_DIM // 2
    pos = np.where(part[None, :] == 0, row[:, None], col[:, None])
    ang = pos * inv[i % (ROPE_AXIS_DIM // 2)][None, :]
    cos = np.cos(ang)
    sin = np.where(first[None, :], -np.sin(ang), np.sin(ang))
    return cos.astype(np.float32), sin.astype(np.float32)


@functools.lru_cache(maxsize=None)
def _head_mean_matrix():
    g = np.kron(np.eye(QK_WIDTH // HEAD_DIM), np.ones((HEAD_DIM, HEAD_DIM))) / HEAD_DIM
    return g.astype(np.float32)


MOD_TILE = 1024


def _mod_kernel(c_ref, w_ref, b_ref, o_ref):
    c = c_ref[...]
    s = c / (1.0 + jnp.exp(-c))
    o_ref[...] = jnp.dot(s.astype(BF16), w_ref[...].astype(BF16),
                         preferred_element_type=F32) + b_ref[...]


def _modulation(cvec, w_mod, b_mod):
    n = w_mod.shape[1]
    return pl.pallas_call(
        _mod_kernel,
        grid=(n // MOD_TILE,),
        in_specs=[pl.BlockSpec((MOD_ROWS, D_MODEL), lambda j: (0, 0)),
                  pl.BlockSpec((D_MODEL, MOD_TILE), lambda j: (0, j)),
                  pl.BlockSpec((1, MOD_TILE), lambda j: (0, j))],
        out_specs=pl.BlockSpec((MOD_ROWS, MOD_TILE), lambda j: (0, j)),
        out_shape=jax.ShapeDtypeStruct((MOD_ROWS, n), F32),
        compiler_params=_params(1),
        name="modulation",
    )(cvec, w_mod, b_mod)


def _wcs_kernel(dft_ref, w_ref, o_ref):
    for g in range(N_FOUR_GROUPS):
        w = w_ref[g]
        wc = jnp.dot(dft_ref[0], w, precision=lax.Precision.HIGHEST, preferred_element_type=F32)
        ws = jnp.dot(dft_ref[1], w, precision=lax.Precision.HIGHEST, preferred_element_type=F32)
        o_ref[g] = jnp.concatenate([wc, ws], axis=-1).astype(BF16)


def _fourier_weights(w_four):
    dft = jnp.asarray(_channel_dft())
    return pl.pallas_call(
        _wcs_kernel,
        out_shape=jax.ShapeDtypeStruct((N_FOUR_GROUPS, FOUR_GROUP, 2 * FOUR_GROUP), BF16),
        name="fourier_weights",
    )(dft, w_four)


PROJ_TILE = 512


def _inproj_kernel(*refs, rope, emit_cache, bt, tn):
    it = iter(refs)
    x_ref, mod_ref, g1_ref, w_ref, qg_ref, kg_ref, gm_ref, wcs_ref = (next(it) for _ in range(8))
    if rope:
        cos_ref, sin_ref = next(it), next(it)
    q_ref, k_ref, v_ref, uc_ref, us_ref = (next(it) for _ in range(5))
    if emit_cache:
        k32_ref, v32_ref = next(it), next(it)
    wb_ref = next(it)

    @pl.when((pl.program_id(0) == 0) & (pl.program_id(1) == 0))
    def _cast_weights():
        wb_ref[...] = w_ref[...].astype(BF16)

    x = x_ref[...].reshape(bt * tn, D_MODEL)
    m = mod_ref[0]
    sh1 = m[:, 0:D_MODEL]
    sc1 = m[:, D_MODEL:2 * D_MODEL]
    ms = jnp.mean(x * x, axis=-1, keepdims=True)
    y = x * lax.rsqrt(ms + EPS) * g1_ref[...]
    h = y * (1.0 + sc1) + sh1
    p = jnp.dot(h.astype(BF16), wb_ref[...], preferred_element_type=F32)
    q = p[:, 0:QK_WIDTH]
    k = p[:, QK_WIDTH:2 * QK_WIDTH]
    v = p[:, 2 * QK_WIDTH:2 * QK_WIDTH + ATTN_WIDTH]

    def head_norm(t, g):
        msq = jnp.dot((t * t).astype(BF16), gm_ref[...], preferred_element_type=F32)
        return t * lax.rsqrt(msq + EPS) * g

    q = head_norm(q, qg_ref[...])
    k = head_norm(k, kg_ref[...])
    if emit_cache:
        for i in range(bt):
            rows = slice(i * tn, (i + 1) * tn)
            k32_ref[i] = k[rows, :].T
            for hd in range(N_HEADS):
                v32_ref[i, pl.ds(hd, tn, stride=N_HEADS), :] = v[rows, hd * V_DIM:(hd + 1) * V_DIM]

    if rope:
        lane = lax.broadcasted_iota(jnp.int32, q.shape, 1)
        first = (lane % ROPE_AXIS_DIM) < (ROPE_AXIS_DIM // 2)
        half = ROPE_AXIS_DIM // 2

        def rot(t):
            sw = jnp.where(first, pltpu.roll(t, QK_WIDTH - half, 1), pltpu.roll(t, half, 1))
            return t * cos_ref[...] + sw * sin_ref[...]

        q = rot(q)
        k = rot(k)

    blk = lambda t: t.reshape(bt, tn, t.shape[-1])
    q_ref[...] = blk((q * (LOG2E / math.sqrt(HEAD_DIM))).astype(BF16))
    k_ref[...] = blk(k.astype(BF16))
    v_ref[...] = blk(v.astype(BF16))

    f0 = 2 * QK_WIDTH + ATTN_WIDTH
    for g in range(N_FOUR_GROUPS):
        fg = p[:, f0 + g * FOUR_GROUP:f0 + (g + 1) * FOUR_GROUP].astype(BF16)
        u = jnp.dot(fg, wcs_ref[g], preferred_element_type=F32)
        uc_ref[:, :, g * FOUR_GROUP:(g + 1) * FOUR_GROUP] = blk(u[:, 0:FOUR_GROUP].astype(BF16))
        us_ref[:, :, g * FOUR_GROUP:(g + 1) * FOUR_GROUP] = blk(u[:, FOUR_GROUP:].astype(BF16))


def _in_projection(x3, mod3, mod_row0, per_batch_mod, norm_g, w_in, qg, kg, gm, wcs, rope_tabs, emit_cache):
    nb, n, _ = x3.shape
    tn = min(PROJ_TILE, n)
    bt = PROJ_TILE // tn
    assert per_batch_mod is False or bt == 1
    rope = rope_tabs is not None
    tile = lambda width: pl.BlockSpec((bt, tn, width), lambda t, b: (b, t, 0))
    mod_row = (lambda t, b: (mod_row0 + b, 0, 0)) if per_batch_mod else (lambda t, b: (mod_row0, 0, 0))
    in_specs = [tile(D_MODEL),
                pl.BlockSpec((1, 1, 6 * D_MODEL), mod_row),
                _resident((1, D_MODEL)),
                _resident((D_MODEL, PROJ_WIDTH)),
                _resident((1, QK_WIDTH)),
                _resident((1, QK_WIDTH)),
                _resident((QK_WIDTH, QK_WIDTH)),
                _resident((N_FOUR_GROUPS, FOUR_GROUP, 2 * FOUR_GROUP))]
    args = [x3, mod3, norm_g, w_in, qg, kg, gm, wcs]
    if rope:
        in_specs += [pl.BlockSpec((tn, QK_WIDTH), lambda t, b: (t, 0))] * 2
        args += list(rope_tabs)
    n_bf = 5
    out_shape = [jax.ShapeDtypeStruct((nb, n, QK_WIDTH), BF16)] * n_bf
    out_specs = [tile(QK_WIDTH)] * n_bf
    if emit_cache:
        assert tn == n
        out_shape += [jax.ShapeDtypeStruct((nb, QK_WIDTH, n), F32),
                      jax.ShapeDtypeStruct((nb, n * N_HEADS, V_DIM), F32)]
        out_specs += [pl.BlockSpec((bt, QK_WIDTH, n), lambda t, b: (b, 0, 0)),
                      pl.BlockSpec((bt, n * N_HEADS, V_DIM), lambda t, b: (b, 0, 0))]
    return pl.pallas_call(
        functools.partial(_inproj_kernel, rope=rope, emit_cache=emit_cache, bt=bt, tn=tn),
        grid=(n // tn, nb // bt),
        in_specs=in_specs,
        out_specs=out_specs,
        out_shape=out_shape,
        scratch_shapes=[pltpu.VMEM((D_MODEL, PROJ_WIDTH), BF16)],
        compiler_params=_params(2),
        name="in_projection_rope" if rope else "in_projection",
    )(*args)


ATTN_Q_TILE = 512


KEY_CHUNK = 1280
ONES_ROWS = 16


FOURIER_PIECE_ROWS = 128
PREFETCH_KEYS = 1024
ATTN_STEP_KEYS = 1024


def _attn_kernel(*refs, has_cache, n, past, bt):
    it = iter(refs)
    lam_ref, sg_ref, q_ref, k_ref, v_ref = (next(it) for _ in range(5))
    if has_cache:
        ck_ref, cv_ref = next(it), next(it)
    pc_ref, ps_ref, uc_ref, us_ref = (next(it) for _ in range(4))
    o_ref, four_ref = next(it), next(it)
    kall_ref, vt_ref = next(it), next(it)
    nk = n + past
    tq = q_ref.shape[1]

    @pl.when(pl.program_id(1) == 0)
    def _per_batch_setup():
        ones = jnp.ones((ONES_ROWS, nk), BF16)
        for b in range(bt):
            if has_cache:
                kall_ref[b, 0:past, :] = ck_ref[b].T.astype(BF16)
            kall_ref[b, past:nk, :] = k_ref[b]
            for h in range(N_HEADS):
                if has_cache:
                    cvh = cv_ref[b, pl.ds(h, past, stride=N_HEADS), :]
                    vt_ref[b, h, 0:V_DIM, 0:past] = cvh.T.astype(BF16)
                vh = v_ref[b, :, h * V_DIM:(h + 1) * V_DIM].astype(F32)
                vt_ref[b, h, 0:V_DIM, past:nk] = vh.T.astype(BF16)
                vt_ref[b, h, V_DIM:V_DIM + ONES_ROWS, :] = ones

    lv = lam_ref[...]
    l1 = jnp.exp(jnp.sum(lv[0:1] * lv[1:2], axis=-1, keepdims=True))
    l2 = jnp.exp(jnp.sum(lv[2:3] * lv[3:4], axis=-1, keepdims=True))
    lam = l1 - l2 + LAM_INIT

    feat = lax.broadcasted_iota(jnp.int32, (V_DIM, tq), 0)
    half0 = feat < HEAD_DIM

    kc = min(KEY_CHUNK, nk)
    chunks = [slice(c * kc, (c + 1) * kc) for c in range(nk // kc)]
    chains = [(b, h, mhalf) for b in range(bt) for h in range(N_HEADS) for mhalf in range(2)]
    head = lambda h: slice(h * V_DIM, (h + 1) * V_DIM)
    qts = {}

    def scores(chain):
        b, h, mhalf = chain
        if (b, h) not in qts:
            qts[b, h] = q_ref[b, :, head(h)].astype(F32).T
        qm = jnp.where(half0 if mhalf == 0 else jnp.logical_not(half0), qts[b, h], 0.0).astype(BF16)

        def one(rows):
            s = jnp.dot(kall_ref[b, rows, head(h)], qm, preferred_element_type=F32)
            return s, jnp.max(s, axis=0, keepdims=True)
        return one

    def fourier_piece(b, rows):
        def emit():
            o = jnp.dot(pc_ref[rows, :], uc_ref[b], preferred_element_type=F32)
            o = o + jnp.dot(ps_ref[rows, :], us_ref[b], preferred_element_type=F32)
            four_ref[b, rows, :] = o.astype(BF16)
        return emit

    fr = min(FOURIER_PIECE_ROWS, tq)
    pieces = [fourier_piece(b, slice(r * fr, (r + 1) * fr)) for b in range(bt) for r in range(tq // fr)]
    piece_every = len(chains) // len(pieces)

    depth = max(1, min(len(chains), PREFETCH_KEYS // nk))
    queue = [[scores(chain)(rows) for rows in chunks] for chain in chains[:depth]]
    acc = [None, None]
    for i, (b, h, mhalf) in enumerate(chains):
        cur = queue.pop(0)
        mx = functools.reduce(jnp.maximum, [m for _, m in cur])
        nxt = scores(chains[i + depth]) if i + depth < len(chains) else None
        issued = []
        a = None
        for c, rows in enumerate(chunks):
            if nxt is not None:
                issued.append(nxt(rows))
            e = jnp.exp2(cur[c][0] - mx).astype(BF16)
            part = jnp.dot(vt_ref[b, h, :, rows], e, preferred_element_type=F32)
            a = part if a is None else a + part
        if nxt is not None:
            queue.append(issued)
        if i % piece_every == 0:
            pieces[i // piece_every]()
        acc[mhalf] = a
        if mhalf == 1:
            a0, a1 = acc
            r0 = 1.0 / a0[V_DIM:V_DIM + 1, :]
            r1 = lam / a1[V_DIM:V_DIM + 1, :]
            o = (a0[0:V_DIM, :] * r0 - a1[0:V_DIM, :] * r1).T
            ms = jnp.mean(o * o, axis=-1, keepdims=True)
            o = o * lax.rsqrt(ms + EPS) * sg_ref[...] * (1.0 - LAM_INIT)
            o_ref[b, :, head(h)] = o.astype(BF16)


def _attention(lamv, subln_g, q, k, v, uc, us, cache_k=None, cache_v=None):
    nb, n, _ = q.shape
    tq = min(ATTN_Q_TILE, n)
    has_cache = cache_k is not None
    past = cache_k.shape[2] if has_cache else 0
    nk = n + past
    bt = max(1, min(nb, ATTN_STEP_KEYS // nk))
    qtile = pl.BlockSpec((bt, tq, QK_WIDTH), lambda b, t: (b, t, 0))
    per_batch = pl.BlockSpec((bt, n, QK_WIDTH), lambda b, t: (b, 0, 0))
    in_specs = [_resident((4, HEAD_DIM)), _resident((1, V_DIM)), qtile, per_batch, per_batch]
    args = [lamv, subln_g, q, k, v]
    if has_cache:
        in_specs += [pl.BlockSpec((bt, QK_WIDTH, past), lambda b, t: (b, 0, 0)),
                     pl.BlockSpec((bt, past * N_HEADS, V_DIM), lambda b, t: (b, 0, 0))]
        args += [cache_k, cache_v]
    pc, ps = _position_dft(n)
    dft_rows = pl.BlockSpec((tq, n), lambda b, t: (t, 0))
    in_specs += [dft_rows, dft_rows, per_batch, per_batch]
    args += [jnp.asarray(pc).astype(BF16), jnp.asarray(ps).astype(BF16), uc, us]
    return pl.pallas_call(
        functools.partial(_attn_kernel, has_cache=has_cache, n=n, past=past, bt=bt),
        grid=(nb // bt, n // tq),
        in_specs=in_specs,
        out_specs=[qtile, qtile],
        out_shape=[jax.ShapeDtypeStruct((nb, n, ATTN_WIDTH), BF16),
                   jax.ShapeDtypeStruct((nb, n, FOUR_WIDTH), BF16)],
        scratch_shapes=[pltpu.VMEM((bt, nk, QK_WIDTH), BF16),
                        pltpu.VMEM((bt, N_HEADS, V_DIM + ONES_ROWS, nk), BF16)],
        compiler_params=_params(2),
        name="diff_attention_cached" if has_cache else "diff_attention",
    )(*args)


OUT_TILE = 512
FF_CHUNK = 1024


def _out_mlp_kernel(x_ref, a_ref, f_ref, mod_ref, wo_ref, g2_ref, w1_ref, w2_ref, o_ref, wob_ref):
    @pl.when((pl.program_id(0) == 0) & (pl.program_id(1) == 0))
    def _cast_weights():
        wob_ref[...] = wo_ref[...].astype(BF16)

    m = mod_ref[0]
    g1 = m[:, 2 * D_MODEL:3 * D_MODEL]
    sh2 = m[:, 3 * D_MODEL:4 * D_MODEL]
    sc2 = m[:, 4 * D_MODEL:5 * D_MODEL]
    g2 = m[:, 5 * D_MODEL:6 * D_MODEL]
    mix = jnp.dot(a_ref[0], wob_ref[0:ATTN_WIDTH, :], preferred_element_type=F32)
    mix = mix + jnp.dot(f_ref[0], wob_ref[ATTN_WIDTH:, :], preferred_element_type=F32)
    x1 = x_ref[0] + g1 * mix
    ms = jnp.mean(x1 * x1, axis=-1, keepdims=True)
    h = x1 * lax.rsqrt(ms + EPS) * g2_ref[...] * (1.0 + sc2) + sh2
    hb = h.astype(BF16)
    acc = jnp.zeros(x1.shape, F32)
    for c in range(D_FF // FF_CHUNK):
        t = jnp.dot(hb, w1_ref[:, c * FF_CHUNK:(c + 1) * FF_CHUNK], preferred_element_type=F32)
        t = jnp.square(jnp.maximum(t, 0.0)).astype(BF16)
        acc = acc + jnp.dot(t, w2_ref[c * FF_CHUNK:(c + 1) * FF_CHUNK, :], preferred_element_type=F32)
    o_ref[0] = x1 + g2 * acc


def _output_mlp(x3, attn, four, mod3, mod_row0, w_out, norm2_g, w1, w2):
    nb, n, _ = x3.shape
    tm = OUT_TILE
    tile = lambda width: pl.BlockSpec((1, tm, width), lambda t, b: (b, t, 0))
    return pl.pallas_call(
        _out_mlp_kernel,
        grid=(n // tm, nb),
        in_specs=[tile(D_MODEL), tile(ATTN_WIDTH), tile(FOUR_WIDTH),
                  pl.BlockSpec((1, 1, 6 * D_MODEL), lambda t, b: (mod_row0 + b, 0, 0)),
                  _resident((ATTN_WIDTH + FOUR_WIDTH, D_MODEL)),
                  _resident((1, D_MODEL)),
                  _resident((D_MODEL, D_FF)),
                  _resident((D_FF, D_MODEL))],
        out_specs=tile(D_MODEL),
        out_shape=jax.ShapeDtypeStruct((nb, n, D_MODEL), F32),
        scratch_shapes=[pltpu.VMEM((ATTN_WIDTH + FOUR_WIDTH, D_MODEL), BF16)],
        compiler_params=_params(2),
        name="output_mlp",
    )(x3, attn, four, mod3, w_out, norm2_g, w1, w2)


def kernel(x_prompt, x_sample, c, cache_k, cache_v, c_ctx, w_mod, b_mod, norm1_g, w_in, q_norm_g, k_norm_g,
           lambda_q1, lambda_k1, lambda_q2, lambda_k2, subln_g, w_four, w_out, norm2_g, w1, w2):
    batch, seq, _ = x_prompt.shape
    dec_batch, dec_seq, _ = x_sample.shape
    past = cache_k.shape[2]
    l = 0

    cvec = jnp.concatenate([c_ctx[None, :], c, jnp.zeros((MOD_ROWS - 1 - dec_batch, D_MODEL), F32)], axis=0)
    mod3 = _modulation(cvec, w_mod[l], b_mod[l][None, :]).reshape(MOD_ROWS, 1, 6 * D_MODEL)
    wcs = _fourier_weights(w_four[l])

    w1_b = w1[l].astype(BF16)
    w2_b = w2[l].astype(BF16)
    n1 = norm1_g[l][None, :]
    n2 = norm2_g[l][None, :]
    qg = jnp.tile(q_norm_g[l], QK_WIDTH // HEAD_DIM)[None, :]
    kg = jnp.tile(k_norm_g[l], QK_WIDTH // HEAD_DIM)[None, :]
    sg = subln_g[l][None, :]
    lamv = jnp.stack([lambda_q1[l], lambda_k1[l], lambda_q2[l], lambda_k2[l]])
    gm = jnp.asarray(_head_mean_matrix()).astype(BF16)
    rope_tabs = tuple(jnp.asarray(t) for t in _rope_tables(dec_seq))

    q, k, v, uc, us, kt32, v32 = _in_projection(x_prompt, mod3, 0, False, n1, w_in[l], qg, kg, gm, wcs, None, True)
    attn, four = _attention(lamv, sg, q, k, v, uc, us)
    flat = lambda t: t.reshape(1, batch * seq, t.shape[-1])
    yp = _output_mlp(flat(x_prompt), flat(attn), flat(four), mod3, 0, w_out[l], n2, w1_b, w2_b)
    y_prompt = yp.reshape(batch, seq, D_MODEL)
    new_cache_k = kt32.reshape(batch, N_HEADS, 2, HEAD_DIM, seq).transpose(0, 4, 1, 2, 3)[:, None]
    new_cache_v = v32.reshape(batch, 1, seq, N_HEADS, V_DIM)

    q, k, v, uc, us = _in_projection(x_sample, mod3, 1, True, n1, w_in[l], qg, kg, gm, wcs, rope_tabs, False)
    ck = cache_k[:, l].transpose(0, 2, 3, 4, 1).reshape(dec_batch, QK_WIDTH, past)
    cv = cache_v[:, l].reshape(dec_batch, past * N_HEADS, V_DIM)
    attn, four = _attention(lamv, sg, q, k, v, uc, us, ck, cv)
    y_sample = _output_mlp(x_sample, attn, four, mod3, 1, w_out[l], n2, w1_b, w2_b)

    return (y_prompt, y_sample, new_cache_k, new_cache_v)
```

```python
import functools
import math

import jax
import jax.numpy as jnp
import numpy as np
from jax import lax
from jax.experimental import pallas as pl
from jax.experimental.pallas import tpu as pltpu

D_MODEL = 1024
N_HEADS = 4
HEAD_DIM = 64
V_DIM = 2 * HEAD_DIM
QK_WIDTH = N_HEADS * 2 * HEAD_DIM
ATTN_WIDTH = N_HEADS * V_DIM
N_FOUR_GROUPS = 4
FOUR_GROUP = 128
FOUR_WIDTH = N_FOUR_GROUPS * FOUR_GROUP
PROJ_WIDTH = 2 * QK_WIDTH + ATTN_WIDTH + FOUR_WIDTH
D_FF = 4 * D_MODEL
GRID_W = 64
ROPE_BASE = 10000.0
ROPE_AXIS_DIM = HEAD_DIM // 2
EPS = 1e-6
LAM_INIT = 0.8 - 0.6 * math.exp(-0.3 * 0)
LOG2E = 1.4426950408889634
MOD_ROWS = 8

F32 = jnp.float32
BF16 = jnp.bfloat16

VMEM_LIMIT_BYTES = 56 * 1024 * 1024


def _params(n_axes):
    return pltpu.CompilerParams(dimension_semantics=("arbitrary",) * n_axes,
                                vmem_limit_bytes=VMEM_LIMIT_BYTES)


def _resident(shape):
    nd = len(shape)
    return pl.BlockSpec(shape, lambda *_: (0,) * nd, pipeline_mode=pl.Buffered(1))


@functools.lru_cache(maxsize=None)
def _channel_dft():
    idx = np.arange(FOUR_GROUP)
    ang = 2.0 * np.pi * ((idx[:, None] * idx[None, :]) % FOUR_GROUP) / FOUR_GROUP
    s = 1.0 / np.sqrt(FOUR_GROUP)
    return np.stack([np.cos(ang) * s, np.sin(ang) * s]).astype(np.float32)


@functools.lru_cache(maxsize=None)
def _position_dft(n):
    idx = np.arange(n)
    ang = 2.0 * np.pi * ((idx[:, None] * idx[None, :]) % n) / n
    s = 1.0 / np.sqrt(n)
    return (np.cos(ang) * s).astype(np.float32), (-np.sin(ang) * s).astype(np.float32)


@functools.lru_cache(maxsize=None)
def _rope_tables(n):
    rows = n // GRID_W
    row = np.repeat(np.arange(rows), GRID_W).astype(np.float64)
    col = np.tile(np.arange(GRID_W), rows).astype(np.float64)
    inv = ROPE_BASE ** (-np.arange(0, ROPE_AXIS_DIM, 2, dtype=np.float64) / ROPE_AXIS_DIM)
    d = np.arange(QK_WIDTH) % HEAD_DIM
    part = d // ROPE_AXIS_DIM
    i = d % ROPE_AXIS_DIM
    first = i < ROPE_AXIS_DIM // 2
    pos = np.where(part[None, :] == 0, row[:, None], col[:, None])
    ang = pos * inv[i % (ROPE_AXIS_DIM // 2)][None, :]
    cos = np.cos(ang)
    sin = np.where(first[None, :], -np.sin(ang), np.sin(ang))
    return cos.astype(np.float32), sin.astype(np.float32)


@functools.lru_cache(maxsize=None)
def _rope_tables_feature_major(n):
    rows = n // GRID_W
    row = np.repeat(np.arange(rows), GRID_W).astype(np.float64)
    col = np.tile(np.arange(GRID_W), rows).astype(np.float64)
    inv = ROPE_BASE ** (-np.arange(0, ROPE_AXIS_DIM, 2, dtype=np.float64) / ROPE_AXIS_DIM)
    ang_r = inv[:, None] * row[None, :]
    ang_c = inv[:, None] * col[None, :]
    return np.stack([np.cos(ang_r), np.sin(ang_r), np.cos(ang_c), np.sin(ang_c)]).astype(np.float32)


@functools.lru_cache(maxsize=None)
def _head_mean_matrix():
    g = np.kron(np.eye(QK_WIDTH // HEAD_DIM), np.ones((HEAD_DIM, HEAD_DIM))) / HEAD_DIM
    return g.astype(np.float32)


MOD_TILE = 1024


def _mod_kernel(c_ref, w_ref, b_ref, o_ref):
    c = c_ref[...]
    s = c / (1.0 + jnp.exp(-c))
    o_ref[...] = jnp.dot(s.astype(BF16), w_ref[...].astype(BF16),
                         preferred_element_type=F32) + b_ref[...]


def _modulation(cvec, w_mod, b_mod):
    n = w_mod.shape[1]
    return pl.pallas_call(
        _mod_kernel,
        grid=(n // MOD_TILE,),
        in_specs=[pl.BlockSpec((MOD_ROWS, D_MODEL), lambda j: (0, 0)),
                  pl.BlockSpec((D_MODEL, MOD_TILE), lambda j: (0, j)),
                  pl.BlockSpec((1, MOD_TILE), lambda j: (0, j))],
        out_specs=pl.BlockSpec((MOD_ROWS, MOD_TILE), lambda j: (0, j)),
        out_shape=jax.ShapeDtypeStruct((MOD_ROWS, n), F32),
        compiler_params=_params(1),
        name="modulation",
    )(cvec, w_mod, b_mod)


def _wcs_kernel(dft_ref, w_ref, o_ref):
    for g in range(N_FOUR_GROUPS):
        w = w_ref[g]
        wc = jnp.dot(dft_ref[0], w, precision=lax.Precision.HIGHEST, preferred_element_type=F32)
        ws = jnp.dot(dft_ref[1], w, precision=lax.Precision.HIGHEST, preferred_element_type=F32)
        o_ref[g] = jnp.concatenate([wc, ws], axis=-1).astype(BF16)


def _fourier_weights(w_four):
    dft = jnp.asarray(_channel_dft())
    return pl.pallas_call(
        _wcs_kernel,
        out_shape=jax.ShapeDtypeStruct((N_FOUR_GROUPS, FOUR_GROUP, 2 * FOUR_GROUP), BF16),
        name="fourier_weights",
    )(dft, w_four)


PROJ_TILE = 512


def _inproj_kernel(*refs, rope, emit_cache, bt, tn):
    it = iter(refs)
    x_ref, mod_ref, g1_ref, w_ref, kg_ref, gm_ref, wcs_ref = (next(it) for _ in range(7))
    if rope:
        cos_ref, sin_ref = next(it), next(it)
    q_ref, k_ref, v_ref, uc_ref, us_ref = (next(it) for _ in range(5))
    if emit_cache:
        k32_ref, v32_ref = next(it), next(it)

    x = x_ref[...].reshape(bt * tn, D_MODEL)
    m = mod_ref[0]
    sh1 = m[:, 0:D_MODEL]
    sc1 = m[:, D_MODEL:2 * D_MODEL]
    ms = jnp.mean(x * x, axis=-1, keepdims=True)
    y = x * lax.rsqrt(ms + EPS) * g1_ref[...]
    h = y * (1.0 + sc1) + sh1
    p = jnp.dot(h.astype(BF16), w_ref[...], preferred_element_type=F32)
    q = p[:, 0:QK_WIDTH]
    k = p[:, QK_WIDTH:2 * QK_WIDTH]
    v = p[:, 2 * QK_WIDTH:2 * QK_WIDTH + ATTN_WIDTH]

    msq = jnp.dot((k * k).astype(BF16), gm_ref[...], preferred_element_type=F32)
    k = k * lax.rsqrt(msq + EPS) * kg_ref[...]
    if emit_cache:
        for i in range(bt):
            rows = slice(i * tn, (i + 1) * tn)
            k32_ref[i] = k[rows, :].T
            for hd in range(N_HEADS):
                v32_ref[i, pl.ds(hd, tn, stride=N_HEADS), :] = v[rows, hd * V_DIM:(hd + 1) * V_DIM]

    if rope:
        lane = lax.broadcasted_iota(jnp.int32, k.shape, 1)
        first = (lane % ROPE_AXIS_DIM) < (ROPE_AXIS_DIM // 2)
        half = ROPE_AXIS_DIM // 2
        sw = jnp.where(first, pltpu.roll(k, QK_WIDTH - half, 1), pltpu.roll(k, half, 1))
        k = k * cos_ref[...] + sw * sin_ref[...]

    blk = lambda t: t.reshape(bt, tn, t.shape[-1])
    q_ref[...] = blk(q)
    k_ref[...] = blk(k.astype(BF16))
    v_ref[...] = blk(v.astype(BF16))

    f0 = 2 * QK_WIDTH + ATTN_WIDTH
    for g in range(N_FOUR_GROUPS):
        fg = p[:, f0 + g * FOUR_GROUP:f0 + (g + 1) * FOUR_GROUP].astype(BF16)
        u = jnp.dot(fg, wcs_ref[g], preferred_element_type=F32)
        uc_ref[:, :, g * FOUR_GROUP:(g + 1) * FOUR_GROUP] = blk(u[:, 0:FOUR_GROUP].astype(BF16))
        us_ref[:, :, g * FOUR_GROUP:(g + 1) * FOUR_GROUP] = blk(u[:, FOUR_GROUP:].astype(BF16))


def _in_projection(x3, mod3, mod_row0, per_batch_mod, norm_g, w_in, kg, gm, wcs, rope_tabs, emit_cache):
    nb, n, _ = x3.shape
    tn = min(PROJ_TILE, n)
    bt = PROJ_TILE // tn
    assert per_batch_mod is False or bt == 1
    rope = rope_tabs is not None
    tile = lambda width: pl.BlockSpec((bt, tn, width), lambda t, b: (b, t, 0))
    mod_row = (lambda t, b: (mod_row0 + b, 0, 0)) if per_batch_mod else (lambda t, b: (mod_row0, 0, 0))
    in_specs = [tile(D_MODEL),
                pl.BlockSpec((1, 1, 6 * D_MODEL), mod_row),
                _resident((1, D_MODEL)),
                _resident((D_MODEL, PROJ_WIDTH)),
                _resident((1, QK_WIDTH)),
                _resident((QK_WIDTH, QK_WIDTH)),
                _resident((N_FOUR_GROUPS, FOUR_GROUP, 2 * FOUR_GROUP))]
    args = [x3, mod3, norm_g, w_in, kg, gm, wcs]
    if rope:
        in_specs += [pl.BlockSpec((tn, QK_WIDTH), lambda t, b: (t, 0))] * 2
        args += list(rope_tabs)
    out_shape = [jax.ShapeDtypeStruct((nb, n, QK_WIDTH), F32)]
    out_shape += [jax.ShapeDtypeStruct((nb, n, QK_WIDTH), BF16)] * 4
    out_specs = [tile(QK_WIDTH)] * 5
    if emit_cache:
        assert tn == n
        out_shape += [jax.ShapeDtypeStruct((nb, QK_WIDTH, n), F32),
                      jax.ShapeDtypeStruct((nb, n * N_HEADS, V_DIM), F32)]
        out_specs += [pl.BlockSpec((bt, QK_WIDTH, n), lambda t, b: (b, 0, 0)),
                      pl.BlockSpec((bt, n * N_HEADS, V_DIM), lambda t, b: (b, 0, 0))]
    return pl.pallas_call(
        functools.partial(_inproj_kernel, rope=rope, emit_cache=emit_cache, bt=bt, tn=tn),
        grid=(n // tn, nb // bt),
        in_specs=in_specs,
        out_specs=out_specs,
        out_shape=out_shape,
        compiler_params=_params(2),
        name="in_projection_rope" if rope else "in_projection",
    )(*args)


ATTN_Q_TILE = 512


KEY_CHUNK = 1280
ONES_ROWS = 16


FOURIER_PIECE_ROWS = 128
PREFETCH_KEYS = 1024
ATTN_STEP_KEYS = 1024


def _attn_kernel(*refs, has_cache, n, past, bt):
    it = iter(refs)
    lam_ref, sg_ref, qg_ref, q_ref, k_ref, v_ref = (next(it) for _ in range(6))
    if has_cache:
        ck_ref, cv_ref, rq_ref = next(it), next(it), next(it)
    pc_ref, ps_ref, uc_ref, us_ref = (next(it) for _ in range(4))
    o_ref, four_ref = next(it), next(it)
    kall_ref, vt_ref = next(it), next(it)
    nk = n + past
    tq = q_ref.shape[1]

    @pl.when(pl.program_id(1) == 0)
    def _per_batch_setup():
        ones = jnp.ones((ONES_ROWS, nk), BF16)
        for b in range(bt):
            if has_cache:
                kall_ref[b, 0:past, :] = ck_ref[b].T.astype(BF16)
            kall_ref[b, past:nk, :] = k_ref[b]
            for h in range(N_HEADS):
                if has_cache:
                    cvh = cv_ref[b, pl.ds(h, past, stride=N_HEADS), :]
                    vt_ref[b, h, 0:V_DIM, 0:past] = cvh.T.astype(BF16)
                vh = v_ref[b, :, h * V_DIM:(h + 1) * V_DIM].astype(F32)
                vt_ref[b, h, 0:V_DIM, past:nk] = vh.T.astype(BF16)
                vt_ref[b, h, V_DIM:V_DIM + ONES_ROWS, :] = ones

    lv = lam_ref[...]
    l1 = jnp.exp(jnp.sum(lv[0:1] * lv[1:2], axis=-1, keepdims=True))
    l2 = jnp.exp(jnp.sum(lv[2:3] * lv[3:4], axis=-1, keepdims=True))
    lam = l1 - l2 + LAM_INIT

    head = lambda h: slice(h * V_DIM, (h + 1) * V_DIM)
    qscale = qg_ref[...] * (LOG2E / math.sqrt(HEAD_DIM))
    zeros = jnp.zeros((HEAD_DIM, tq), BF16)
    nf = ROPE_AXIS_DIM // 2

    def prepare_q(b, h):
        qt = q_ref[b, :, head(h)].T
        halves = []
        for mhalf in range(2):
            t = qt[mhalf * HEAD_DIM:(mhalf + 1) * HEAD_DIM, :]
            t = t * lax.rsqrt(jnp.mean(t * t, axis=0, keepdims=True) + EPS) * qscale
            if has_cache:
                parts = []
                for axis in range(2):
                    x1 = t[(2 * axis) * nf:(2 * axis + 1) * nf, :]
                    x2 = t[(2 * axis + 1) * nf:(2 * axis + 2) * nf, :]
                    cos, sin = rq_ref[2 * axis], rq_ref[2 * axis + 1]
                    parts += [x1 * cos - x2 * sin, x2 * cos + x1 * sin]
                t = jnp.concatenate(parts, axis=0)
            halves.append(t.astype(BF16))
        return halves

    kc = min(KEY_CHUNK, nk)
    chunks = [slice(c * kc, (c + 1) * kc) for c in range(nk // kc)]
    chains = [(b, h, mhalf) for b in range(bt) for h in range(N_HEADS) for mhalf in range(2)]
    qts = {}

    def scores(chain):
        b, h, mhalf = chain
        if (b, h) not in qts:
            qts[b, h] = prepare_q(b, h)
        qm = jnp.concatenate([qts[b, h][0], zeros] if mhalf == 0 else [zeros, qts[b, h][1]], axis=0)

        def one(rows):
            s = jnp.dot(kall_ref[b, rows, head(h)], qm, preferred_element_type=F32)
            return s, jnp.max(s, axis=0, keepdims=True)
        return one

    def fourier_piece(b, rows):
        def emit():
            o = jnp.dot(pc_ref[rows, :], uc_ref[b], preferred_element_type=F32)
            o = o + jnp.dot(ps_ref[rows, :], us_ref[b], preferred_element_type=F32)
            four_ref[b, rows, :] = o.astype(BF16)
        return emit

    fr = min(FOURIER_PIECE_ROWS, tq)
    pieces = [fourier_piece(b, slice(r * fr, (r + 1) * fr)) for b in range(bt) for r in range(tq // fr)]
    piece_every = len(chains) // len(pieces)

    depth = max(1, min(len(chains), PREFETCH_KEYS // nk))
    queue = [[scores(chain)(rows) for rows in chunks] for chain in chains[:depth]]
    acc = [None, None]
    for i, (b, h, mhalf) in enumerate(chains):
        cur = queue.pop(0)
        mx = functools.reduce(jnp.maximum, [m for _, m in cur])
        nxt = scores(chains[i + depth]) if i + depth < len(chains) else None
        issued = []
        a = None
        for c, rows in enumerate(chunks):
            if nxt is not None:
                issued.append(nxt(rows))
            e = jnp.exp2(cur[c][0] - mx).astype(BF16)
            part = jnp.dot(vt_ref[b, h, :, rows], e, preferred_element_type=F32)
            a = part if a is None else a + part
        if nxt is not None:
            queue.append(issued)
        if i % piece_every == 0:
            pieces[i // piece_every]()
        acc[mhalf] = a
        if mhalf == 1:
            a0, a1 = acc
            r0 = 1.0 / a0[V_DIM:V_DIM + 1, :]
            r1 = lam / a1[V_DIM:V_DIM + 1, :]
            o = (a0[0:V_DIM, :] * r0 - a1[0:V_DIM, :] * r1).T
            ms = jnp.mean(o * o, axis=-1, keepdims=True)
            o = o * lax.rsqrt(ms + EPS) * sg_ref[...] * (1.0 - LAM_INIT)
            o_ref[b, :, head(h)] = o.astype(BF16)


def _attention(lamv, subln_g, qg_col, q, k, v, uc, us, cache_k=None, cache_v=None):
    nb, n, _ = q.shape
    tq = min(ATTN_Q_TILE, n)
    has_cache = cache_k is not None
    past = cache_k.shape[2] if has_cache else 0
    nk = n + past
    bt = max(1, min(nb, ATTN_STEP_KEYS // nk))
    qtile = pl.BlockSpec((bt, tq, QK_WIDTH), lambda b, t: (b, t, 0))
    per_batch = pl.BlockSpec((bt, n, QK_WIDTH), lambda b, t: (b, 0, 0))
    in_specs = [_resident((4, HEAD_DIM)), _resident((1, V_DIM)), _resident((HEAD_DIM, 1)),
                qtile, per_batch, per_batch]
    args = [lamv, subln_g, qg_col, q, k, v]
    if has_cache:
        in_specs += [pl.BlockSpec((bt, QK_WIDTH, past), lambda b, t: (b, 0, 0)),
                     pl.BlockSpec((bt, past * N_HEADS, V_DIM), lambda b, t: (b, 0, 0)),
                     pl.BlockSpec((4, ROPE_AXIS_DIM // 2, tq), lambda b, t: (0, 0, t))]
        args += [cache_k, cache_v, jnp.asarray(_rope_tables_feature_major(n))]
    pc, ps = _position_dft(n)
    dft_rows = pl.BlockSpec((tq, n), lambda b, t: (t, 0))
    in_specs += [dft_rows, dft_rows, per_batch, per_batch]
    args += [jnp.asarray(pc).astype(BF16), jnp.asarray(ps).astype(BF16), uc, us]
    return pl.pallas_call(
        functools.partial(_attn_kernel, has_cache=has_cache, n=n, past=past, bt=bt),
        grid=(nb // bt, n // tq),
        in_specs=in_specs,
        out_specs=[qtile, qtile],
        out_shape=[jax.ShapeDtypeStruct((nb, n, ATTN_WIDTH), BF16),
                   jax.ShapeDtypeStruct((nb, n, FOUR_WIDTH), BF16)],
        scratch_shapes=[pltpu.VMEM((bt, nk, QK_WIDTH), BF16),
                        pltpu.VMEM((bt, N_HEADS, V_DIM + ONES_ROWS, nk), BF16)],
        compiler_params=_params(2),
        name="diff_attention_cached" if has_cache else "diff_attention",
    )(*args)


OUT_TILE = 512
FF_CHUNK = 1024


def _out_mlp_kernel(x_ref, a_ref, f_ref, mod_ref, wo_ref, g2_ref, w1_ref, w2_ref, o_ref):
    m = mod_ref[0]
    g1 = m[:, 2 * D_MODEL:3 * D_MODEL]
    sh2 = m[:, 3 * D_MODEL:4 * D_MODEL]
    sc2 = m[:, 4 * D_MODEL:5 * D_MODEL]
    g2 = m[:, 5 * D_MODEL:6 * D_MODEL]
    mix = jnp.dot(a_ref[0], wo_ref[0:ATTN_WIDTH, :], preferred_element_type=F32)
    mix = mix + jnp.dot(f_ref[0], wo_ref[ATTN_WIDTH:, :], preferred_element_type=F32)
    x1 = x_ref[0] + g1 * mix
    ms = jnp.mean(x1 * x1, axis=-1, keepdims=True)
    h = x1 * lax.rsqrt(ms + EPS) * g2_ref[...] * (1.0 + sc2) + sh2
    hb = h.astype(BF16)
    acc = jnp.zeros(x1.shape, F32)
    for c in range(D_FF // FF_CHUNK):
        t = jnp.dot(hb, w1_ref[:, c * FF_CHUNK:(c + 1) * FF_CHUNK], preferred_element_type=F32)
        t = jnp.square(jnp.maximum(t, 0.0)).astype(BF16)
        acc = acc + jnp.dot(t, w2_ref[c * FF_CHUNK:(c + 1) * FF_CHUNK, :], preferred_element_type=F32)
    o_ref[0] = x1 + g2 * acc


def _output_mlp(x3, attn, four, mod3, mod_row0, w_out, norm2_g, w1, w2):
    nb, n, _ = x3.shape
    tm = OUT_TILE
    tile = lambda width: pl.BlockSpec((1, tm, width), lambda t, b: (b, t, 0))
    return pl.pallas_call(
        _out_mlp_kernel,
        grid=(n // tm, nb),
        in_specs=[tile(D_MODEL), tile(ATTN_WIDTH), tile(FOUR_WIDTH),
                  pl.BlockSpec((1, 1, 6 * D_MODEL), lambda t, b: (mod_row0 + b, 0, 0)),
                  _resident((ATTN_WIDTH + FOUR_WIDTH, D_MODEL)),
                  _resident((1, D_MODEL)),
                  _resident((D_MODEL, D_FF)),
                  _resident((D_FF, D_MODEL))],
        out_specs=tile(D_MODEL),
        out_shape=jax.ShapeDtypeStruct((nb, n, D_MODEL), F32),
        compiler_params=_params(2),
        name="output_mlp",
    )(x3, attn, four, mod3, w_out, norm2_g, w1, w2)


def kernel(x_prompt, x_sample, c, cache_k, cache_v, c_ctx, w_mod, b_mod, norm1_g, w_in, q_norm_g, k_norm_g,
           lambda_q1, lambda_k1, lambda_q2, lambda_k2, subln_g, w_four, w_out, norm2_g, w1, w2):
    batch, seq, _ = x_prompt.shape
    dec_batch, dec_seq, _ = x_sample.shape
    past = cache_k.shape[2]
    l = 0

    cvec = jnp.concatenate([c_ctx[None, :], c, jnp.zeros((MOD_ROWS - 1 - dec_batch, D_MODEL), F32)], axis=0)
    mod3 = _modulation(cvec, w_mod[l], b_mod[l][None, :]).reshape(MOD_ROWS, 1, 6 * D_MODEL)
    wcs = _fourier_weights(w_four[l])

    w_in_b = w_in[l].astype(BF16)
    w_out_b = w_out[l].astype(BF16)
    w1_b = w1[l].astype(BF16)
    w2_b = w2[l].astype(BF16)
    n1 = norm1_g[l][None, :]
    n2 = norm2_g[l][None, :]
    qg_col = q_norm_g[l][:, None]
    kg = jnp.tile(k_norm_g[l], QK_WIDTH // HEAD_DIM)[None, :]
    sg = subln_g[l][None, :]
    lamv = jnp.stack([lambda_q1[l], lambda_k1[l], lambda_q2[l], lambda_k2[l]])
    gm = jnp.asarray(_head_mean_matrix()).astype(BF16)
    rope_tabs = tuple(jnp.asarray(t) for t in _rope_tables(dec_seq))

    q, k, v, uc, us, kt32, v32 = _in_projection(x_prompt, mod3, 0, False, n1, w_in_b, kg, gm, wcs, None, True)
    attn, four = _attention(lamv, sg, qg_col, q, k, v, uc, us)
    flat = lambda t: t.reshape(1, batch * seq, t.shape[-1])
    yp = _output_mlp(flat(x_prompt), flat(attn), flat(four), mod3, 0, w_out_b, n2, w1_b, w2_b)
    y_prompt = yp.reshape(batch, seq, D_MODEL)
    new_cache_k = kt32.reshape(batch, N_HEADS, 2, HEAD_DIM, seq).transpose(0, 4, 1, 2, 3)[:, None]
    new_cache_v = v32.reshape(batch, 1, seq, N_HEADS, V_DIM)

    q, k, v, uc, us = _in_projection(x_sample, mod3, 1, True, n1, w_in_b, kg, gm, wcs, rope_tabs, False)
    ck = cache_k[:, l].transpose(0, 2, 3, 4, 1).reshape(dec_batch, QK_WIDTH, past)
    cv = cache_v[:, l].reshape(dec_batch, past * N_HEADS, V_DIM)
    attn, four = _attention(lamv, sg, qg_col, q, k, v, uc, us, ck, cv)
    y_sample = _output_mlp(x_sample, attn, four, mod3, 1, w_out_b, n2, w1_b, w2_b)

    return (y_prompt, y_sample, new_cache_k, new_cache_v)
```

```python
import functools
import math

import jax
import jax.numpy as jnp
import numpy as np
from jax import lax
from jax.experimental import pallas as pl
from jax.experimental.pallas import tpu as pltpu

D_MODEL = 1024
N_HEADS = 4
HEAD_DIM = 64
V_DIM = 2 * HEAD_DIM
QK_WIDTH = N_HEADS * 2 * HEAD_DIM
ATTN_WIDTH = N_HEADS * V_DIM
N_FOUR_GROUPS = 4
FOUR_GROUP = 128
FOUR_WIDTH = N_FOUR_GROUPS * FOUR_GROUP
PROJ_WIDTH = 2 * QK_WIDTH + ATTN_WIDTH + FOUR_WIDTH
D_FF = 4 * D_MODEL
GRID_W = 64
ROPE_BASE = 10000.0
ROPE_AXIS_DIM = HEAD_DIM // 2
EPS = 1e-6
LAM_INIT = 0.8 - 0.6 * math.exp(-0.3 * 0)
LOG2E = 1.4426950408889634
MOD_ROWS = 8

F32 = jnp.float32
BF16 = jnp.bfloat16

VMEM_LIMIT_BYTES = 56 * 1024 * 1024


def _params(n_axes):
    return pltpu.CompilerParams(dimension_semantics=("arbitrary",) * n_axes,
                                vmem_limit_bytes=VMEM_LIMIT_BYTES)


def _resident(shape):
    nd = len(shape)
    return pl.BlockSpec(shape, lambda *_: (0,) * nd, pipeline_mode=pl.Buffered(1))


@functools.lru_cache(maxsize=None)
def _channel_dft():
    idx = np.arange(FOUR_GROUP)
    ang = 2.0 * np.pi * ((idx[:, None] * idx[None, :]) % FOUR_GROUP) / FOUR_GROUP
    s = 1.0 / np.sqrt(FOUR_GROUP)
    return np.stack([np.cos(ang) * s, np.sin(ang) * s]).astype(np.float32)


@functools.lru_cache(maxsize=None)
def _position_dft(n):
    idx = np.arange(n)
    ang = 2.0 * np.pi * ((idx[:, None] * idx[None, :]) % n) / n
    s = 1.0 / np.sqrt(n)
    return (np.cos(ang) * s).astype(np.float32), (-np.sin(ang) * s).astype(np.float32)


@functools.lru_cache(maxsize=None)
def _rope_tables(n):
    rows = n // GRID_W
    row = np.repeat(np.arange(rows), GRID_W).astype(np.float64)
    col = np.tile(np.arange(GRID_W), rows).astype(np.float64)
    inv = ROPE_BASE ** (-np.arange(0, ROPE_AXIS_DIM, 2, dtype=np.float64) / ROPE_AXIS_DIM)
    d = np.arange(QK_WIDTH) % HEAD_DIM
    part = d // ROPE_AXIS_DIM
    i = d % ROPE_AXIS_DIM
    first = i < ROPE_AXIS_DIM // 2
    pos = np.where(part[None, :] == 0, row[:, None], col[:, None])
    ang = pos * inv[i % (ROPE_AXIS_DIM // 2)][None, :]
    cos = np.cos(ang)
    sin = np.where(first[None, :], -np.sin(ang), np.sin(ang))
    return cos.astype(np.float32), sin.astype(np.float32)


@functools.lru_cache(maxsize=None)
def _rope_tables_feature_major(n):
    rows = n // GRID_W
    row = np.repeat(np.arange(rows), GRID_W).astype(np.float64)
    col = np.tile(np.arange(GRID_W), rows).astype(np.float64)
    inv = ROPE_BASE ** (-np.arange(0, ROPE_AXIS_DIM, 2, dtype=np.float64) / ROPE_AXIS_DIM)
    ang_r = inv[:, None] * row[None, :]
    ang_c = inv[:, None] * col[None, :]
    return np.stack([np.cos(ang_r), np.sin(ang_r), np.cos(ang_c), np.sin(ang_c)]).astype(np.float32)


@functools.lru_cache(maxsize=None)
def _head_mean_matrix():
    g = np.kron(np.eye(QK_WIDTH // HEAD_DIM), np.ones((HEAD_DIM, HEAD_DIM))) / HEAD_DIM
    return g.astype(np.float32)


MOD_TILE = 1024


def _mod_kernel(cc_ref, c_ref, w_ref, b_ref, dft_ref, wf_ref, o_ref, wcs_ref):
    @pl.when(pl.program_id(0) == 0)
    def _fourier_weights():
        for g in range(N_FOUR_GROUPS):
            w = wf_ref[g]
            wc = jnp.dot(dft_ref[0], w, precision=lax.Precision.HIGHEST, preferred_element_type=F32)
            ws = jnp.dot(dft_ref[1], w, precision=lax.Precision.HIGHEST, preferred_element_type=F32)
            wcs_ref[g] = jnp.concatenate([wc, ws], axis=-1).astype(BF16)

    w = w_ref[...].astype(BF16)
    nd = c_ref.shape[0]

    def rows(c):
        s = c / (1.0 + jnp.exp(-c))
        return jnp.dot(s.astype(BF16), w, preferred_element_type=F32) + b_ref[...]

    o_ref[0:1, :] = rows(cc_ref[...])
    o_ref[1:1 + nd, :] = rows(c_ref[...])
    o_ref[1 + nd:, :] = jnp.zeros((MOD_ROWS - 1 - nd, MOD_TILE), F32)


def _modulation_and_fourier_weights(c_ctx, c, w_mod, b_mod, w_four):
    n = w_mod.shape[1]
    nd = c.shape[0]
    dft = jnp.asarray(_channel_dft())
    return pl.pallas_call(
        _mod_kernel,
        grid=(n // MOD_TILE,),
        in_specs=[pl.BlockSpec((1, D_MODEL), lambda j: (0, 0)),
                  pl.BlockSpec((nd, D_MODEL), lambda j: (0, 0)),
                  pl.BlockSpec((D_MODEL, MOD_TILE), lambda j: (0, j)),
                  pl.BlockSpec((1, MOD_TILE), lambda j: (0, j)),
                  pl.BlockSpec((2, FOUR_GROUP, FOUR_GROUP), lambda j: (0, 0, 0)),
                  pl.BlockSpec((N_FOUR_GROUPS, FOUR_GROUP, FOUR_GROUP), lambda j: (0, 0, 0))],
        out_specs=[pl.BlockSpec((MOD_ROWS, MOD_TILE), lambda j: (0, j)),
                   pl.BlockSpec((N_FOUR_GROUPS, FOUR_GROUP, 2 * FOUR_GROUP), lambda j: (0, 0, 0))],
        out_shape=[jax.ShapeDtypeStruct((MOD_ROWS, n), F32),
                   jax.ShapeDtypeStruct((N_FOUR_GROUPS, FOUR_GROUP, 2 * FOUR_GROUP), BF16)],
        compiler_params=_params(1),
        name="modulation",
    )(c_ctx, c, w_mod, b_mod, dft, w_four)


PROJ_TILE = 1024


def _mod_row(mod_ref, mod_row0, per_batch_mod):
    if per_batch_mod:
        return mod_ref[pl.ds(mod_row0 + pl.program_id(1), 1), :]
    return mod_ref[mod_row0:mod_row0 + 1, :]


def _inproj_kernel(*refs, rope, emit_cache, bt, tn, mod_row0, per_batch_mod):
    it = iter(refs)
    x_ref, mod_ref, g1_ref, w_ref, kg_ref, gm_ref, wcs_ref = (next(it) for _ in range(7))
    if rope:
        cos_ref, sin_ref = next(it), next(it)
    q_ref, k_ref, v_ref, uc_ref, us_ref = (next(it) for _ in range(5))
    if emit_cache:
        k32_ref, v32_ref = next(it), next(it)

    x = x_ref[...].reshape(bt * tn, D_MODEL)
    m = _mod_row(mod_ref, mod_row0, per_batch_mod)
    sh1 = m[:, 0:D_MODEL]
    sc1 = m[:, D_MODEL:2 * D_MODEL]
    ms = jnp.mean(x * x, axis=-1, keepdims=True)
    y = x * lax.rsqrt(ms + EPS) * g1_ref[...]
    h = y * (1.0 + sc1) + sh1
    p = jnp.dot(h.astype(BF16), w_ref[...], preferred_element_type=F32)
    q = p[:, 0:QK_WIDTH]
    k = p[:, QK_WIDTH:2 * QK_WIDTH]
    v = p[:, 2 * QK_WIDTH:2 * QK_WIDTH + ATTN_WIDTH]

    msq = jnp.dot((k * k).astype(BF16), gm_ref[...], preferred_element_type=F32)
    k = k * lax.rsqrt(msq + EPS) * kg_ref[...]
    if emit_cache:
        for i in range(bt):
            rows = slice(i * tn, (i + 1) * tn)
            k32_ref[i] = k[rows, :].T
            for hd in range(N_HEADS):
                v32_ref[i, pl.ds(hd, tn, stride=N_HEADS), :] = v[rows, hd * V_DIM:(hd + 1) * V_DIM]

    if rope:
        lane = lax.broadcasted_iota(jnp.int32, k.shape, 1)
        first = (lane % ROPE_AXIS_DIM) < (ROPE_AXIS_DIM // 2)
        half = ROPE_AXIS_DIM // 2
        sw = jnp.where(first, pltpu.roll(k, QK_WIDTH - half, 1), pltpu.roll(k, half, 1))
        k = k * cos_ref[...] + sw * sin_ref[...]

    blk = lambda t: t.reshape(bt, tn, t.shape[-1])
    q_ref[...] = blk(q)
    k_ref[...] = blk(k.astype(BF16))
    v_ref[...] = blk(v.astype(BF16))

    f0 = 2 * QK_WIDTH + ATTN_WIDTH
    for g in range(N_FOUR_GROUPS):
        fg = p[:, f0 + g * FOUR_GROUP:f0 + (g + 1) * FOUR_GROUP].astype(BF16)
        u = jnp.dot(fg, wcs_ref[g], preferred_element_type=F32)
        uc_ref[:, :, g * FOUR_GROUP:(g + 1) * FOUR_GROUP] = blk(u[:, 0:FOUR_GROUP].astype(BF16))
        us_ref[:, :, g * FOUR_GROUP:(g + 1) * FOUR_GROUP] = blk(u[:, FOUR_GROUP:].astype(BF16))


def _in_projection(x3, mod, mod_row0, per_batch_mod, norm_g, w_in, kg, gm, wcs, rope_tabs, emit_cache):
    nb, n, _ = x3.shape
    tn = min(PROJ_TILE, n)
    bt = PROJ_TILE // tn
    assert per_batch_mod is False or bt == 1
    rope = rope_tabs is not None
    tile = lambda width: pl.BlockSpec((bt, tn, width), lambda t, b: (b, t, 0))
    in_specs = [tile(D_MODEL),
                _resident((MOD_ROWS, 6 * D_MODEL)),
                _resident((1, D_MODEL)),
                _resident((D_MODEL, PROJ_WIDTH)),
                _resident((1, QK_WIDTH)),
                _resident((QK_WIDTH, QK_WIDTH)),
                _resident((N_FOUR_GROUPS, FOUR_GROUP, 2 * FOUR_GROUP))]
    args = [x3, mod, norm_g, w_in, kg, gm, wcs]
    if rope:
        in_specs += [pl.BlockSpec((tn, QK_WIDTH), lambda t, b: (t, 0))] * 2
        args += list(rope_tabs)
    out_shape = [jax.ShapeDtypeStruct((nb, n, QK_WIDTH), F32)]
    out_shape += [jax.ShapeDtypeStruct((nb, n, QK_WIDTH), BF16)] * 4
    out_specs = [tile(QK_WIDTH)] * 5
    if emit_cache:
        assert tn == n
        out_shape += [jax.ShapeDtypeStruct((nb, QK_WIDTH, n), F32),
                      jax.ShapeDtypeStruct((nb, n * N_HEADS, V_DIM), F32)]
        out_specs += [pl.BlockSpec((bt, QK_WIDTH, n), lambda t, b: (b, 0, 0)),
                      pl.BlockSpec((bt, n * N_HEADS, V_DIM), lambda t, b: (b, 0, 0))]
    return pl.pallas_call(
        functools.partial(_inproj_kernel, rope=rope, emit_cache=emit_cache, bt=bt, tn=tn,
                          mod_row0=mod_row0, per_batch_mod=per_batch_mod),
        grid=(n // tn, nb // bt),
        in_specs=in_specs,
        out_specs=out_specs,
        out_shape=out_shape,
        compiler_params=_params(2),
        name="in_projection_rope" if rope else "in_projection",
    )(*args)


ATTN_Q_TILE = 512
KEY_CHUNK = 1280
ONES_ROWS = 16
FOURIER_PIECE_ROWS = 128
PREFETCH_KEYS = 1024
ATTN_STEP_KEYS = 1024


def _attn_kernel(*refs, has_cache, n, past, bt):
    it = iter(refs)
    lq1_ref, lk1_ref, lq2_ref, lk2_ref, sg_ref, qg_ref, q_ref, k_ref, v_ref = (next(it) for _ in range(9))
    if has_cache:
        ck_ref, cv_ref, rq_ref = next(it), next(it), next(it)
    pc_ref, ps_ref, uc_ref, us_ref = (next(it) for _ in range(4))
    o_ref, four_ref = next(it), next(it)
    kall_ref, vt_ref = next(it), next(it)
    nk = n + past
    tq = q_ref.shape[1]

    @pl.when(pl.program_id(1) == 0)
    def _per_batch_setup():
        ones = jnp.ones((ONES_ROWS, nk), BF16)
        for b in range(bt):
            if has_cache:
                kall_ref[b, 0:past, :] = ck_ref[b].T.astype(BF16)
            kall_ref[b, past:nk, :] = k_ref[b]
            for h in range(N_HEADS):
                if has_cache:
                    cvh = cv_ref[b, pl.ds(h, past, stride=N_HEADS), :]
                    vt_ref[b, h, 0:V_DIM, 0:past] = cvh.T.astype(BF16)
                vh = v_ref[b, :, h * V_DIM:(h + 1) * V_DIM].astype(F32)
                vt_ref[b, h, 0:V_DIM, past:nk] = vh.T.astype(BF16)
                vt_ref[b, h, V_DIM:V_DIM + ONES_ROWS, :] = ones

    l1 = jnp.exp(jnp.sum(lq1_ref[...] * lk1_ref[...], axis=-1, keepdims=True))
    l2 = jnp.exp(jnp.sum(lq2_ref[...] * lk2_ref[...], axis=-1, keepdims=True))
    lam = l1 - l2 + LAM_INIT

    head = lambda h: slice(h * V_DIM, (h + 1) * V_DIM)
    qscale = qg_ref[...] * (LOG2E / math.sqrt(HEAD_DIM))
    zeros = jnp.zeros((HEAD_DIM, tq), BF16)
    nf = ROPE_AXIS_DIM // 2

    def prepare_q(b, h):
        qt = q_ref[b, :, head(h)].T
        halves = []
        for mhalf in range(2):
            t = qt[mhalf * HEAD_DIM:(mhalf + 1) * HEAD_DIM, :]
            t = t * lax.rsqrt(jnp.mean(t * t, axis=0, keepdims=True) + EPS) * qscale
            if has_cache:
                parts = []
                for axis in range(2):
                    x1 = t[(2 * axis) * nf:(2 * axis + 1) * nf, :]
                    x2 = t[(2 * axis + 1) * nf:(2 * axis + 2) * nf, :]
                    cos, sin = rq_ref[2 * axis], rq_ref[2 * axis + 1]
                    parts += [x1 * cos - x2 * sin, x2 * cos + x1 * sin]
                t = jnp.concatenate(parts, axis=0)
            halves.append(t.astype(BF16))
        return halves

    kc = min(KEY_CHUNK, nk)
    chunks = [slice(c * kc, (c + 1) * kc) for c in range(nk // kc)]
    chains = [(b, h, mhalf) for b in range(bt) for h in range(N_HEADS) for mhalf in range(2)]
    qts = {}

    def scores(chain):
        b, h, mhalf = chain
        if (b, h) not in qts:
            qts[b, h] = prepare_q(b, h)
        qm = jnp.concatenate([qts[b, h][0], zeros] if mhalf == 0 else [zeros, qts[b, h][1]], axis=0)

        def one(rows):
            s = jnp.dot(kall_ref[b, rows, head(h)], qm, preferred_element_type=F32)
            return s, jnp.max(s, axis=0, keepdims=True)
        return one

    def fourier_piece(b, rows):
        def emit():
            o = jnp.dot(pc_ref[rows, :], uc_ref[b], preferred_element_type=F32)
            o = o + jnp.dot(ps_ref[rows, :], us_ref[b], preferred_element_type=F32)
            four_ref[b, rows, :] = o.astype(BF16)
        return emit

    fr = min(FOURIER_PIECE_ROWS, tq)
    pieces = [fourier_piece(b, slice(r * fr, (r + 1) * fr)) for b in range(bt) for r in range(tq // fr)]
    piece_every = len(chains) // len(pieces)

    depth = max(1, min(len(chains), PREFETCH_KEYS // nk))
    queue = [[scores(chain)(rows) for rows in chunks] for chain in chains[:depth]]
    acc = [None, None]
    for i, (b, h, mhalf) in enumerate(chains):
        cur = queue.pop(0)
        mx = functools.reduce(jnp.maximum, [m for _, m in cur])
        nxt = scores(chains[i + depth]) if i + depth < len(chains) else None
        issued = []
        a = None
        for c, rows in enumerate(chunks):
            if nxt is not None:
                issued.append(nxt(rows))
            e = jnp.exp2(cur[c][0] - mx).astype(BF16)
            part = jnp.dot(vt_ref[b, h, :, rows], e, preferred_element_type=F32)
            a = part if a is None else a + part
        if nxt is not None:
            queue.append(issued)
        if i % piece_every == 0:
            pieces[i // piece_every]()
        acc[mhalf] = a
        if mhalf == 1:
            a0, a1 = acc
            r0 = 1.0 / a0[V_DIM:V_DIM + 1, :]
            r1 = lam / a1[V_DIM:V_DIM + 1, :]
            o = (a0[0:V_DIM, :] * r0 - a1[0:V_DIM, :] * r1).T
            ms = jnp.mean(o * o, axis=-1, keepdims=True)
            o = o * lax.rsqrt(ms + EPS) * sg_ref[...] * (1.0 - LAM_INIT)
            o_ref[b, :, head(h)] = o.astype(BF16)


def _attention(lam_vecs, subln_g, qg_col, q, k, v, uc, us, cache_k=None, cache_v=None):
    nb, n, _ = q.shape
    tq = min(ATTN_Q_TILE, n)
    has_cache = cache_k is not None
    past = cache_k.shape[2] if has_cache else 0
    nk = n + past
    bt = max(1, min(nb, ATTN_STEP_KEYS // nk))
    qtile = pl.BlockSpec((bt, tq, QK_WIDTH), lambda b, t: (b, t, 0))
    per_batch = pl.BlockSpec((bt, n, QK_WIDTH), lambda b, t: (b, 0, 0))
    in_specs = [_resident((1, HEAD_DIM))] * 4 + [_resident((1, V_DIM)), _resident((HEAD_DIM, 1)),
                                                   qtile, per_batch, per_batch]
    args = list(lam_vecs) + [subln_g, qg_col, q, k, v]
    if has_cache:
        in_specs += [pl.BlockSpec((bt, QK_WIDTH, past), lambda b, t: (b, 0, 0)),
                     pl.BlockSpec((bt, past * N_HEADS, V_DIM), lambda b, t: (b, 0, 0)),
                     pl.BlockSpec((4, ROPE_AXIS_DIM // 2, tq), lambda b, t: (0, 0, t))]
        args += [cache_k, cache_v, jnp.asarray(_rope_tables_feature_major(n))]
    pc, ps = _position_dft(n)
    dft_rows = pl.BlockSpec((tq, n), lambda b, t: (t, 0))
    in_specs += [dft_rows, dft_rows, per_batch, per_batch]
    args += [jnp.asarray(pc).astype(BF16), jnp.asarray(ps).astype(BF16), uc, us]
    return pl.pallas_call(
        functools.partial(_attn_kernel, has_cache=has_cache, n=n, past=past, bt=bt),
        grid=(nb // bt, n // tq),
        in_specs=in_specs,
        out_specs=[qtile, qtile],
        out_shape=[jax.ShapeDtypeStruct((nb, n, ATTN_WIDTH), BF16),
                   jax.ShapeDtypeStruct((nb, n, FOUR_WIDTH), BF16)],
        scratch_shapes=[pltpu.VMEM((bt, nk, QK_WIDTH), BF16),
                        pltpu.VMEM((bt, N_HEADS, V_DIM + ONES_ROWS, nk), BF16)],
        compiler_params=_params(2),
        name="diff_attention_cached" if has_cache else "diff_attention",
    )(*args)


OUT_TILE = 512
FF_CHUNK = 1024


def _out_mlp_kernel(x_ref, a_ref, f_ref, mod_ref, wo_ref, g2_ref, w1_ref, w2_ref, o_ref, *, mod_row0, per_batch_mod):
    m = _mod_row(mod_ref, mod_row0, per_batch_mod)
    g1 = m[:, 2 * D_MODEL:3 * D_MODEL]
    sh2 = m[:, 3 * D_MODEL:4 * D_MODEL]
    sc2 = m[:, 4 * D_MODEL:5 * D_MODEL]
    g2 = m[:, 5 * D_MODEL:6 * D_MODEL]
    mix = jnp.dot(a_ref[0], wo_ref[0:ATTN_WIDTH, :], preferred_element_type=F32)
    mix = mix + jnp.dot(f_ref[0], wo_ref[ATTN_WIDTH:, :], preferred_element_type=F32)
    x1 = x_ref[0] + g1 * mix
    ms = jnp.mean(x1 * x1, axis=-1, keepdims=True)
    h = x1 * lax.rsqrt(ms + EPS) * g2_ref[...] * (1.0 + sc2) + sh2
    hb = h.astype(BF16)
    acc = jnp.zeros(x1.shape, F32)
    for c in range(D_FF // FF_CHUNK):
        t = jnp.dot(hb, w1_ref[:, c * FF_CHUNK:(c + 1) * FF_CHUNK], preferred_element_type=F32)
        t = jnp.square(jnp.maximum(t, 0.0)).astype(BF16)
        acc = acc + jnp.dot(t, w2_ref[c * FF_CHUNK:(c + 1) * FF_CHUNK, :], preferred_element_type=F32)
    o_ref[0] = x1 + g2 * acc


def _output_mlp(x3, attn, four, mod, mod_row0, per_batch_mod, w_out, norm2_g, w1, w2):
    nb, n, _ = x3.shape
    tm = OUT_TILE
    tile = lambda width: pl.BlockSpec((1, tm, width), lambda t, b: (b, t, 0))
    return pl.pallas_call(
        functools.partial(_out_mlp_kernel, mod_row0=mod_row0, per_batch_mod=per_batch_mod),
        grid=(n // tm, nb),
        in_specs=[tile(D_MODEL), tile(ATTN_WIDTH), tile(FOUR_WIDTH),
                  _resident((MOD_ROWS, 6 * D_MODEL)),
                  _resident((ATTN_WIDTH + FOUR_WIDTH, D_MODEL)),
                  _resident((1, D_MODEL)),
                  _resident((D_MODEL, D_FF)),
                  _resident((D_FF, D_MODEL))],
        out_specs=tile(D_MODEL),
        out_shape=jax.ShapeDtypeStruct((nb, n, D_MODEL), F32),
        compiler_params=_params(2),
        name="output_mlp",
    )(x3, attn, four, mod, w_out, norm2_g, w1, w2)


def kernel(x_prompt, x_sample, c, cache_k, cache_v, c_ctx, w_mod, b_mod, norm1_g, w_in, q_norm_g, k_norm_g,
           lambda_q1, lambda_k1, lambda_q2, lambda_k2, subln_g, w_four, w_out, norm2_g, w1, w2):
    batch, seq, _ = x_prompt.shape
    dec_batch, dec_seq, _ = x_sample.shape
    past = cache_k.shape[2]
    l = 0

    mod, wcs = _modulation_and_fourier_weights(c_ctx[None, :], c, w_mod[l], b_mod[l][None, :], w_four[l])

    w_in_b = w_in[l].astype(BF16)
    w_out_b = w_out[l].astype(BF16)
    w1_b = w1[l].astype(BF16)
    w2_b = w2[l].astype(BF16)
    n1 = norm1_g[l][None, :]
    n2 = norm2_g[l][None, :]
    qg_col = q_norm_g[l][:, None]
    kg = jnp.tile(k_norm_g[l], QK_WIDTH // HEAD_DIM)[None, :]
    sg = subln_g[l][None, :]
    lamv = tuple(t[l][None, :] for t in (lambda_q1, lambda_k1, lambda_q2, lambda_k2))
    gm = jnp.asarray(_head_mean_matrix()).astype(BF16)
    rope_tabs = tuple(jnp.asarray(t) for t in _rope_tables(dec_seq))

    q, k, v, uc, us, kt32, v32 = _in_projection(x_prompt, mod, 0, False, n1, w_in_b, kg, gm, wcs, None, True)
    attn, four = _attention(lamv, sg, qg_col, q, k, v, uc, us)
    flat = lambda t: t.reshape(1, batch * seq, t.shape[-1])
    yp = _output_mlp(flat(x_prompt), flat(attn), flat(four), mod, 0, False, w_out_b, n2, w1_b, w2_b)
    y_prompt = yp.reshape(batch, seq, D_MODEL)
    new_cache_k = kt32.reshape(batch, N_HEADS, 2, HEAD_DIM, seq).transpose(0, 4, 1, 2, 3)[:, None]
    new_cache_v = v32.reshape(batch, 1, seq, N_HEADS, V_DIM)

    q, k, v, uc, us = _in_projection(x_sample, mod, 1, True, n1, w_in_b, kg, gm, wcs, rope_tabs, False)
    ck = cache_k[:, l].transpose(0, 2, 3, 4, 1).reshape(dec_batch, QK_WIDTH, past)
    cv = cache_v[:, l].reshape(dec_batch, past * N_HEADS, V_DIM)
    attn, four = _attention(lamv, sg, qg_col, q, k, v, uc, us, ck, cv)
    y_sample = _output_mlp(x_sample, attn, four, mod, 1, True, w_out_b, n2, w1_b, w2_b)

    return (y_prompt, y_sample, new_cache_k, new_cache_v)
```

```python
import functools
import math

import jax
import jax.numpy as jnp
import numpy as np
from jax import lax
from jax.experimental import pallas as pl
from jax.experimental.pallas import tpu as pltpu

D_MODEL = 1024
N_HEADS = 4
HEAD_DIM = 64
V_DIM = 2 * HEAD_DIM
QK_WIDTH = N_HEADS * 2 * HEAD_DIM
ATTN_WIDTH = N_HEADS * V_DIM
N_FOUR_GROUPS = 4
FOUR_GROUP = 128
FOUR_WIDTH = N_FOUR_GROUPS * FOUR_GROUP
PROJ_WIDTH = 2 * QK_WIDTH + ATTN_WIDTH + FOUR_WIDTH
D_FF = 4 * D_MODEL
GRID_W = 64
ROPE_BASE = 10000.0
ROPE_AXIS_DIM = HEAD_DIM // 2
EPS = 1e-6
LAM_INIT = 0.8 - 0.6 * math.exp(-0.3 * 0)
LOG2E = 1.4426950408889634
MOD_ROWS = 8

F32 = jnp.float32
BF16 = jnp.bfloat16

VMEM_LIMIT_BYTES = 56 * 1024 * 1024


def _params(n_axes):
    return pltpu.CompilerParams(dimension_semantics=("arbitrary",) * n_axes,
                                vmem_limit_bytes=VMEM_LIMIT_BYTES)


def _resident(shape):
    nd = len(shape)
    return pl.BlockSpec(shape, lambda *_: (0,) * nd, pipeline_mode=pl.Buffered(1))


@functools.lru_cache(maxsize=None)
def _channel_dft():
    idx = np.arange(FOUR_GROUP)
    ang = 2.0 * np.pi * ((idx[:, None] * idx[None, :]) % FOUR_GROUP) / FOUR_GROUP
    s = 1.0 / np.sqrt(FOUR_GROUP)
    return np.stack([np.cos(ang) * s, np.sin(ang) * s]).astype(np.float32)


@functools.lru_cache(maxsize=None)
def _position_dft(n):
    idx = np.arange(n)
    ang = 2.0 * np.pi * ((idx[:, None] * idx[None, :]) % n) / n
    s = 1.0 / np.sqrt(n)
    return (np.cos(ang) * s).astype(np.float32), (-np.sin(ang) * s).astype(np.float32)


@functools.lru_cache(maxsize=None)
def _rope_tables(n):
    rows = n // GRID_W
    row = np.repeat(np.arange(rows), GRID_W).astype(np.float64)
    col = np.tile(np.arange(GRID_W), rows).astype(np.float64)
    inv = ROPE_BASE ** (-np.arange(0, ROPE_AXIS_DIM, 2, dtype=np.float64) / ROPE_AXIS_DIM)
    d = np.arange(QK_WIDTH) % HEAD_DIM
    part = d // ROPE_AXIS_DIM
    i = d % ROPE_AXIS_DIM
    first = i < ROPE_AXIS_DIM // 2
    pos = np.where(part[None, :] == 0, row[:, None], col[:, None])
    ang = pos * inv[i % (ROPE_AXIS_DIM // 2)][None, :]
    cos = np.cos(ang)
    sin = np.where(first[None, :], -np.sin(ang), np.sin(ang))
    return cos.astype(np.float32), sin.astype(np.float32)


@functools.lru_cache(maxsize=None)
def _rope_tables_feature_major(n):
    rows = n // GRID_W
    row = np.repeat(np.arange(rows), GRID_W).astype(np.float64)
    col = np.tile(np.arange(GRID_W), rows).astype(np.float64)
    inv = ROPE_BASE ** (-np.arange(0, ROPE_AXIS_DIM, 2, dtype=np.float64) / ROPE_AXIS_DIM)
    ang_r = inv[:, None] * row[None, :]
    ang_c = inv[:, None] * col[None, :]
    return np.stack([np.cos(ang_r), np.sin(ang_r), np.cos(ang_c), np.sin(ang_c)]).astype(np.float32)


@functools.lru_cache(maxsize=None)
def _head_mean_matrix():
    g = np.kron(np.eye(QK_WIDTH // HEAD_DIM), np.ones((HEAD_DIM, HEAD_DIM))) / HEAD_DIM
    return g.astype(np.float32)


MOD_TILE = 512
MOD_HEAD_COLS = 2 * D_MODEL
MOD_TAIL_COLS = 4 * D_MODEL


def _modulation_block(cc_ref, c_ref, w, b_ref, o_ref):
    nd = c_ref.shape[0]

    def rows(cvec):
        s = cvec / (1.0 + jnp.exp(-cvec))
        return jnp.dot(s.astype(BF16), w, preferred_element_type=F32) + b_ref[...]

    o_ref[0:1, :] = rows(cc_ref[...])
    o_ref[1:1 + nd, :] = rows(c_ref[...])
    o_ref[1 + nd:, :] = jnp.zeros((MOD_ROWS - 1 - nd, o_ref.shape[1]), F32)


def _mod_kernel(cc_ref, c_ref, w_ref, b_ref, dft_ref, wf_ref, win_ref, o_ref, wcs_ref, winb_ref):
    winb_ref[...] = win_ref[...].astype(BF16)

    @pl.when(pl.program_id(0) == 0)
    def _fourier_weights():
        for g in range(N_FOUR_GROUPS):
            w = wf_ref[g]
            wc = jnp.dot(dft_ref[0], w, precision=lax.Precision.HIGHEST, preferred_element_type=F32)
            ws = jnp.dot(dft_ref[1], w, precision=lax.Precision.HIGHEST, preferred_element_type=F32)
            wcs_ref[g] = jnp.concatenate([wc, ws], axis=-1).astype(BF16)

    _modulation_block(cc_ref, c_ref, w_ref[...].astype(BF16), b_ref, o_ref)


def _modulation_head_and_fourier_weights(c_ctx, c, w_mod, b_mod, w_four, w_in):
    nd = c.shape[0]
    steps = MOD_HEAD_COLS // MOD_TILE
    dft = jnp.asarray(_channel_dft())
    slab = pl.BlockSpec((w_in.shape[0] // steps, w_in.shape[1]), lambda j: (j, 0))
    return pl.pallas_call(
        _mod_kernel,
        grid=(steps,),
        in_specs=[pl.BlockSpec((1, D_MODEL), lambda j: (0, 0)),
                  pl.BlockSpec((nd, D_MODEL), lambda j: (0, 0)),
                  pl.BlockSpec((D_MODEL, MOD_TILE), lambda j: (0, j)),
                  pl.BlockSpec((1, MOD_TILE), lambda j: (0, j)),
                  pl.BlockSpec((2, FOUR_GROUP, FOUR_GROUP), lambda j: (0, 0, 0)),
                  pl.BlockSpec((N_FOUR_GROUPS, FOUR_GROUP, FOUR_GROUP), lambda j: (0, 0, 0)),
                  slab],
        out_specs=[pl.BlockSpec((MOD_ROWS, MOD_TILE), lambda j: (0, j)),
                   pl.BlockSpec((N_FOUR_GROUPS, FOUR_GROUP, 2 * FOUR_GROUP), lambda j: (0, 0, 0)),
                   slab],
        out_shape=[jax.ShapeDtypeStruct((MOD_ROWS, MOD_HEAD_COLS), F32),
                   jax.ShapeDtypeStruct((N_FOUR_GROUPS, FOUR_GROUP, 2 * FOUR_GROUP), BF16),
                   jax.ShapeDtypeStruct(w_in.shape, BF16)],
        compiler_params=_params(1),
        name="modulation",
    )(c_ctx, c, w_mod, b_mod, dft, w_four, w_in)


PROJ_TILE = 1024


def _mod_row(mod_ref, mod_row0, per_batch_mod):
    if per_batch_mod:
        return mod_ref[pl.ds(mod_row0 + pl.program_id(1), 1), :]
    return mod_ref[mod_row0:mod_row0 + 1, :]


def _inproj_kernel(*refs, rope, emit_cache, bt, tn, mod_row0, per_batch_mod):
    it = iter(refs)
    x_ref, mod_ref, g1_ref, w_ref, kg_ref, gm_ref, wcs_ref = (next(it) for _ in range(7))
    if rope:
        cos_ref, sin_ref = next(it), next(it)
    q_ref, k_ref, v_ref, uc_ref, us_ref = (next(it) for _ in range(5))
    if emit_cache:
        k32_ref, v32_ref = next(it), next(it)

    x = x_ref[...].reshape(bt * tn, D_MODEL)
    m = _mod_row(mod_ref, mod_row0, per_batch_mod)
    sh1 = m[:, 0:D_MODEL]
    sc1 = m[:, D_MODEL:2 * D_MODEL]
    ms = jnp.mean(x * x, axis=-1, keepdims=True)
    y = x * lax.rsqrt(ms + EPS) * g1_ref[...]
    h = y * (1.0 + sc1) + sh1
    p = jnp.dot(h.astype(BF16), w_ref[...], preferred_element_type=F32)
    q = p[:, 0:QK_WIDTH]
    k = p[:, QK_WIDTH:2 * QK_WIDTH]
    v = p[:, 2 * QK_WIDTH:2 * QK_WIDTH + ATTN_WIDTH]

    msq = jnp.dot((k * k).astype(BF16), gm_ref[...], preferred_element_type=F32)
    k = k * lax.rsqrt(msq + EPS) * kg_ref[...]
    if emit_cache:
        for i in range(bt):
            rows = slice(i * tn, (i + 1) * tn)
            k32_ref[i] = k[rows, :].T
            for hd in range(N_HEADS):
                v32_ref[i, pl.ds(hd, tn, stride=N_HEADS), :] = v[rows, hd * V_DIM:(hd + 1) * V_DIM]

    if rope:
        lane = lax.broadcasted_iota(jnp.int32, k.shape, 1)
        first = (lane % ROPE_AXIS_DIM) < (ROPE_AXIS_DIM // 2)
        half = ROPE_AXIS_DIM // 2
        sw = jnp.where(first, pltpu.roll(k, QK_WIDTH - half, 1), pltpu.roll(k, half, 1))
        k = k * cos_ref[...] + sw * sin_ref[...]

    blk = lambda t: t.reshape(bt, tn, t.shape[-1])
    q_ref[...] = blk(q)
    k_ref[...] = blk(k.astype(BF16))
    v_ref[...] = blk(v.astype(BF16))

    f0 = 2 * QK_WIDTH + ATTN_WIDTH
    for g in range(N_FOUR_GROUPS):
        fg = p[:, f0 + g * FOUR_GROUP:f0 + (g + 1) * FOUR_GROUP].astype(BF16)
        u = jnp.dot(fg, wcs_ref[g], preferred_element_type=F32)
        uc_ref[:, :, g * FOUR_GROUP:(g + 1) * FOUR_GROUP] = blk(u[:, 0:FOUR_GROUP].astype(BF16))
        us_ref[:, :, g * FOUR_GROUP:(g + 1) * FOUR_GROUP] = blk(u[:, FOUR_GROUP:].astype(BF16))


def _in_projection(x3, mod, mod_row0, per_batch_mod, norm_g, w_in, kg, gm, wcs, rope_tabs, emit_cache):
    nb, n, _ = x3.shape
    tn = min(PROJ_TILE, n)
    bt = PROJ_TILE // tn
    assert per_batch_mod is False or bt == 1
    rope = rope_tabs is not None
    tile = lambda width: pl.BlockSpec((bt, tn, width), lambda t, b: (b, t, 0))
    in_specs = [tile(D_MODEL),
                _resident((MOD_ROWS, MOD_HEAD_COLS)),
                _resident((1, D_MODEL)),
                _resident((D_MODEL, PROJ_WIDTH)),
                _resident((1, QK_WIDTH)),
                _resident((QK_WIDTH, QK_WIDTH)),
                _resident((N_FOUR_GROUPS, FOUR_GROUP, 2 * FOUR_GROUP))]
    args = [x3, mod, norm_g, w_in, kg, gm, wcs]
    if rope:
        in_specs += [pl.BlockSpec((tn, QK_WIDTH), lambda t, b: (t, 0))] * 2
        args += list(rope_tabs)
    out_shape = [jax.ShapeDtypeStruct((nb, n, QK_WIDTH), F32)]
    out_shape += [jax.ShapeDtypeStruct((nb, n, QK_WIDTH), BF16)] * 4
    out_specs = [tile(QK_WIDTH)] * 5
    if emit_cache:
        assert tn == n
        out_shape += [jax.ShapeDtypeStruct((nb, QK_WIDTH, n), F32),
                      jax.ShapeDtypeStruct((nb, n * N_HEADS, V_DIM), F32)]
        out_specs += [pl.BlockSpec((bt, QK_WIDTH, n), lambda t, b: (b, 0, 0)),
                      pl.BlockSpec((bt, n * N_HEADS, V_DIM), lambda t, b: (b, 0, 0))]
    return pl.pallas_call(
        functools.partial(_inproj_kernel, rope=rope, emit_cache=emit_cache, bt=bt, tn=tn,
                          mod_row0=mod_row0, per_batch_mod=per_batch_mod),
        grid=(n // tn, nb // bt),
        in_specs=in_specs,
        out_specs=out_specs,
        out_shape=out_shape,
        compiler_params=_params(2),
        name="in_projection_rope" if rope else "in_projection",
    )(*args)


ATTN_Q_TILE = 512
KEY_CHUNK = 1280
ONES_ROWS = 16
FOURIER_PIECE_ROWS = 128
PREFETCH_KEYS = 1024
ATTN_STEP_KEYS = 1024


def _attn_kernel(*refs, has_cache, n, past, bt, n_cast):
    it = iter(refs)
    lq1_ref, lk1_ref, lq2_ref, lk2_ref, sg_ref, qg_ref, q_ref, k_ref, v_ref = (next(it) for _ in range(9))
    if has_cache:
        ck_ref, cv_ref, rq_ref = next(it), next(it), next(it)
    pc_ref, ps_ref, uc_ref, us_ref = (next(it) for _ in range(4))
    cast_in = [next(it) for _ in range(n_cast)]
    if n_cast:
        cc_ref, c_ref, wm_ref, bm_ref = (next(it) for _ in range(4))
    o_ref, four_ref = next(it), next(it)
    for src, dst in zip(cast_in, [next(it) for _ in range(n_cast)]):
        dst[...] = src[...].astype(BF16)
    if n_cast:
        _modulation_block(cc_ref, c_ref, wm_ref[...].astype(BF16), bm_ref, next(it))
    kall_ref, vt_ref = next(it), next(it)
    nk = n + past
    tq = q_ref.shape[1]

    @pl.when(pl.program_id(1) == 0)
    def _per_batch_setup():
        ones = jnp.ones((ONES_ROWS, nk), BF16)
        for b in range(bt):
            if has_cache:
                kall_ref[b, 0:past, :] = ck_ref[b].T.astype(BF16)
            kall_ref[b, past:nk, :] = k_ref[b]
            for h in range(N_HEADS):
                if has_cache:
                    cvh = cv_ref[b, pl.ds(h, past, stride=N_HEADS), :]
                    vt_ref[b, h, 0:V_DIM, 0:past] = cvh.T.astype(BF16)
                vh = v_ref[b, :, h * V_DIM:(h + 1) * V_DIM].astype(F32)
                vt_ref[b, h, 0:V_DIM, past:nk] = vh.T.astype(BF16)
                vt_ref[b, h, V_DIM:V_DIM + ONES_ROWS, :] = ones

    l1 = jnp.exp(jnp.sum(lq1_ref[...] * lk1_ref[...], axis=-1, keepdims=True))
    l2 = jnp.exp(jnp.sum(lq2_ref[...] * lk2_ref[...], axis=-1, keepdims=True))
    lam = l1 - l2 + LAM_INIT

    head = lambda h: slice(h * V_DIM, (h + 1) * V_DIM)
    qscale = qg_ref[...] * (LOG2E / math.sqrt(HEAD_DIM))
    zeros = jnp.zeros((HEAD_DIM, tq), BF16)
    nf = ROPE_AXIS_DIM // 2

    def prepare_q(b, h):
        qt = q_ref[b, :, head(h)].T
        halves = []
        for mhalf in range(2):
            t = qt[mhalf * HEAD_DIM:(mhalf + 1) * HEAD_DIM, :]
            t = t * lax.rsqrt(jnp.mean(t * t, axis=0, keepdims=True) + EPS) * qscale
            if has_cache:
                parts = []
                for axis in range(2):
                    x1 = t[(2 * axis) * nf:(2 * axis + 1) * nf, :]
                    x2 = t[(2 * axis + 1) * nf:(2 * axis + 2) * nf, :]
                    cos, sin = rq_ref[2 * axis], rq_ref[2 * axis + 1]
                    parts += [x1 * cos - x2 * sin, x2 * cos + x1 * sin]
                t = jnp.concatenate(parts, axis=0)
            halves.append(t.astype(BF16))
        return halves

    kc = min(KEY_CHUNK, nk)
    chunks = [slice(c * kc, (c + 1) * kc) for c in range(nk // kc)]
    chains = [(b, h, mhalf) for b in range(bt) for h in range(N_HEADS) for mhalf in range(2)]
    qts = {}

    def scores(chain):
        b, h, mhalf = chain
        if (b, h) not in qts:
            qts[b, h] = prepare_q(b, h)
        qm = jnp.concatenate([qts[b, h][0], zeros] if mhalf == 0 else [zeros, qts[b, h][1]], axis=0)

        def one(rows):
            s = jnp.dot(kall_ref[b, rows, head(h)], qm, preferred_element_type=F32)
            return s, jnp.max(s, axis=0, keepdims=True)
        return one

    def fourier_piece(b, rows):
        def emit():
            o = jnp.dot(pc_ref[rows, :], uc_ref[b], preferred_element_type=F32)
            o = o + jnp.dot(ps_ref[rows, :], us_ref[b], preferred_element_type=F32)
            four_ref[b, rows, :] = o.astype(BF16)
        return emit

    fr = min(FOURIER_PIECE_ROWS, tq)
    pieces = [fourier_piece(b, slice(r * fr, (r + 1) * fr)) for b in range(bt) for r in range(tq // fr)]
    piece_every = len(chains) // len(pieces)

    depth = max(1, min(len(chains), PREFETCH_KEYS // nk))
    queue = [[scores(chain)(rows) for rows in chunks] for chain in chains[:depth]]
    acc = [None, None]
    for i, (b, h, mhalf) in enumerate(chains):
        cur = queue.pop(0)
        mx = functools.reduce(jnp.maximum, [m for _, m in cur])
        nxt = scores(chains[i + depth]) if i + depth < len(chains) else None
        issued = []
        a = None
        for c, rows in enumerate(chunks):
            if nxt is not None:
                issued.append(nxt(rows))
            e = jnp.exp2(cur[c][0] - mx).astype(BF16)
            part = jnp.dot(vt_ref[b, h, :, rows], e, preferred_element_type=F32)
            a = part if a is None else a + part
        if nxt is not None:
            queue.append(issued)
        if i % piece_every == 0:
            pieces[i // piece_every]()
        acc[mhalf] = a
        if mhalf == 1:
            a0, a1 = acc
            r0 = 1.0 / a0[V_DIM:V_DIM + 1, :]
            r1 = lam / a1[V_DIM:V_DIM + 1, :]
            o = (a0[0:V_DIM, :] * r0 - a1[0:V_DIM, :] * r1).T
            ms = jnp.mean(o * o, axis=-1, keepdims=True)
            o = o * lax.rsqrt(ms + EPS) * sg_ref[...] * (1.0 - LAM_INIT)
            o_ref[b, :, head(h)] = o.astype(BF16)


def _attention(lam_vecs, subln_g, qg_col, q, k, v, uc, us, cache_k=None, cache_v=None, side_jobs=None):
    nb, n, _ = q.shape
    tq = min(ATTN_Q_TILE, n)
    has_cache = cache_k is not None
    past = cache_k.shape[2] if has_cache else 0
    nk = n + past
    bt = max(1, min(nb, ATTN_STEP_KEYS // nk))
    qtile = pl.BlockSpec((bt, tq, QK_WIDTH), lambda b, t: (b, t, 0))
    per_batch = pl.BlockSpec((bt, n, QK_WIDTH), lambda b, t: (b, 0, 0))
    in_specs = [_resident((1, HEAD_DIM))] * 4 + [_resident((1, V_DIM)), _resident((HEAD_DIM, 1)),
                                                   qtile, per_batch, per_batch]
    args = list(lam_vecs) + [subln_g, qg_col, q, k, v]
    if has_cache:
        in_specs += [pl.BlockSpec((bt, QK_WIDTH, past), lambda b, t: (b, 0, 0)),
                     pl.BlockSpec((bt, past * N_HEADS, V_DIM), lambda b, t: (b, 0, 0)),
                     pl.BlockSpec((4, ROPE_AXIS_DIM // 2, tq), lambda b, t: (0, 0, t))]
        args += [cache_k, cache_v, jnp.asarray(_rope_tables_feature_major(n))]
    pc, ps = _position_dft(n)
    dft_rows = pl.BlockSpec((tq, n), lambda b, t: (t, 0))
    in_specs += [dft_rows, dft_rows, per_batch, per_batch]
    args += [jnp.asarray(pc).astype(BF16), jnp.asarray(ps).astype(BF16), uc, us]
    nt = n // tq
    steps = (nb // bt) * nt
    out_specs = [qtile, qtile]
    out_shape = [jax.ShapeDtypeStruct((nb, n, ATTN_WIDTH), BF16), jax.ShapeDtypeStruct((nb, n, FOUR_WIDTH), BF16)]
    cast_weights = ()
    if side_jobs is not None:
        cast_weights, (c_ctx, c, w_mod, b_mod) = side_jobs
        slabs = [pl.BlockSpec((w.shape[0] // steps, w.shape[1]), lambda b, t: (b * nt + t, 0)) for w in cast_weights]
        tail = MOD_TAIL_COLS // steps
        first = MOD_HEAD_COLS // tail
        in_specs += slabs + [_resident(c_ctx.shape), _resident(c.shape),
                             pl.BlockSpec((D_MODEL, tail), lambda b, t: (0, first + b * nt + t)),
                             pl.BlockSpec((1, tail), lambda b, t: (0, first + b * nt + t))]
        args += list(cast_weights) + [c_ctx, c, w_mod, b_mod]
        out_specs += slabs + [pl.BlockSpec((MOD_ROWS, tail), lambda b, t: (0, b * nt + t))]
        out_shape += [jax.ShapeDtypeStruct(w.shape, BF16) for w in cast_weights]
        out_shape += [jax.ShapeDtypeStruct((MOD_ROWS, MOD_TAIL_COLS), F32)]
    return pl.pallas_call(
        functools.partial(_attn_kernel, has_cache=has_cache, n=n, past=past, bt=bt, n_cast=len(cast_weights)),
        grid=(nb // bt, nt),
        in_specs=in_specs,
        out_specs=out_specs,
        out_shape=out_shape,
        scratch_shapes=[pltpu.VMEM((bt, nk, QK_WIDTH), BF16),
                        pltpu.VMEM((bt, N_HEADS, V_DIM + ONES_ROWS, nk), BF16)],
        compiler_params=_params(2),
        name="diff_attention_cached" if has_cache else "diff_attention",
    )(*args)


OUT_TILE = 512
FF_CHUNK = 1024


def _out_mlp_kernel(x_ref, a_ref, f_ref, mod_ref, wo_ref, g2_ref, w1_ref, w2_ref, o_ref, *, mod_row0, per_batch_mod):
    m = _mod_row(mod_ref, mod_row0, per_batch_mod)
    g1 = m[:, 0:D_MODEL]
    sh2 = m[:, D_MODEL:2 * D_MODEL]
    sc2 = m[:, 2 * D_MODEL:3 * D_MODEL]
    g2 = m[:, 3 * D_MODEL:4 * D_MODEL]
    mix = jnp.dot(a_ref[0], wo_ref[0:ATTN_WIDTH, :], preferred_element_type=F32)
    mix = mix + jnp.dot(f_ref[0], wo_ref[ATTN_WIDTH:, :], preferred_element_type=F32)
    x1 = x_ref[0] + g1 * mix
    ms = jnp.mean(x1 * x1, axis=-1, keepdims=True)
    h = x1 * lax.rsqrt(ms + EPS) * g2_ref[...] * (1.0 + sc2) + sh2
    hb = h.astype(BF16)
    acc = jnp.zeros(x1.shape, F32)
    for c in range(D_FF // FF_CHUNK):
        t = jnp.dot(hb, w1_ref[:, c * FF_CHUNK:(c + 1) * FF_CHUNK], preferred_element_type=F32)
        t = jnp.square(jnp.maximum(t, 0.0)).astype(BF16)
        acc = acc + jnp.dot(t, w2_ref[c * FF_CHUNK:(c + 1) * FF_CHUNK, :], preferred_element_type=F32)
    o_ref[0] = x1 + g2 * acc


def _output_mlp(x3, attn, four, mod, mod_row0, per_batch_mod, w_out, norm2_g, w1, w2):
    nb, n, _ = x3.shape
    tm = OUT_TILE
    tile = lambda width: pl.BlockSpec((1, tm, width), lambda t, b: (b, t, 0))
    return pl.pallas_call(
        functools.partial(_out_mlp_kernel, mod_row0=mod_row0, per_batch_mod=per_batch_mod),
        grid=(n // tm, nb),
        in_specs=[tile(D_MODEL), tile(ATTN_WIDTH), tile(FOUR_WIDTH),
                  _resident((MOD_ROWS, MOD_TAIL_COLS)),
                  _resident((ATTN_WIDTH + FOUR_WIDTH, D_MODEL)),
                  _resident((1, D_MODEL)),
                  _resident((D_MODEL, D_FF)),
                  _resident((D_FF, D_MODEL))],
        out_specs=tile(D_MODEL),
        out_shape=jax.ShapeDtypeStruct((nb, n, D_MODEL), F32),
        compiler_params=_params(2),
        name="output_mlp",
    )(x3, attn, four, mod, w_out, norm2_g, w1, w2)


def kernel(x_prompt, x_sample, c, cache_k, cache_v, c_ctx, w_mod, b_mod, norm1_g, w_in, q_norm_g, k_norm_g,
           lambda_q1, lambda_k1, lambda_q2, lambda_k2, subln_g, w_four, w_out, norm2_g, w1, w2):
    batch, seq, _ = x_prompt.shape
    dec_batch, dec_seq, _ = x_sample.shape
    past = cache_k.shape[2]
    l = 0

    cc = c_ctx[None, :]
    bm = b_mod[l][None, :]
    mod, wcs, w_in_b = _modulation_head_and_fourier_weights(cc, c, w_mod[l], bm, w_four[l], w_in[l])

    n1 = norm1_g[l][None, :]
    n2 = norm2_g[l][None, :]
    qg_col = q_norm_g[l][:, None]
    kg = jnp.tile(k_norm_g[l], QK_WIDTH // HEAD_DIM)[None, :]
    sg = subln_g[l][None, :]
    lamv = tuple(t[l][None, :] for t in (lambda_q1, lambda_k1, lambda_q2, lambda_k2))
    gm = jnp.asarray(_head_mean_matrix()).astype(BF16)
    rope_tabs = tuple(jnp.asarray(t) for t in _rope_tables(dec_seq))

    qp, kp, vp, ucp, usp, kt32, v32 = _in_projection(x_prompt, mod, 0, False, n1, w_in_b, kg, gm, wcs, None, True)
    q, k, v, uc, us = _in_projection(x_sample, mod, 1, True, n1, w_in_b, kg, gm, wcs, rope_tabs, False)

    ck = cache_k[:, l].transpose(0, 2, 3, 4, 1).reshape(dec_batch, QK_WIDTH, past)
    cv = cache_v[:, l].reshape(dec_batch, past * N_HEADS, V_DIM)
    attn_s, four_s, w_out_b, w1_b, w2_b, mod_tail = _attention(
        lamv, sg, qg_col, q, k, v, uc, us, ck, cv, ((w_out[l], w1[l], w2[l]), (cc, c, w_mod[l], bm)))

    attn, four = _attention(lamv, sg, qg_col, qp, kp, vp, ucp, usp)
    flat = lambda t: t.reshape(1, batch * seq, t.shape[-1])
    yp = _output_mlp(flat(x_prompt), flat(attn), flat(four), mod_tail, 0, False, w_out_b, n2, w1_b, w2_b)
    y_prompt = yp.reshape(batch, seq, D_MODEL)
    new_cache_k = kt32.reshape(batch, N_HEADS, 2, HEAD_DIM, seq).transpose(0, 4, 1, 2, 3)[:, None]
    new_cache_v = v32.reshape(batch, 1, seq, N_HEADS, V_DIM)

    y_sample = _output_mlp(x_sample, attn_s, four_s, mod_tail, 1, True, w_out_b, n2, w1_b, w2_b)

    return (y_prompt, y_sample, new_cache_k, new_cache_v)
```

```python
import functools
import math

import jax
import jax.numpy as jnp
import numpy as np
from jax import lax
from jax.experimental import pallas as pl
from jax.experimental.pallas import tpu as pltpu

D_MODEL = 1024
N_HEADS = 4
HEAD_DIM = 64
V_DIM = 2 * HEAD_DIM
QK_WIDTH = N_HEADS * 2 * HEAD_DIM
ATTN_WIDTH = N_HEADS * V_DIM
N_FOUR_GROUPS = 4
FOUR_GROUP = 128
FOUR_WIDTH = N_FOUR_GROUPS * FOUR_GROUP
PROJ_WIDTH = 2 * QK_WIDTH + ATTN_WIDTH + FOUR_WIDTH
D_FF = 4 * D_MODEL
GRID_W = 64
ROPE_BASE = 10000.0
ROPE_AXIS_DIM = HEAD_DIM // 2
EPS = 1e-6
LAM_INIT = 0.8 - 0.6 * math.exp(-0.3 * 0)
LOG2E = 1.4426950408889634
MOD_ROWS = 8

F32 = jnp.float32
BF16 = jnp.bfloat16

VMEM_LIMIT_BYTES = 60 * 1024 * 1024


def _params(n_axes):
    return pltpu.CompilerParams(dimension_semantics=("arbitrary",) * n_axes,
                                vmem_limit_bytes=VMEM_LIMIT_BYTES)


def _resident(shape):
    nd = len(shape)
    return pl.BlockSpec(shape, lambda *_: (0,) * nd, pipeline_mode=pl.Buffered(1))


@functools.lru_cache(maxsize=None)
def _channel_dft():
    idx = np.arange(FOUR_GROUP)
    ang = 2.0 * np.pi * ((idx[:, None] * idx[None, :]) % FOUR_GROUP) / FOUR_GROUP
    s = 1.0 / np.sqrt(FOUR_GROUP)
    return np.stack([np.cos(ang) * s, np.sin(ang) * s]).astype(np.float32)


@functools.lru_cache(maxsize=None)
def _position_dft(n):
    idx = np.arange(n)
    ang = 2.0 * np.pi * ((idx[:, None] * idx[None, :]) % n) / n
    s = 1.0 / np.sqrt(n)
    return (np.cos(ang) * s).astype(np.float32), (-np.sin(ang) * s).astype(np.float32)


@functools.lru_cache(maxsize=None)
def _fold_permutation(n):
    h = n // 2
    p = np.zeros((h, h), np.float32)
    i = np.arange(1, h)
    p[i, h - i] = 1.0
    return p


@functools.lru_cache(maxsize=None)
def _rope_tables(n):
    rows = n // GRID_W
    row = np.repeat(np.arange(rows), GRID_W).astype(np.float64)
    col = np.tile(np.arange(GRID_W), rows).astype(np.float64)
    inv = ROPE_BASE ** (-np.arange(0, ROPE_AXIS_DIM, 2, dtype=np.float64) / ROPE_AXIS_DIM)
    d = np.arange(QK_WIDTH) % HEAD_DIM
    part = d // ROPE_AXIS_DIM
    i = d % ROPE_AXIS_DIM
    first = i < ROPE_AXIS_DIM // 2
    pos = np.where(part[None, :] == 0, row[:, None], col[:, None])
    ang = pos * inv[i % (ROPE_AXIS_DIM // 2)][None, :]
    cos = np.cos(ang)
    sin = np.where(first[None, :], -np.sin(ang), np.sin(ang))
    return cos.astype(np.float32), sin.astype(np.float32)


@functools.lru_cache(maxsize=None)
def _rope_tables_feature_major(n):
    rows = n // GRID_W
    row = np.repeat(np.arange(rows), GRID_W).astype(np.float64)
    col = np.tile(np.arange(GRID_W), rows).astype(np.float64)
    inv = ROPE_BASE ** (-np.arange(0, ROPE_AXIS_DIM, 2, dtype=np.float64) / ROPE_AXIS_DIM)
    ang_r = inv[:, None] * row[None, :]
    ang_c = inv[:, None] * col[None, :]
    return np.stack([np.cos(ang_r), np.sin(ang_r), np.cos(ang_c), np.sin(ang_c)]).astype(np.float32)


@functools.lru_cache(maxsize=None)
def _head_mean_matrix():
    g = np.kron(np.eye(QK_WIDTH // HEAD_DIM), np.ones((HEAD_DIM, HEAD_DIM))) / HEAD_DIM
    return g.astype(np.float32)


MOD_TILE = 512
MOD_HEAD_COLS = 2 * D_MODEL
MOD_TAIL_COLS = 4 * D_MODEL


def _modulation_block(cc_ref, c_ref, w, b_ref, o_ref):
    nd = c_ref.shape[0]

    def rows(cvec):
        s = cvec / (1.0 + jnp.exp(-cvec))
        return jnp.dot(s.astype(BF16), w, preferred_element_type=F32) + b_ref[...]

    o_ref[0:1, :] = rows(cc_ref[...])
    o_ref[1:1 + nd, :] = rows(c_ref[...])
    o_ref[1 + nd:, :] = jnp.zeros((MOD_ROWS - 1 - nd, o_ref.shape[1]), F32)


def _mod_kernel(cc_ref, c_ref, w_ref, b_ref, dft_ref, wf_ref, win_ref, o_ref, wcs_ref, winb_ref):
    winb_ref[...] = win_ref[...].astype(BF16)

    @pl.when(pl.program_id(0) == 0)
    def _fourier_weights():
        for g in range(N_FOUR_GROUPS):
            w = wf_ref[g]
            wc = jnp.dot(dft_ref[0], w, precision=lax.Precision.HIGHEST, preferred_element_type=F32)
            ws = jnp.dot(dft_ref[1], w, precision=lax.Precision.HIGHEST, preferred_element_type=F32)
            wcs_ref[g] = jnp.concatenate([wc, ws], axis=-1).astype(BF16)

    _modulation_block(cc_ref, c_ref, w_ref[...].astype(BF16), b_ref, o_ref)


def _modulation_head_and_fourier_weights(c_ctx, c, w_mod, b_mod, w_four, w_in):
    nd = c.shape[0]
    steps = MOD_HEAD_COLS // MOD_TILE
    dft = jnp.asarray(_channel_dft())
    slab = pl.BlockSpec((w_in.shape[0] // steps, w_in.shape[1]), lambda j: (j, 0))
    return pl.pallas_call(
        _mod_kernel,
        grid=(steps,),
        in_specs=[pl.BlockSpec((1, D_MODEL), lambda j: (0, 0)),
                  pl.BlockSpec((nd, D_MODEL), lambda j: (0, 0)),
                  pl.BlockSpec((D_MODEL, MOD_TILE), lambda j: (0, j)),
                  pl.BlockSpec((1, MOD_TILE), lambda j: (0, j)),
                  pl.BlockSpec((2, FOUR_GROUP, FOUR_GROUP), lambda j: (0, 0, 0)),
                  pl.BlockSpec((N_FOUR_GROUPS, FOUR_GROUP, FOUR_GROUP), lambda j: (0, 0, 0)),
                  slab],
        out_specs=[pl.BlockSpec((MOD_ROWS, MOD_TILE), lambda j: (0, j)),
                   pl.BlockSpec((N_FOUR_GROUPS, FOUR_GROUP, 2 * FOUR_GROUP), lambda j: (0, 0, 0)),
                   slab],
        out_shape=[jax.ShapeDtypeStruct((MOD_ROWS, MOD_HEAD_COLS), F32),
                   jax.ShapeDtypeStruct((N_FOUR_GROUPS, FOUR_GROUP, 2 * FOUR_GROUP), BF16),
                   jax.ShapeDtypeStruct(w_in.shape, BF16)],
        compiler_params=_params(1),
        name="modulation",
    )(c_ctx, c, w_mod, b_mod, dft, w_four, w_in)


PROJ_TILE = 1024


def _mod_row(mod_ref, mod_row0, per_batch_mod):
    if per_batch_mod:
        return mod_ref[pl.ds(mod_row0 + pl.program_id(1), 1), :]
    return mod_ref[mod_row0:mod_row0 + 1, :]


def _inproj_kernel(*refs, rope, emit_cache, bt, tn, mod_row0, per_batch_mod):
    it = iter(refs)
    x_ref, mod_ref, g1_ref, w_ref, kg_ref, gm_ref, wcs_ref = (next(it) for _ in range(7))
    if rope:
        cos_ref, sin_ref = next(it), next(it)
    q_ref, k_ref, v_ref, uc_ref, us_ref = (next(it) for _ in range(5))
    if emit_cache:
        k32_ref, v32_ref = next(it), next(it)

    x = x_ref[...].reshape(bt * tn, D_MODEL)
    m = _mod_row(mod_ref, mod_row0, per_batch_mod)
    sh1 = m[:, 0:D_MODEL]
    sc1 = m[:, D_MODEL:2 * D_MODEL]
    ms = jnp.mean(x * x, axis=-1, keepdims=True)
    y = x * lax.rsqrt(ms + EPS) * g1_ref[...]
    h = y * (1.0 + sc1) + sh1
    p = jnp.dot(h.astype(BF16), w_ref[...], preferred_element_type=F32)
    q = p[:, 0:QK_WIDTH]
    k = p[:, QK_WIDTH:2 * QK_WIDTH]
    v = p[:, 2 * QK_WIDTH:2 * QK_WIDTH + ATTN_WIDTH]

    msq = jnp.dot((k * k).astype(BF16), gm_ref[...], preferred_element_type=F32)
    k = k * lax.rsqrt(msq + EPS) * kg_ref[...]
    if emit_cache:
        for i in range(bt):
            rows = slice(i * tn, (i + 1) * tn)
            k32_ref[i] = k[rows, :].T
            for hd in range(N_HEADS):
                v32_ref[i, pl.ds(hd, tn, stride=N_HEADS), :] = v[rows, hd * V_DIM:(hd + 1) * V_DIM]

    if rope:
        lane = lax.broadcasted_iota(jnp.int32, k.shape, 1)
        first = (lane % ROPE_AXIS_DIM) < (ROPE_AXIS_DIM // 2)
        half = ROPE_AXIS_DIM // 2
        sw = jnp.where(first, pltpu.roll(k, QK_WIDTH - half, 1), pltpu.roll(k, half, 1))
        k = k * cos_ref[...] + sw * sin_ref[...]

    blk = lambda t: t.reshape(bt, tn, t.shape[-1])
    q_ref[...] = blk(q)
    k_ref[...] = blk(k.astype(BF16))
    v_ref[...] = blk(v.astype(BF16))

    f0 = 2 * QK_WIDTH + ATTN_WIDTH
    for g in range(N_FOUR_GROUPS):
        fg = p[:, f0 + g * FOUR_GROUP:f0 + (g + 1) * FOUR_GROUP].astype(BF16)
        u = jnp.dot(fg, wcs_ref[g], preferred_element_type=F32)
        uc_ref[:, :, g * FOUR_GROUP:(g + 1) * FOUR_GROUP] = blk(u[:, 0:FOUR_GROUP].astype(BF16))
        us_ref[:, :, g * FOUR_GROUP:(g + 1) * FOUR_GROUP] = blk(u[:, FOUR_GROUP:].astype(BF16))


def _in_projection(x3, mod, mod_row0, per_batch_mod, norm_g, w_in, kg, gm, wcs, rope_tabs, emit_cache):
    nb, n, _ = x3.shape
    tn = min(PROJ_TILE, n)
    bt = PROJ_TILE // tn
    assert per_batch_mod is False or bt == 1
    rope = rope_tabs is not None
    tile = lambda width: pl.BlockSpec((bt, tn, width), lambda t, b: (b, t, 0))
    in_specs = [tile(D_MODEL),
                _resident((MOD_ROWS, MOD_HEAD_COLS)),
                _resident((1, D_MODEL)),
                _resident((D_MODEL, PROJ_WIDTH)),
                _resident((1, QK_WIDTH)),
                _resident((QK_WIDTH, QK_WIDTH)),
                _resident((N_FOUR_GROUPS, FOUR_GROUP, 2 * FOUR_GROUP))]
    args = [x3, mod, norm_g, w_in, kg, gm, wcs]
    if rope:
        in_specs += [pl.BlockSpec((tn, QK_WIDTH), lambda t, b: (t, 0))] * 2
        args += list(rope_tabs)
    out_shape = [jax.ShapeDtypeStruct((nb, n, QK_WIDTH), F32)]
    out_shape += [jax.ShapeDtypeStruct((nb, n, QK_WIDTH), BF16)] * 4
    out_specs = [tile(QK_WIDTH)] * 5
    if emit_cache:
        assert tn == n
        out_shape += [jax.ShapeDtypeStruct((nb, QK_WIDTH, n), F32),
                      jax.ShapeDtypeStruct((nb, n * N_HEADS, V_DIM), F32)]
        out_specs += [pl.BlockSpec((bt, QK_WIDTH, n), lambda t, b: (b, 0, 0)),
                      pl.BlockSpec((bt, n * N_HEADS, V_DIM), lambda t, b: (b, 0, 0))]
    return pl.pallas_call(
        functools.partial(_inproj_kernel, rope=rope, emit_cache=emit_cache, bt=bt, tn=tn,
                          mod_row0=mod_row0, per_batch_mod=per_batch_mod),
        grid=(n // tn, nb // bt),
        in_specs=in_specs,
        out_specs=out_specs,
        out_shape=out_shape,
        compiler_params=_params(2),
        name="in_projection_rope" if rope else "in_projection",
    )(*args)


ATTN_Q_TILE = 512
KEY_CHUNK = 1280
ONES_ROWS = 16
FOURIER_PIECE_ROWS = 128
PREFETCH_KEYS = 1024
ATTN_STEP_KEYS = 1024


def _attn_kernel(*refs, has_cache, n, past, bt, n_cast):
    it = iter(refs)
    lq1_ref, lk1_ref, lq2_ref, lk2_ref, sg_ref, qg_ref, q_ref, k_ref, v_ref = (next(it) for _ in range(9))
    if has_cache:
        ck_ref, cv_ref, rq_ref = next(it), next(it), next(it)
    pc_ref, ps_ref, perm_ref, uc_ref, us_ref = (next(it) for _ in range(5))
    cast_in = [next(it) for _ in range(n_cast)]
    if n_cast:
        cc_ref, c_ref, wm_ref, bm_ref = (next(it) for _ in range(4))
    o_ref, four_ref = next(it), next(it)
    for src, dst in zip(cast_in, [next(it) for _ in range(n_cast)]):
        dst[...] = src[...].astype(BF16)
    if n_cast:
        _modulation_block(cc_ref, c_ref, wm_ref[...].astype(BF16), bm_ref, next(it))
    kall_ref, vt_ref = next(it), next(it)
    fc_ref, fs_ref = next(it), next(it)
    nk = n + past
    half_n = n // 2
    tq = q_ref.shape[1]

    @pl.when(pl.program_id(1) == 0)
    def _per_batch_setup():
        ones = jnp.ones((ONES_ROWS, nk), BF16)
        for b in range(bt):
            if has_cache:
                kall_ref[b, 0:past, :] = ck_ref[b].T.astype(BF16)
            kall_ref[b, past:nk, :] = k_ref[b]
            for h in range(N_HEADS):
                if has_cache:
                    cvh = cv_ref[b, pl.ds(h, past, stride=N_HEADS), :]
                    vt_ref[b, h, 0:V_DIM, 0:past] = cvh.T.astype(BF16)
                vh = v_ref[b, :, h * V_DIM:(h + 1) * V_DIM].astype(F32)
                vt_ref[b, h, 0:V_DIM, past:nk] = vh.T.astype(BF16)
                vt_ref[b, h, V_DIM:V_DIM + ONES_ROWS, :] = ones
            rev_c = jnp.dot(perm_ref[...], uc_ref[b, half_n:n, :], preferred_element_type=F32)
            rev_s = jnp.dot(perm_ref[...], us_ref[b, half_n:n, :], preferred_element_type=F32)
            fc_ref[b] = (uc_ref[b, 0:half_n, :].astype(F32) + rev_c).astype(BF16)
            fs_ref[b] = (us_ref[b, 0:half_n, :].astype(F32) - rev_s).astype(BF16)

    l1 = jnp.exp(jnp.sum(lq1_ref[...] * lk1_ref[...], axis=-1, keepdims=True))
    l2 = jnp.exp(jnp.sum(lq2_ref[...] * lk2_ref[...], axis=-1, keepdims=True))
    lam = l1 - l2 + LAM_INIT

    head = lambda h: slice(h * V_DIM, (h + 1) * V_DIM)
    qscale = qg_ref[...] * (LOG2E / math.sqrt(HEAD_DIM))
    zeros = jnp.zeros((HEAD_DIM, tq), BF16)
    nf = ROPE_AXIS_DIM // 2

    def prepare_q(b, h):
        qt = q_ref[b, :, head(h)].T
        halves = []
        for mhalf in range(2):
            t = qt[mhalf * HEAD_DIM:(mhalf + 1) * HEAD_DIM, :]
            t = t * lax.rsqrt(jnp.mean(t * t, axis=0, keepdims=True) + EPS) * qscale
            if has_cache:
                parts = []
                for axis in range(2):
                    x1 = t[(2 * axis) * nf:(2 * axis + 1) * nf, :]
                    x2 = t[(2 * axis + 1) * nf:(2 * axis + 2) * nf, :]
                    cos, sin = rq_ref[2 * axis], rq_ref[2 * axis + 1]
                    parts += [x1 * cos - x2 * sin, x2 * cos + x1 * sin]
                t = jnp.concatenate(parts, axis=0)
            halves.append(t.astype(BF16))
        return halves

    kc = min(KEY_CHUNK, nk)
    chunks = [slice(c * kc, (c + 1) * kc) for c in range(nk // kc)]
    chains = [(b, h, mhalf) for b in range(bt) for h in range(N_HEADS) for mhalf in range(2)]
    qts = {}

    def scores(chain):
        b, h, mhalf = chain
        if (b, h) not in qts:
            qts[b, h] = prepare_q(b, h)
        qm = jnp.concatenate([qts[b, h][0], zeros] if mhalf == 0 else [zeros, qts[b, h][1]], axis=0)

        def one(rows):
            s = jnp.dot(kall_ref[b, rows, head(h)], qm, preferred_element_type=F32)
            return s, jnp.max(s, axis=0, keepdims=True)
        return one

    def fourier_piece(b, rows):
        def emit():
            o = jnp.dot(pc_ref[rows, :], fc_ref[b], preferred_element_type=F32)
            o = o + jnp.dot(ps_ref[rows, :], fs_ref[b], preferred_element_type=F32)
            k_par = lax.broadcasted_iota(jnp.int32, (rows.stop - rows.start, 1), 0) % 2
            mid = jnp.where(k_par == 0, 1.0, -1.0) * (1.0 / math.sqrt(n))
            o = o + mid * uc_ref[b, half_n:half_n + 1, :].astype(F32)
            four_ref[b, rows, :] = o.astype(BF16)
        return emit

    fr = min(FOURIER_PIECE_ROWS, tq)
    pieces = [fourier_piece(b, slice(r * fr, (r + 1) * fr)) for b in range(bt) for r in range(tq // fr)]
    piece_every = len(chains) // len(pieces)

    depth = max(1, min(len(chains), PREFETCH_KEYS // nk))
    queue = [[scores(chain)(rows) for rows in chunks] for chain in chains[:depth]]
    acc = [None, None]
    for i, (b, h, mhalf) in enumerate(chains):
        cur = queue.pop(0)
        mx = functools.reduce(jnp.maximum, [m for _, m in cur])
        nxt = scores(chains[i + depth]) if i + depth < len(chains) else None
        issued = []
        a = None
        for c, rows in enumerate(chunks):
            if nxt is not None:
                issued.append(nxt(rows))
            e = jnp.exp2(cur[c][0] - mx).astype(BF16)
            part = jnp.dot(vt_ref[b, h, :, rows], e, preferred_element_type=F32)
            a = part if a is None else a + part
        if nxt is not None:
            queue.append(issued)
        if i % piece_every == 0:
            pieces[i // piece_every]()
        acc[mhalf] = a
        if mhalf == 1:
            a0, a1 = acc
            r0 = 1.0 / a0[V_DIM:V_DIM + 1, :]
            r1 = lam / a1[V_DIM:V_DIM + 1, :]
            o = (a0[0:V_DIM, :] * r0 - a1[0:V_DIM, :] * r1).T
            ms = jnp.mean(o * o, axis=-1, keepdims=True)
            o = o * lax.rsqrt(ms + EPS) * sg_ref[...] * (1.0 - LAM_INIT)
            o_ref[b, :, head(h)] = o.astype(BF16)


def _attention(lam_vecs, subln_g, qg_col, q, k, v, uc, us, cache_k=None, cache_v=None, side_jobs=None):
    nb, n, _ = q.shape
    tq = min(ATTN_Q_TILE, n)
    has_cache = cache_k is not None
    past = cache_k.shape[2] if has_cache else 0
    nk = n + past
    bt = max(1, min(nb, ATTN_STEP_KEYS // nk))
    qtile = pl.BlockSpec((bt, tq, QK_WIDTH), lambda b, t: (b, t, 0))
    per_batch = pl.BlockSpec((bt, n, QK_WIDTH), lambda b, t: (b, 0, 0))
    in_specs = [_resident((1, HEAD_DIM))] * 4 + [_resident((1, V_DIM)), _resident((HEAD_DIM, 1)),
                                                   qtile, per_batch, per_batch]
    args = list(lam_vecs) + [subln_g, qg_col, q, k, v]
    if has_cache:
        in_specs += [pl.BlockSpec((bt, QK_WIDTH, past), lambda b, t: (b, 0, 0)),
                     pl.BlockSpec((bt, past * N_HEADS, V_DIM), lambda b, t: (b, 0, 0)),
                     pl.BlockSpec((4, ROPE_AXIS_DIM // 2, tq), lambda b, t: (0, 0, t))]
        args += [cache_k, cache_v, jnp.asarray(_rope_tables_feature_major(n))]
    pc, ps = _position_dft(n)
    dft_rows = pl.BlockSpec((tq, n // 2), lambda b, t: (t, 0))
    in_specs += [dft_rows, dft_rows, _resident((n // 2, n // 2)), per_batch, per_batch]
    args += [jnp.asarray(pc).astype(BF16), jnp.asarray(ps).astype(BF16),
             jnp.asarray(_fold_permutation(n)).astype(BF16), uc, us]
    nt = n // tq
    steps = (nb // bt) * nt
    out_specs = [qtile, qtile]
    out_shape = [jax.ShapeDtypeStruct((nb, n, ATTN_WIDTH), BF16), jax.ShapeDtypeStruct((nb, n, FOUR_WIDTH), BF16)]
    cast_weights = ()
    if side_jobs is not None:
        cast_weights, (c_ctx, c, w_mod, b_mod) = side_jobs
        slabs = [pl.BlockSpec((w.shape[0] // steps, w.shape[1]), lambda b, t: (b * nt + t, 0)) for w in cast_weights]
        tail = MOD_TAIL_COLS // steps
        first = MOD_HEAD_COLS // tail
        in_specs += slabs + [_resident(c_ctx.shape), _resident(c.shape),
                             pl.BlockSpec((D_MODEL, tail), lambda b, t: (0, first + b * nt + t)),
                             pl.BlockSpec((1, tail), lambda b, t: (0, first + b * nt + t))]
        args += list(cast_weights) + [c_ctx, c, w_mod, b_mod]
        out_specs += slabs + [pl.BlockSpec((MOD_ROWS, tail), lambda b, t: (0, b * nt + t))]
        out_shape += [jax.ShapeDtypeStruct(w.shape, BF16) for w in cast_weights]
        out_shape += [jax.ShapeDtypeStruct((MOD_ROWS, MOD_TAIL_COLS), F32)]
    return pl.pallas_call(
        functools.partial(_attn_kernel, has_cache=has_cache, n=n, past=past, bt=bt, n_cast=len(cast_weights)),
        grid=(nb // bt, nt),
        in_specs=in_specs,
        out_specs=out_specs,
        out_shape=out_shape,
        scratch_shapes=[pltpu.VMEM((bt, nk, QK_WIDTH), BF16),
                        pltpu.VMEM((bt, N_HEADS, V_DIM + ONES_ROWS, nk), BF16),
                        pltpu.VMEM((bt, n // 2, FOUR_WIDTH), BF16),
                        pltpu.VMEM((bt, n // 2, FOUR_WIDTH), BF16)],
        compiler_params=_params(2),
        name="diff_attention_cached" if has_cache else "diff_attention",
    )(*args)


OUT_TILE = 512
FF_CHUNK = 1024


def _out_mlp_kernel(x_ref, a_ref, f_ref, mod_ref, wo_ref, g2_ref, w1_ref, w2_ref, o_ref, *, mod_row0, per_batch_mod):
    m = _mod_row(mod_ref, mod_row0, per_batch_mod)
    g1 = m[:, 0:D_MODEL]
    sh2 = m[:, D_MODEL:2 * D_MODEL]
    sc2 = m[:, 2 * D_MODEL:3 * D_MODEL]
    g2 = m[:, 3 * D_MODEL:4 * D_MODEL]
    mix = jnp.dot(a_ref[0], wo_ref[0:ATTN_WIDTH, :], preferred_element_type=F32)
    mix = mix + jnp.dot(f_ref[0], wo_ref[ATTN_WIDTH:, :], preferred_element_type=F32)
    x1 = x_ref[0] + g1 * mix
    ms = jnp.mean(x1 * x1, axis=-1, keepdims=True)
    h = x1 * lax.rsqrt(ms + EPS) * g2_ref[...] * (1.0 + sc2) + sh2
    hb = h.astype(BF16)
    acc = jnp.zeros(x1.shape, F32)
    for c in range(D_FF // FF_CHUNK):
        t = jnp.dot(hb, w1_ref[:, c * FF_CHUNK:(c + 1) * FF_CHUNK], preferred_element_type=F32)
        t = jnp.square(jnp.maximum(t, 0.0)).astype(BF16)
        acc = acc + jnp.dot(t, w2_ref[c * FF_CHUNK:(c + 1) * FF_CHUNK, :], preferred_element_type=F32)
    o_ref[0] = x1 + g2 * acc


def _output_mlp(x3, attn, four, mod, mod_row0, per_batch_mod, w_out, norm2_g, w1, w2):
    nb, n, _ = x3.shape
    tm = OUT_TILE
    tile = lambda width: pl.BlockSpec((1, tm, width), lambda t, b: (b, t, 0))
    return pl.pallas_call(
        functools.partial(_out_mlp_kernel, mod_row0=mod_row0, per_batch_mod=per_batch_mod),
        grid=(n // tm, nb),
        in_specs=[tile(D_MODEL), tile(ATTN_WIDTH), tile(FOUR_WIDTH),
                  _resident((MOD_ROWS, MOD_TAIL_COLS)),
                  _resident((ATTN_WIDTH + FOUR_WIDTH, D_MODEL)),
                  _resident((1, D_MODEL)),
                  _resident((D_MODEL, D_FF)),
                  _resident((D_FF, D_MODEL))],
        out_specs=tile(D_MODEL),
        out_shape=jax.ShapeDtypeStruct((nb, n, D_MODEL), F32),
        compiler_params=_params(2),
        name="output_mlp",
    )(x3, attn, four, mod, w_out, norm2_g, w1, w2)


def kernel(x_prompt, x_sample, c, cache_k, cache_v, c_ctx, w_mod, b_mod, norm1_g, w_in, q_norm_g, k_norm_g,
           lambda_q1, lambda_k1, lambda_q2, lambda_k2, subln_g, w_four, w_out, norm2_g, w1, w2):
    batch, seq, _ = x_prompt.shape
    dec_batch, dec_seq, _ = x_sample.shape
    past = cache_k.shape[2]
    l = 0

    cc = c_ctx[None, :]
    bm = b_mod[l][None, :]
    mod, wcs, w_in_b = _modulation_head_and_fourier_weights(cc, c, w_mod[l], bm, w_four[l], w_in[l])

    n1 = norm1_g[l][None, :]
    n2 = norm2_g[l][None, :]
    qg_col = q_norm_g[l][:, None]
    kg = jnp.tile(k_norm_g[l], QK_WIDTH // HEAD_DIM)[None, :]
    sg = subln_g[l][None, :]
    lamv = tuple(t[l][None, :] for t in (lambda_q1, lambda_k1, lambda_q2, lambda_k2))
    gm = jnp.asarray(_head_mean_matrix()).astype(BF16)
    rope_tabs = tuple(jnp.asarray(t) for t in _rope_tables(dec_seq))

    qp, kp, vp, ucp, usp, kt32, v32 = _in_projection(x_prompt, mod, 0, False, n1, w_in_b, kg, gm, wcs, None, True)
    q, k, v, uc, us = _in_projection(x_sample, mod, 1, True, n1, w_in_b, kg, gm, wcs, rope_tabs, False)

    ck = cache_k[:, l].transpose(0, 2, 3, 4, 1).reshape(dec_batch, QK_WIDTH, past)
    cv = cache_v[:, l].reshape(dec_batch, past * N_HEADS, V_DIM)
    attn_s, four_s, w_out_b, w1_b, w2_b, mod_tail = _attention(
        lamv, sg, qg_col, q, k, v, uc, us, ck, cv, ((w_out[l], w1[l], w2[l]), (cc, c, w_mod[l], bm)))

    attn, four = _attention(lamv, sg, qg_col, qp, kp, vp, ucp, usp)
    flat = lambda t: t.reshape(1, batch * seq, t.shape[-1])
    yp = _output_mlp(flat(x_prompt), flat(attn), flat(four), mod_tail, 0, False, w_out_b, n2, w1_b, w2_b)
    y_prompt = yp.reshape(batch, seq, D_MODEL)
    new_cache_k = kt32.reshape(batch, N_HEADS, 2, HEAD_DIM, seq).transpose(0, 4, 1, 2, 3)[:, None]
    new_cache_v = v32.reshape(batch, 1, seq, N_HEADS, V_DIM)

    y_sample = _output_mlp(x_sample, attn_s, four_s, mod_tail, 1, True, w_out_b, n2, w1_b, w2_b)

    return (y_prompt, y_sample, new_cache_k, new_cache_v)
```

```python
import functools
import math

import jax
import jax.numpy as jnp
import numpy as np
from jax import lax
from jax.experimental import pallas as pl
from jax.experimental.pallas import tpu as pltpu

D_MODEL = 1024
N_HEADS = 4
HEAD_DIM = 64
V_DIM = 2 * HEAD_DIM
QK_WIDTH = N_HEADS * 2 * HEAD_DIM
ATTN_WIDTH = N_HEADS * V_DIM
N_FOUR_GROUPS = 4
FOUR_GROUP = 128
FOUR_WIDTH = N_FOUR_GROUPS * FOUR_GROUP
PROJ_WIDTH = 2 * QK_WIDTH + ATTN_WIDTH + FOUR_WIDTH
D_FF = 4 * D_MODEL
GRID_W = 64
ROPE_BASE = 10000.0
ROPE_AXIS_DIM = HEAD_DIM // 2
EPS = 1e-6
LAM_INIT = 0.8 - 0.6 * math.exp(-0.3 * 0)
LOG2E = 1.4426950408889634
MOD_ROWS = 8

F32 = jnp.float32
BF16 = jnp.bfloat16

VMEM_LIMIT_BYTES = 60 * 1024 * 1024


def _params(n_axes):
    return pltpu.CompilerParams(dimension_semantics=("arbitrary",) * n_axes,
                                vmem_limit_bytes=VMEM_LIMIT_BYTES)


def _resident(shape):
    nd = len(shape)
    return pl.BlockSpec(shape, lambda *_: (0,) * nd, pipeline_mode=pl.Buffered(1))


@functools.lru_cache(maxsize=None)
def _channel_dft():
    idx = np.arange(FOUR_GROUP)
    ang = 2.0 * np.pi * ((idx[:, None] * idx[None, :]) % FOUR_GROUP) / FOUR_GROUP
    s = 1.0 / np.sqrt(FOUR_GROUP)
    return np.stack([np.cos(ang) * s, np.sin(ang) * s]).astype(np.float32)


@functools.lru_cache(maxsize=None)
def _position_dft(n):
    idx = np.arange(n)
    ang = 2.0 * np.pi * ((idx[:, None] * idx[None, :]) % n) / n
    s = 1.0 / np.sqrt(n)
    return (np.cos(ang) * s).astype(np.float32), (-np.sin(ang) * s).astype(np.float32)


@functools.lru_cache(maxsize=None)
def _fold_permutation(n):
    h = n // 2
    p = np.zeros((h, h), np.float32)
    i = np.arange(1, h)
    p[i, h - i] = 1.0
    return p


@functools.lru_cache(maxsize=None)
def _rope_tables(n):
    rows = n // GRID_W
    row = np.repeat(np.arange(rows), GRID_W).astype(np.float64)
    col = np.tile(np.arange(GRID_W), rows).astype(np.float64)
    inv = ROPE_BASE ** (-np.arange(0, ROPE_AXIS_DIM, 2, dtype=np.float64) / ROPE_AXIS_DIM)
    d = np.arange(QK_WIDTH) % HEAD_DIM
    part = d // ROPE_AXIS_DIM
    i = d % ROPE_AXIS_DIM
    first = i < ROPE_AXIS_DIM // 2
    pos = np.where(part[None, :] == 0, row[:, None], col[:, None])
    ang = pos * inv[i % (ROPE_AXIS_DIM // 2)][None, :]
    cos = np.cos(ang)
    sin = np.where(first[None, :], -np.sin(ang), np.sin(ang))
    return cos.astype(np.float32), sin.astype(np.float32)


@functools.lru_cache(maxsize=None)
def _rope_tables_feature_major(n):
    rows = n // GRID_W
    row = np.repeat(np.arange(rows), GRID_W).astype(np.float64)
    col = np.tile(np.arange(GRID_W), rows).astype(np.float64)
    inv = ROPE_BASE ** (-np.arange(0, ROPE_AXIS_DIM, 2, dtype=np.float64) / ROPE_AXIS_DIM)
    ang_r = inv[:, None] * row[None, :]
    ang_c = inv[:, None] * col[None, :]
    return np.stack([np.cos(ang_r), np.sin(ang_r), np.cos(ang_c), np.sin(ang_c)]).astype(np.float32)


@functools.lru_cache(maxsize=None)
def _rope_partner_permutation():
    j = np.arange(QK_WIDTH)
    half = ROPE_AXIS_DIM // 2
    src = np.where((j % ROPE_AXIS_DIM) < half, j + half, j - half)
    p = np.zeros((QK_WIDTH, QK_WIDTH), np.float32)
    p[src, j] = 1.0
    return p


@functools.lru_cache(maxsize=None)
def _head_mean_matrix():
    g = np.kron(np.eye(QK_WIDTH // HEAD_DIM), np.ones((HEAD_DIM, HEAD_DIM))) / HEAD_DIM
    return g.astype(np.float32)


MOD_TILE = 512
MOD_HEAD_COLS = 2 * D_MODEL
MOD_TAIL_COLS = 4 * D_MODEL


def _modulation_block(cc_ref, c_ref, w, b_ref, o_ref):
    nd = c_ref.shape[0]

    def rows(cvec):
        s = cvec / (1.0 + jnp.exp(-cvec))
        return jnp.dot(s.astype(BF16), w, preferred_element_type=F32) + b_ref[...]

    o_ref[0:1, :] = rows(cc_ref[...])
    o_ref[1:1 + nd, :] = rows(c_ref[...])
    o_ref[1 + nd:, :] = jnp.zeros((MOD_ROWS - 1 - nd, o_ref.shape[1]), F32)


def _mod_kernel(cc_ref, c_ref, w_ref, b_ref, dft_ref, wf_ref, win_ref, o_ref, wcs_ref, winb_ref):
    winb_ref[...] = win_ref[...].astype(BF16)

    @pl.when(pl.program_id(0) == 0)
    def _fourier_weights():
        for g in range(N_FOUR_GROUPS):
            w = wf_ref[g]
            wc = jnp.dot(dft_ref[0], w, precision=lax.Precision.HIGHEST, preferred_element_type=F32)
            ws = jnp.dot(dft_ref[1], w, precision=lax.Precision.HIGHEST, preferred_element_type=F32)
            wcs_ref[g] = jnp.concatenate([wc, ws], axis=-1).astype(BF16)

    _modulation_block(cc_ref, c_ref, w_ref[...].astype(BF16), b_ref, o_ref)


def _modulation_head_and_fourier_weights(c_ctx, c, w_mod, b_mod, w_four, w_in):
    nd = c.shape[0]
    steps = MOD_HEAD_COLS // MOD_TILE
    dft = jnp.asarray(_channel_dft())
    slab = pl.BlockSpec((w_in.shape[0] // steps, w_in.shape[1]), lambda j: (j, 0))
    return pl.pallas_call(
        _mod_kernel,
        grid=(steps,),
        in_specs=[pl.BlockSpec((1, D_MODEL), lambda j: (0, 0)),
                  pl.BlockSpec((nd, D_MODEL), lambda j: (0, 0)),
                  pl.BlockSpec((D_MODEL, MOD_TILE), lambda j: (0, j)),
                  pl.BlockSpec((1, MOD_TILE), lambda j: (0, j)),
                  pl.BlockSpec((2, FOUR_GROUP, FOUR_GROUP), lambda j: (0, 0, 0)),
                  pl.BlockSpec((N_FOUR_GROUPS, FOUR_GROUP, FOUR_GROUP), lambda j: (0, 0, 0)),
                  slab],
        out_specs=[pl.BlockSpec((MOD_ROWS, MOD_TILE), lambda j: (0, j)),
                   pl.BlockSpec((N_FOUR_GROUPS, FOUR_GROUP, 2 * FOUR_GROUP), lambda j: (0, 0, 0)),
                   slab],
        out_shape=[jax.ShapeDtypeStruct((MOD_ROWS, MOD_HEAD_COLS), F32),
                   jax.ShapeDtypeStruct((N_FOUR_GROUPS, FOUR_GROUP, 2 * FOUR_GROUP), BF16),
                   jax.ShapeDtypeStruct(w_in.shape, BF16)],
        compiler_params=_params(1),
        name="modulation",
    )(c_ctx, c, w_mod, b_mod, dft, w_four, w_in)


PROJ_TILE = 1024


def _mod_row(mod_ref, mod_row0, per_batch_mod):
    if per_batch_mod:
        return mod_ref[pl.ds(mod_row0 + pl.program_id(1), 1), :]
    return mod_ref[mod_row0:mod_row0 + 1, :]


def _inproj_kernel(*refs, rope, emit_cache, bt, tn, mod_row0, per_batch_mod):
    it = iter(refs)
    x_ref, mod_ref, g1_ref, w_ref, kg_ref, gm_ref, wcs_ref = (next(it) for _ in range(7))
    if rope:
        cos_ref, sin_ref, perm_ref = next(it), next(it), next(it)
    q_ref, k_ref, v_ref, uc_ref, us_ref = (next(it) for _ in range(5))
    if emit_cache:
        k32_ref, v32_ref = next(it), next(it)

    x = x_ref[...].reshape(bt * tn, D_MODEL)
    m = _mod_row(mod_ref, mod_row0, per_batch_mod)
    sh1 = m[:, 0:D_MODEL]
    sc1 = m[:, D_MODEL:2 * D_MODEL]
    ms = jnp.mean(x * x, axis=-1, keepdims=True)
    y = x * lax.rsqrt(ms + EPS) * g1_ref[...]
    h = y * (1.0 + sc1) + sh1
    p = jnp.dot(h.astype(BF16), w_ref[...], preferred_element_type=F32)
    q = p[:, 0:QK_WIDTH]
    k = p[:, QK_WIDTH:2 * QK_WIDTH]
    v = p[:, 2 * QK_WIDTH:2 * QK_WIDTH + ATTN_WIDTH]

    msq = jnp.dot((k * k).astype(BF16), gm_ref[...], preferred_element_type=F32)
    k = k * lax.rsqrt(msq + EPS) * kg_ref[...]
    if emit_cache:
        for i in range(bt):
            rows = slice(i * tn, (i + 1) * tn)
            k32_ref[i] = k[rows, :].T
            for hd in range(N_HEADS):
                v32_ref[i, pl.ds(hd, tn, stride=N_HEADS), :] = v[rows, hd * V_DIM:(hd + 1) * V_DIM]

    if rope:
        sw = jnp.dot(k.astype(BF16), perm_ref[...], preferred_element_type=F32)
        k = k * cos_ref[...] + sw * sin_ref[...]

    blk = lambda t: t.reshape(bt, tn, t.shape[-1])
    q_ref[...] = blk(q)
    k_ref[...] = blk(k.astype(BF16))
    v_ref[...] = blk(v.astype(BF16))

    f0 = 2 * QK_WIDTH + ATTN_WIDTH
    for g in range(N_FOUR_GROUPS):
        fg = p[:, f0 + g * FOUR_GROUP:f0 + (g + 1) * FOUR_GROUP].astype(BF16)
        u = jnp.dot(fg, wcs_ref[g], preferred_element_type=F32)
        uc_ref[:, :, g * FOUR_GROUP:(g + 1) * FOUR_GROUP] = blk(u[:, 0:FOUR_GROUP].astype(BF16))
        us_ref[:, :, g * FOUR_GROUP:(g + 1) * FOUR_GROUP] = blk(u[:, FOUR_GROUP:].astype(BF16))


def _in_projection(x3, mod, mod_row0, per_batch_mod, norm_g, w_in, kg, gm, wcs, rope_tabs, emit_cache):
    nb, n, _ = x3.shape
    tn = min(PROJ_TILE, n)
    bt = PROJ_TILE // tn
    assert per_batch_mod is False or bt == 1
    rope = rope_tabs is not None
    tile = lambda width: pl.BlockSpec((bt, tn, width), lambda t, b: (b, t, 0))
    in_specs = [tile(D_MODEL),
                _resident((MOD_ROWS, MOD_HEAD_COLS)),
                _resident((1, D_MODEL)),
                _resident((D_MODEL, PROJ_WIDTH)),
                _resident((1, QK_WIDTH)),
                _resident((QK_WIDTH, QK_WIDTH)),
                _resident((N_FOUR_GROUPS, FOUR_GROUP, 2 * FOUR_GROUP))]
    args = [x3, mod, norm_g, w_in, kg, gm, wcs]
    if rope:
        in_specs += [pl.BlockSpec((tn, QK_WIDTH), lambda t, b: (t, 0))] * 2 + [_resident((QK_WIDTH, QK_WIDTH))]
        args += list(rope_tabs) + [jnp.asarray(_rope_partner_permutation()).astype(BF16)]
    out_shape = [jax.ShapeDtypeStruct((nb, n, QK_WIDTH), F32)]
    out_shape += [jax.ShapeDtypeStruct((nb, n, QK_WIDTH), BF16)] * 4
    out_specs = [tile(QK_WIDTH)] * 5
    if emit_cache:
        assert tn == n
        out_shape += [jax.ShapeDtypeStruct((nb, QK_WIDTH, n), F32),
                      jax.ShapeDtypeStruct((nb, n * N_HEADS, V_DIM), F32)]
        out_specs += [pl.BlockSpec((bt, QK_WIDTH, n), lambda t, b: (b, 0, 0)),
                      pl.BlockSpec((bt, n * N_HEADS, V_DIM), lambda t, b: (b, 0, 0))]
    return pl.pallas_call(
        functools.partial(_inproj_kernel, rope=rope, emit_cache=emit_cache, bt=bt, tn=tn,
                          mod_row0=mod_row0, per_batch_mod=per_batch_mod),
        grid=(n // tn, nb // bt),
        in_specs=in_specs,
        out_specs=out_specs,
        out_shape=out_shape,
        compiler_params=_params(2),
        name="in_projection_rope" if rope else "in_projection",
    )(*args)


ATTN_Q_TILE = 512
KEY_CHUNK = 1280
ONES_ROWS = 16
FOURIER_PIECE_ROWS = 128
PREFETCH_KEYS = 1024
ATTN_STEP_KEYS = 1024


def _attn_kernel(*refs, has_cache, n, past, bt, n_cast):
    it = iter(refs)
    lq1_ref, lk1_ref, lq2_ref, lk2_ref, sg_ref, qg_ref, q_ref, k_ref, v_ref = (next(it) for _ in range(9))
    if has_cache:
        ck_ref, cv_ref, rq_ref = next(it), next(it), next(it)
    pc_ref, ps_ref, perm_ref, uc_ref, us_ref = (next(it) for _ in range(5))
    cast_in = [next(it) for _ in range(n_cast)]
    if n_cast:
        cc_ref, c_ref, wm_ref, bm_ref = (next(it) for _ in range(4))
    o_ref, four_ref = next(it), next(it)
    for src, dst in zip(cast_in, [next(it) for _ in range(n_cast)]):
        dst[...] = src[...].astype(BF16)
    if n_cast:
        _modulation_block(cc_ref, c_ref, wm_ref[...].astype(BF16), bm_ref, next(it))
    kall_ref, vt_ref = next(it), next(it)
    fc_ref, fs_ref = next(it), next(it)
    nk = n + past
    half_n = n // 2
    tq = q_ref.shape[1]

    @pl.when(pl.program_id(1) == 0)
    def _per_batch_setup():
        ones = jnp.ones((ONES_ROWS, nk), BF16)
        for b in range(bt):
            if has_cache:
                kall_ref[b, 0:past, :] = ck_ref[b].T.astype(BF16)
            kall_ref[b, past:nk, :] = k_ref[b]
            for h in range(N_HEADS):
                if has_cache:
                    cvh = cv_ref[b, pl.ds(h, past, stride=N_HEADS), :]
                    vt_ref[b, h, 0:V_DIM, 0:past] = cvh.T.astype(BF16)
                vh = v_ref[b, :, h * V_DIM:(h + 1) * V_DIM].astype(F32)
                vt_ref[b, h, 0:V_DIM, past:nk] = vh.T.astype(BF16)
                vt_ref[b, h, V_DIM:V_DIM + ONES_ROWS, :] = ones
            rev_c = jnp.dot(perm_ref[...], uc_ref[b, half_n:n, :], preferred_element_type=F32)
            rev_s = jnp.dot(perm_ref[...], us_ref[b, half_n:n, :], preferred_element_type=F32)
            fc_ref[b] = (uc_ref[b, 0:half_n, :].astype(F32) + rev_c).astype(BF16)
            fs_ref[b] = (us_ref[b, 0:half_n, :].astype(F32) - rev_s).astype(BF16)

    l1 = jnp.exp(jnp.sum(lq1_ref[...] * lk1_ref[...], axis=-1, keepdims=True))
    l2 = jnp.exp(jnp.sum(lq2_ref[...] * lk2_ref[...], axis=-1, keepdims=True))
    lam = l1 - l2 + LAM_INIT

    head = lambda h: slice(h * V_DIM, (h + 1) * V_DIM)
    qscale = qg_ref[...] * (LOG2E / math.sqrt(HEAD_DIM))
    zeros = jnp.zeros((HEAD_DIM, tq), BF16)
    nf = ROPE_AXIS_DIM // 2

    def prepare_q(b, h):
        qt = q_ref[b, :, head(h)].T
        halves = []
        for mhalf in range(2):
            t = qt[mhalf * HEAD_DIM:(mhalf + 1) * HEAD_DIM, :]
            t = t * lax.rsqrt(jnp.mean(t * t, axis=0, keepdims=True) + EPS) * qscale
            if has_cache:
                parts = []
                for axis in range(2):
                    x1 = t[(2 * axis) * nf:(2 * axis + 1) * nf, :]
                    x2 = t[(2 * axis + 1) * nf:(2 * axis + 2) * nf, :]
                    cos, sin = rq_ref[2 * axis], rq_ref[2 * axis + 1]
                    parts += [x1 * cos - x2 * sin, x2 * cos + x1 * sin]
                t = jnp.concatenate(parts, axis=0)
            halves.append(t.astype(BF16))
        return halves

    kc = min(KEY_CHUNK, nk)
    chunks = [slice(c * kc, (c + 1) * kc) for c in range(nk // kc)]
    chains = [(b, h, mhalf) for b in range(bt) for h in range(N_HEADS) for mhalf in range(2)]
    qts = {}

    def scores(chain):
        b, h, mhalf = chain
        if (b, h) not in qts:
            qts[b, h] = prepare_q(b, h)
        qm = jnp.concatenate([qts[b, h][0], zeros] if mhalf == 0 else [zeros, qts[b, h][1]], axis=0)

        def one(rows):
            s = jnp.dot(kall_ref[b, rows, head(h)], qm, preferred_element_type=F32)
            return s, jnp.max(s, axis=0, keepdims=True)
        return one

    def fourier_piece(b, rows):
        def emit():
            o = jnp.dot(pc_ref[rows, :], fc_ref[b], preferred_element_type=F32)
            o = o + jnp.dot(ps_ref[rows, :], fs_ref[b], preferred_element_type=F32)
            k_par = lax.broadcasted_iota(jnp.int32, (rows.stop - rows.start, 1), 0) % 2
            mid = jnp.where(k_par == 0, 1.0, -1.0) * (1.0 / math.sqrt(n))
            o = o + mid * uc_ref[b, half_n:half_n + 1, :].astype(F32)
            four_ref[b, rows, :] = o.astype(BF16)
        return emit

    fr = min(FOURIER_PIECE_ROWS, tq)
    pieces = [fourier_piece(b, slice(r * fr, (r + 1) * fr)) for b in range(bt) for r in range(tq // fr)]
    piece_every = len(chains) // len(pieces)

    depth = max(1, min(len(chains), PREFETCH_KEYS // nk))
    queue = [[scores(chain)(rows) for rows in chunks] for chain in chains[:depth]]
    acc = [None, None]
    for i, (b, h, mhalf) in enumerate(chains):
        cur = queue.pop(0)
        mx = functools.reduce(jnp.maximum, [m for _, m in cur])
        nxt = scores(chains[i + depth]) if i + depth < len(chains) else None
        issued = []
        a = None
        for c, rows in enumerate(chunks):
            if nxt is not None:
                issued.append(nxt(rows))
            e = jnp.exp2(cur[c][0] - mx).astype(BF16)
            part = jnp.dot(vt_ref[b, h, :, rows], e, preferred_element_type=F32)
            a = part if a is None else a + part
        if nxt is not None:
            queue.append(issued)
        if i % piece_every == 0:
            pieces[i // piece_every]()
        acc[mhalf] = a
        if mhalf == 1:
            a0, a1 = acc
            r0 = 1.0 / a0[V_DIM:V_DIM + 1, :]
            r1 = lam / a1[V_DIM:V_DIM + 1, :]
            o = (a0[0:V_DIM, :] * r0 - a1[0:V_DIM, :] * r1).T
            ms = jnp.mean(o * o, axis=-1, keepdims=True)
            o = o * lax.rsqrt(ms + EPS) * sg_ref[...] * (1.0 - LAM_INIT)
            o_ref[b, :, head(h)] = o.astype(BF16)


def _attention(lam_vecs, subln_g, qg_col, q, k, v, uc, us, cache_k=None, cache_v=None, side_jobs=None):
    nb, n, _ = q.shape
    tq = min(ATTN_Q_TILE, n)
    has_cache = cache_k is not None
    past = cache_k.shape[2] if has_cache else 0
    nk = n + past
    bt = max(1, min(nb, ATTN_STEP_KEYS // nk))
    qtile = pl.BlockSpec((bt, tq, QK_WIDTH), lambda b, t: (b, t, 0))
    per_batch = pl.BlockSpec((bt, n, QK_WIDTH), lambda b, t: (b, 0, 0))
    in_specs = [_resident((1, HEAD_DIM))] * 4 + [_resident((1, V_DIM)), _resident((HEAD_DIM, 1)),
                                                   qtile, per_batch, per_batch]
    args = list(lam_vecs) + [subln_g, qg_col, q, k, v]
    if has_cache:
        in_specs += [pl.BlockSpec((bt, QK_WIDTH, past), lambda b, t: (b, 0, 0)),
                     pl.BlockSpec((bt, past * N_HEADS, V_DIM), lambda b, t: (b, 0, 0)),
                     pl.BlockSpec((4, ROPE_AXIS_DIM // 2, tq), lambda b, t: (0, 0, t))]
        args += [cache_k, cache_v, jnp.asarray(_rope_tables_feature_major(n))]
    pc, ps = _position_dft(n)
    dft_rows = pl.BlockSpec((tq, n // 2), lambda b, t: (t, 0))
    in_specs += [dft_rows, dft_rows, _resident((n // 2, n // 2)), per_batch, per_batch]
    args += [jnp.asarray(pc).astype(BF16), jnp.asarray(ps).astype(BF16),
             jnp.asarray(_fold_permutation(n)).astype(BF16), uc, us]
    nt = n // tq
    steps = (nb // bt) * nt
    out_specs = [qtile, qtile]
    out_shape = [jax.ShapeDtypeStruct((nb, n, ATTN_WIDTH), BF16), jax.ShapeDtypeStruct((nb, n, FOUR_WIDTH), BF16)]
    cast_weights = ()
    if side_jobs is not None:
        cast_weights, (c_ctx, c, w_mod, b_mod) = side_jobs
        slabs = [pl.BlockSpec((w.shape[0] // steps, w.shape[1]), lambda b, t: (b * nt + t, 0)) for w in cast_weights]
        tail = MOD_TAIL_COLS // steps
        first = MOD_HEAD_COLS // tail
        in_specs += slabs + [_resident(c_ctx.shape), _resident(c.shape),
                             pl.BlockSpec((D_MODEL, tail), lambda b, t: (0, first + b * nt + t)),
                             pl.BlockSpec((1, tail), lambda b, t: (0, first + b * nt + t))]
        args += list(cast_weights) + [c_ctx, c, w_mod, b_mod]
        out_specs += slabs + [pl.BlockSpec((MOD_ROWS, tail), lambda b, t: (0, b * nt + t))]
        out_shape += [jax.ShapeDtypeStruct(w.shape, BF16) for w in cast_weights]
        out_shape += [jax.ShapeDtypeStruct((MOD_ROWS, MOD_TAIL_COLS), F32)]
    return pl.pallas_call(
        functools.partial(_attn_kernel, has_cache=has_cache, n=n, past=past, bt=bt, n_cast=len(cast_weights)),
        grid=(nb // bt, nt),
        in_specs=in_specs,
        out_specs=out_specs,
        out_shape=out_shape,
        scratch_shapes=[pltpu.VMEM((bt, nk, QK_WIDTH), BF16),
                        pltpu.VMEM((bt, N_HEADS, V_DIM + ONES_ROWS, nk), BF16),
                        pltpu.VMEM((bt, n // 2, FOUR_WIDTH), BF16),
                        pltpu.VMEM((bt, n // 2, FOUR_WIDTH), BF16)],
        compiler_params=_params(2),
        name="diff_attention_cached" if has_cache else "diff_attention",
    )(*args)


OUT_TILE = 512
FF_CHUNK = 1024


def _out_mlp_kernel(x_ref, a_ref, f_ref, mod_ref, wo_ref, g2_ref, w1_ref, w2_ref, o_ref, *, mod_row0, per_batch_mod):
    m = _mod_row(mod_ref, mod_row0, per_batch_mod)
    g1 = m[:, 0:D_MODEL]
    sh2 = m[:, D_MODEL:2 * D_MODEL]
    sc2 = m[:, 2 * D_MODEL:3 * D_MODEL]
    g2 = m[:, 3 * D_MODEL:4 * D_MODEL]
    mix = jnp.dot(a_ref[0], wo_ref[0:ATTN_WIDTH, :], preferred_element_type=F32)
    mix = mix + jnp.dot(f_ref[0], wo_ref[ATTN_WIDTH:, :], preferred_element_type=F32)
    x1 = x_ref[0] + g1 * mix
    ms = jnp.mean(x1 * x1, axis=-1, keepdims=True)
    h = x1 * lax.rsqrt(ms + EPS) * g2_ref[...] * (1.0 + sc2) + sh2
    hb = h.astype(BF16)
    acc = jnp.zeros(x1.shape, F32)
    for c in range(D_FF // FF_CHUNK):
        t = jnp.dot(hb, w1_ref[:, c * FF_CHUNK:(c + 1) * FF_CHUNK], preferred_element_type=F32)
        t = jnp.square(jnp.maximum(t, 0.0)).astype(BF16)
        acc = acc + jnp.dot(t, w2_ref[c * FF_CHUNK:(c + 1) * FF_CHUNK, :], preferred_element_type=F32)
    o_ref[0] = x1 + g2 * acc


def _output_mlp(x3, attn, four, mod, mod_row0, per_batch_mod, w_out, norm2_g, w1, w2):
    nb, n, _ = x3.shape
    tm = OUT_TILE
    tile = lambda width: pl.BlockSpec((1, tm, width), lambda t, b: (b, t, 0))
    return pl.pallas_call(
        functools.partial(_out_mlp_kernel, mod_row0=mod_row0, per_batch_mod=per_batch_mod),
        grid=(n // tm, nb),
        in_specs=[tile(D_MODEL), tile(ATTN_WIDTH), tile(FOUR_WIDTH),
                  _resident((MOD_ROWS, MOD_TAIL_COLS)),
                  _resident((ATTN_WIDTH + FOUR_WIDTH, D_MODEL)),
                  _resident((1, D_MODEL)),
                  _resident((D_MODEL, D_FF)),
                  _resident((D_FF, D_MODEL))],
        out_specs=tile(D_MODEL),
        out_shape=jax.ShapeDtypeStruct((nb, n, D_MODEL), F32),
        compiler_params=_params(2),
        name="output_mlp",
    )(x3, attn, four, mod, w_out, norm2_g, w1, w2)


def kernel(x_prompt, x_sample, c, cache_k, cache_v, c_ctx, w_mod, b_mod, norm1_g, w_in, q_norm_g, k_norm_g,
           lambda_q1, lambda_k1, lambda_q2, lambda_k2, subln_g, w_four, w_out, norm2_g, w1, w2):
    batch, seq, _ = x_prompt.shape
    dec_batch, dec_seq, _ = x_sample.shape
    past = cache_k.shape[2]
    l = 0

    cc = c_ctx[None, :]
    bm = b_mod[l][None, :]
    mod, wcs, w_in_b = _modulation_head_and_fourier_weights(cc, c, w_mod[l], bm, w_four[l], w_in[l])

    n1 = norm1_g[l][None, :]
    n2 = norm2_g[l][None, :]
    qg_col = q_norm_g[l][:, None]
    kg = jnp.tile(k_norm_g[l], QK_WIDTH // HEAD_DIM)[None, :]
    sg = subln_g[l][None, :]
    lamv = tuple(t[l][None, :] for t in (lambda_q1, lambda_k1, lambda_q2, lambda_k2))
    gm = jnp.asarray(_head_mean_matrix()).astype(BF16)
    rope_tabs = tuple(jnp.asarray(t) for t in _rope_tables(dec_seq))

    qp, kp, vp, ucp, usp, kt32, v32 = _in_projection(x_prompt, mod, 0, False, n1, w_in_b, kg, gm, wcs, None, True)
    q, k, v, uc, us = _in_projection(x_sample, mod, 1, True, n1, w_in_b, kg, gm, wcs, rope_tabs, False)

    ck = cache_k[:, l].transpose(0, 2, 3, 4, 1).reshape(dec_batch, QK_WIDTH, past)
    cv = cache_v[:, l].reshape(dec_batch, past * N_HEADS, V_DIM)
    attn_s, four_s, w_out_b, w1_b, w2_b, mod_tail = _attention(
        lamv, sg, qg_col, q, k, v, uc, us, ck, cv, ((w_out[l], w1[l], w2[l]), (cc, c, w_mod[l], bm)))

    attn, four = _attention(lamv, sg, qg_col, qp, kp, vp, ucp, usp)
    flat = lambda t: t.reshape(1, batch * seq, t.shape[-1])
    yp = _output_mlp(flat(x_prompt), flat(attn), flat(four), mod_tail, 0, False, w_out_b, n2, w1_b, w2_b)
    y_prompt = yp.reshape(batch, seq, D_MODEL)
    new_cache_k = kt32.reshape(batch, N_HEADS, 2, HEAD_DIM, seq).transpose(0, 4, 1, 2, 3)[:, None]
    new_cache_v = v32.reshape(batch, 1, seq, N_HEADS, V_DIM)

    y_sample = _output_mlp(x_sample, attn_s, four_s, mod_tail, 1, True, w_out_b, n2, w1_b, w2_b)

    return (y_prompt, y_sample, new_cache_k, new_cache_v)
```

```python
import functools
import math

import jax
import jax.numpy as jnp
import numpy as np
from jax import lax
from jax.experimental import pallas as pl
from jax.experimental.pallas import tpu as pltpu

D_MODEL = 1024
N_HEADS = 4
HEAD_DIM = 64
V_DIM = 2 * HEAD_DIM
QK_WIDTH = N_HEADS * 2 * HEAD_DIM
ATTN_WIDTH = N_HEADS * V_DIM
N_FOUR_GROUPS = 4
FOUR_GROUP = 128
FOUR_WIDTH = N_FOUR_GROUPS * FOUR_GROUP
PROJ_WIDTH = 2 * QK_WIDTH + ATTN_WIDTH + FOUR_WIDTH
D_FF = 4 * D_MODEL
GRID_W = 64
ROPE_BASE = 10000.0
ROPE_AXIS_DIM = HEAD_DIM // 2
EPS = 1e-6
LAM_INIT = 0.8 - 0.6 * math.exp(-0.3 * 0)
LOG2E = 1.4426950408889634
MOD_ROWS = 8

F32 = jnp.float32
BF16 = jnp.bfloat16

VMEM_LIMIT_BYTES = 60 * 1024 * 1024


def _params(n_axes):
    return pltpu.CompilerParams(dimension_semantics=("arbitrary",) * n_axes,
                                vmem_limit_bytes=VMEM_LIMIT_BYTES)


def _resident(shape):
    nd = len(shape)
    return pl.BlockSpec(shape, lambda *_: (0,) * nd, pipeline_mode=pl.Buffered(1))


@functools.lru_cache(maxsize=None)
def _channel_dft():
    idx = np.arange(FOUR_GROUP)
    ang = 2.0 * np.pi * ((idx[:, None] * idx[None, :]) % FOUR_GROUP) / FOUR_GROUP
    s = 1.0 / np.sqrt(FOUR_GROUP)
    return np.stack([np.cos(ang) * s, np.sin(ang) * s]).astype(np.float32)


@functools.lru_cache(maxsize=None)
def _position_dft(n):
    idx = np.arange(n)
    ang = 2.0 * np.pi * ((idx[:, None] * idx[None, :]) % n) / n
    s = 1.0 / np.sqrt(n)
    return (np.cos(ang) * s).astype(np.float32), (-np.sin(ang) * s).astype(np.float32)


@functools.lru_cache(maxsize=None)
def _mirror_block(rows):
    p = np.zeros((rows, rows), np.float32)
    r = np.arange(1, rows)
    p[r, rows - r] = 1.0
    return p


@functools.lru_cache(maxsize=None)
def _rope_tables(n):
    rows = n // GRID_W
    row = np.repeat(np.arange(rows), GRID_W).astype(np.float64)
    col = np.tile(np.arange(GRID_W), rows).astype(np.float64)
    inv = ROPE_BASE ** (-np.arange(0, ROPE_AXIS_DIM, 2, dtype=np.float64) / ROPE_AXIS_DIM)
    d = np.arange(QK_WIDTH) % HEAD_DIM
    part = d // ROPE_AXIS_DIM
    i = d % ROPE_AXIS_DIM
    first = i < ROPE_AXIS_DIM // 2
    pos = np.where(part[None, :] == 0, row[:, None], col[:, None])
    ang = pos * inv[i % (ROPE_AXIS_DIM // 2)][None, :]
    cos = np.cos(ang)
    sin = np.where(first[None, :], -np.sin(ang), np.sin(ang))
    return cos.astype(np.float32), sin.astype(np.float32)


@functools.lru_cache(maxsize=None)
def _rope_tables_feature_major(n):
    rows = n // GRID_W
    row = np.repeat(np.arange(rows), GRID_W).astype(np.float64)
    col = np.tile(np.arange(GRID_W), rows).astype(np.float64)
    inv = ROPE_BASE ** (-np.arange(0, ROPE_AXIS_DIM, 2, dtype=np.float64) / ROPE_AXIS_DIM)
    ang_r = inv[:, None] * row[None, :]
    ang_c = inv[:, None] * col[None, :]
    return np.stack([np.cos(ang_r), np.sin(ang_r), np.cos(ang_c), np.sin(ang_c)]).astype(np.float32)


@functools.lru_cache(maxsize=None)
def _head_mean_matrix():
    g = np.kron(np.eye(QK_WIDTH // HEAD_DIM), np.ones((HEAD_DIM, HEAD_DIM))) / HEAD_DIM
    return g.astype(np.float32)


MOD_TILE = 512
MOD_HEAD_COLS = 2 * D_MODEL
MOD_TAIL_COLS = 4 * D_MODEL


def _modulation_block(cc_ref, c_ref, w, b_ref, o_ref):
    nd = c_ref.shape[0]

    def rows(cvec):
        s = cvec / (1.0 + jnp.exp(-cvec))
        return jnp.dot(s.astype(BF16), w, preferred_element_type=F32) + b_ref[...]

    o_ref[0:1, :] = rows(cc_ref[...])
    o_ref[1:1 + nd, :] = rows(c_ref[...])
    o_ref[1 + nd:, :] = jnp.zeros((MOD_ROWS - 1 - nd, o_ref.shape[1]), F32)


def _mod_kernel(cc_ref, c_ref, w_ref, b_ref, dft_ref, wf_ref, win_ref, o_ref, wcs_ref, winb_ref):
    winb_ref[...] = win_ref[...].astype(BF16)

    @pl.when(pl.program_id(0) == 0)
    def _fourier_weights():
        for g in range(N_FOUR_GROUPS):
            w = wf_ref[g]
            wc = jnp.dot(dft_ref[0], w, precision=lax.Precision.HIGHEST, preferred_element_type=F32)
            ws = jnp.dot(dft_ref[1], w, precision=lax.Precision.HIGHEST, preferred_element_type=F32)
            wcs_ref[g] = jnp.concatenate([wc, ws], axis=-1).astype(BF16)

    _modulation_block(cc_ref, c_ref, w_ref[...].astype(BF16), b_ref, o_ref)


def _modulation_head_and_fourier_weights(c_ctx, c, w_mod, b_mod, w_four, w_in):
    nd = c.shape[0]
    steps = MOD_HEAD_COLS // MOD_TILE
    dft = jnp.asarray(_channel_dft())
    slab = pl.BlockSpec((w_in.shape[0] // steps, w_in.shape[1]), lambda j: (j, 0))
    return pl.pallas_call(
        _mod_kernel,
        grid=(steps,),
        in_specs=[pl.BlockSpec((1, D_MODEL), lambda j: (0, 0)),
                  pl.BlockSpec((nd, D_MODEL), lambda j: (0, 0)),
                  pl.BlockSpec((D_MODEL, MOD_TILE), lambda j: (0, j)),
                  pl.BlockSpec((1, MOD_TILE), lambda j: (0, j)),
                  pl.BlockSpec((2, FOUR_GROUP, FOUR_GROUP), lambda j: (0, 0, 0)),
                  pl.BlockSpec((N_FOUR_GROUPS, FOUR_GROUP, FOUR_GROUP), lambda j: (0, 0, 0)),
                  slab],
        out_specs=[pl.BlockSpec((MOD_ROWS, MOD_TILE), lambda j: (0, j)),
                   pl.BlockSpec((N_FOUR_GROUPS, FOUR_GROUP, 2 * FOUR_GROUP), lambda j: (0, 0, 0)),
                   slab],
        out_shape=[jax.ShapeDtypeStruct((MOD_ROWS, MOD_HEAD_COLS), F32),
                   jax.ShapeDtypeStruct((N_FOUR_GROUPS, FOUR_GROUP, 2 * FOUR_GROUP), BF16),
                   jax.ShapeDtypeStruct(w_in.shape, BF16)],
        compiler_params=_params(1),
        name="modulation",
    )(c_ctx, c, w_mod, b_mod, dft, w_four, w_in)


PROJ_TILE = 1024


def _mod_row(mod_ref, mod_row0, per_batch_mod):
    if per_batch_mod:
        return mod_ref[pl.ds(mod_row0 + pl.program_id(1), 1), :]
    return mod_ref[mod_row0:mod_row0 + 1, :]


def _inproj_kernel(*refs, rope, emit_cache, bt, tn, mod_row0, per_batch_mod):
    it = iter(refs)
    x_ref, mod_ref, g1_ref, w_ref, kg_ref, gm_ref, wcs_ref = (next(it) for _ in range(7))
    if rope:
        cos_ref, sin_ref = next(it), next(it)
    q_ref, k_ref, v_ref, uc_ref, us_ref = (next(it) for _ in range(5))
    if emit_cache:
        k32_ref, v32_ref = next(it), next(it)

    x = x_ref[...].reshape(bt * tn, D_MODEL)
    m = _mod_row(mod_ref, mod_row0, per_batch_mod)
    sh1 = m[:, 0:D_MODEL]
    sc1 = m[:, D_MODEL:2 * D_MODEL]
    ms = jnp.mean(x * x, axis=-1, keepdims=True)
    y = x * lax.rsqrt(ms + EPS) * g1_ref[...]
    h = y * (1.0 + sc1) + sh1
    p = jnp.dot(h.astype(BF16), w_ref[...], preferred_element_type=F32)
    q = p[:, 0:QK_WIDTH]
    k = p[:, QK_WIDTH:2 * QK_WIDTH]
    v = p[:, 2 * QK_WIDTH:2 * QK_WIDTH + ATTN_WIDTH]

    msq = jnp.dot((k * k).astype(BF16), gm_ref[...], preferred_element_type=F32)
    k = k * lax.rsqrt(msq + EPS) * kg_ref[...]
    if emit_cache:
        for i in range(bt):
            rows = slice(i * tn, (i + 1) * tn)
            k32_ref[i] = k[rows, :].T
            for hd in range(N_HEADS):
                v32_ref[i, pl.ds(hd, tn, stride=N_HEADS), :] = v[rows, hd * V_DIM:(hd + 1) * V_DIM]

    if rope:
        lane = lax.broadcasted_iota(jnp.int32, (k.shape[0], V_DIM), 1)
        first = (lane % ROPE_AXIS_DIM) < (ROPE_AXIS_DIM // 2)
        half = ROPE_AXIS_DIM // 2
        cols = [k[:, hd * V_DIM:(hd + 1) * V_DIM] for hd in range(N_HEADS)]
        sw = jnp.concatenate([jnp.where(first, pltpu.roll(t, V_DIM - half, 1), pltpu.roll(t, half, 1))
                              for t in cols], axis=-1)
        k = k * cos_ref[...] + sw * sin_ref[...]

    blk = lambda t: t.reshape(bt, tn, t.shape[-1])
    q_ref[...] = blk(q)
    k_ref[...] = blk(k.astype(BF16))
    v_ref[...] = blk(v.astype(BF16))

    f0 = 2 * QK_WIDTH + ATTN_WIDTH
    for g in range(N_FOUR_GROUPS):
        fg = p[:, f0 + g * FOUR_GROUP:f0 + (g + 1) * FOUR_GROUP].astype(BF16)
        u = jnp.dot(fg, wcs_ref[g], preferred_element_type=F32)
        uc_ref[:, :, g * FOUR_GROUP:(g + 1) * FOUR_GROUP] = blk(u[:, 0:FOUR_GROUP].astype(BF16))
        us_ref[:, :, g * FOUR_GROUP:(g + 1) * FOUR_GROUP] = blk(u[:, FOUR_GROUP:].astype(BF16))


def _in_projection(x3, mod, mod_row0, per_batch_mod, norm_g, w_in, kg, gm, wcs, rope_tabs, emit_cache):
    nb, n, _ = x3.shape
    tn = min(PROJ_TILE, n)
    bt = PROJ_TILE // tn
    assert per_batch_mod is False or bt == 1
    rope = rope_tabs is not None
    tile = lambda width: pl.BlockSpec((bt, tn, width), lambda t, b: (b, t, 0))
    in_specs = [tile(D_MODEL),
                _resident((MOD_ROWS, MOD_HEAD_COLS)),
                _resident((1, D_MODEL)),
                _resident((D_MODEL, PROJ_WIDTH)),
                _resident((1, QK_WIDTH)),
                _resident((QK_WIDTH, QK_WIDTH)),
                _resident((N_FOUR_GROUPS, FOUR_GROUP, 2 * FOUR_GROUP))]
    args = [x3, mod, norm_g, w_in, kg, gm, wcs]
    if rope:
        in_specs += [pl.BlockSpec((tn, QK_WIDTH), lambda t, b: (t, 0))] * 2
        args += list(rope_tabs)
    out_shape = [jax.ShapeDtypeStruct((nb, n, QK_WIDTH), F32)]
    out_shape += [jax.ShapeDtypeStruct((nb, n, QK_WIDTH), BF16)] * 4
    out_specs = [tile(QK_WIDTH)] * 5
    if emit_cache:
        assert tn == n
        out_shape += [jax.ShapeDtypeStruct((nb, QK_WIDTH, n), F32),
                      jax.ShapeDtypeStruct((nb, n * N_HEADS, V_DIM), F32)]
        out_specs += [pl.BlockSpec((bt, QK_WIDTH, n), lambda t, b: (b, 0, 0)),
                      pl.BlockSpec((bt, n * N_HEADS, V_DIM), lambda t, b: (b, 0, 0))]
    return pl.pallas_call(
        functools.partial(_inproj_kernel, rope=rope, emit_cache=emit_cache, bt=bt, tn=tn,
                          mod_row0=mod_row0, per_batch_mod=per_batch_mod),
        grid=(n // tn, nb // bt),
        in_specs=in_specs,
        out_specs=out_specs,
        out_shape=out_shape,
        compiler_params=_params(2),
        name="in_projection_rope" if rope else "in_projection",
    )(*args)


ATTN_Q_TILE = 512
KEY_CHUNK = 1280
ONES_ROWS = 16
FOURIER_PIECE_ROWS = 128
MIRROR_BLOCK_ROWS = 256
PREFETCH_KEYS = 1024
ATTN_STEP_KEYS = 1024


def _attn_kernel(*refs, has_cache, n, past, bt, n_cast):
    it = iter(refs)
    lq1_ref, lk1_ref, lq2_ref, lk2_ref, sg_ref, qg_ref, q_ref, k_ref, v_ref = (next(it) for _ in range(9))
    if has_cache:
        ck_ref, cv_ref, rq_ref = next(it), next(it), next(it)
    pc_ref, ps_ref, perm_ref, uc_ref, us_ref = (next(it) for _ in range(5))
    cast_in = [next(it) for _ in range(n_cast)]
    if n_cast:
        cc_ref, c_ref, wm_ref, bm_ref = (next(it) for _ in range(4))
    o_ref, four_ref = next(it), next(it)
    for src, dst in zip(cast_in, [next(it) for _ in range(n_cast)]):
        dst[...] = src[...].astype(BF16)
    if n_cast:
        _modulation_block(cc_ref, c_ref, wm_ref[...].astype(BF16), bm_ref, next(it))
    kall_ref, vt_ref = next(it), next(it)
    fc_ref, fs_ref = next(it), next(it)
    nk = n + past
    half_n = n // 2
    tq = q_ref.shape[1]

    @pl.when(pl.program_id(1) == 0)
    def _per_batch_setup():
        ones = jnp.ones((ONES_ROWS, nk), BF16)
        for b in range(bt):
            if has_cache:
                kall_ref[b, 0:past, :] = ck_ref[b].T.astype(BF16)
            kall_ref[b, past:nk, :] = k_ref[b]
            for h in range(N_HEADS):
                if has_cache:
                    cvh = cv_ref[b, pl.ds(h, past, stride=N_HEADS), :]
                    vt_ref[b, h, 0:V_DIM, 0:past] = cvh.T.astype(BF16)
                vh = v_ref[b, :, h * V_DIM:(h + 1) * V_DIM].astype(F32)
                vt_ref[b, h, 0:V_DIM, past:nk] = vh.T.astype(BF16)
                vt_ref[b, h, V_DIM:V_DIM + ONES_ROWS, :] = ones
            blk = perm_ref.shape[0]
            nblk = half_n // blk
            first_row = lax.broadcasted_iota(jnp.int32, (blk, 1), 0) == 0
            for a in range(nblk):
                dst = slice(a * blk, (a + 1) * blk)
                src = slice((2 * nblk - 1 - a) * blk, (2 * nblk - a) * blk)
                rev_c = jnp.dot(perm_ref[...], uc_ref[b, src, :], preferred_element_type=F32)
                rev_s = jnp.dot(perm_ref[...], us_ref[b, src, :], preferred_element_type=F32)
                if a > 0:
                    edge = slice(n - a * blk, n - a * blk + 1)
                    rev_c = rev_c + jnp.where(first_row, uc_ref[b, edge, :].astype(F32), 0.0)
                    rev_s = rev_s + jnp.where(first_row, us_ref[b, edge, :].astype(F32), 0.0)
                fc_ref[b, dst, :] = (uc_ref[b, dst, :].astype(F32) + rev_c).astype(BF16)
                fs_ref[b, dst, :] = (us_ref[b, dst, :].astype(F32) - rev_s).astype(BF16)

    l1 = jnp.exp(jnp.sum(lq1_ref[...] * lk1_ref[...], axis=-1, keepdims=True))
    l2 = jnp.exp(jnp.sum(lq2_ref[...] * lk2_ref[...], axis=-1, keepdims=True))
    lam = l1 - l2 + LAM_INIT

    head = lambda h: slice(h * V_DIM, (h + 1) * V_DIM)
    qscale = qg_ref[...] * (LOG2E / math.sqrt(HEAD_DIM))
    zeros = jnp.zeros((HEAD_DIM, tq), BF16)
    nf = ROPE_AXIS_DIM // 2

    def prepare_q(b, h):
        qt = q_ref[b, :, head(h)].T
        halves = []
        for mhalf in range(2):
            t = qt[mhalf * HEAD_DIM:(mhalf + 1) * HEAD_DIM, :]
            t = t * lax.rsqrt(jnp.mean(t * t, axis=0, keepdims=True) + EPS) * qscale
            if has_cache:
                parts = []
                for axis in range(2):
                    x1 = t[(2 * axis) * nf:(2 * axis + 1) * nf, :]
                    x2 = t[(2 * axis + 1) * nf:(2 * axis + 2) * nf, :]
                    cos, sin = rq_ref[2 * axis], rq_ref[2 * axis + 1]
                    parts += [x1 * cos - x2 * sin, x2 * cos + x1 * sin]
                t = jnp.concatenate(parts, axis=0)
            halves.append(t.astype(BF16))
        return halves

    kc = min(KEY_CHUNK, nk)
    chunks = [slice(c * kc, (c + 1) * kc) for c in range(nk // kc)]
    chains = [(b, h, mhalf) for b in range(bt) for h in range(N_HEADS) for mhalf in range(2)]
    qts = {}

    def scores(chain):
        b, h, mhalf = chain
        if (b, h) not in qts:
            qts[b, h] = prepare_q(b, h)
        qm = jnp.concatenate([qts[b, h][0], zeros] if mhalf == 0 else [zeros, qts[b, h][1]], axis=0)

        def one(rows):
            s = jnp.dot(kall_ref[b, rows, head(h)], qm, preferred_element_type=F32)
            return s, jnp.max(s, axis=0, keepdims=True)
        return one

    def fourier_piece(b, rows):
        def emit():
            o = jnp.dot(pc_ref[rows, :], fc_ref[b], preferred_element_type=F32)
            o = o + jnp.dot(ps_ref[rows, :], fs_ref[b], preferred_element_type=F32)
            k_par = lax.broadcasted_iota(jnp.int32, (rows.stop - rows.start, 1), 0) % 2
            mid = jnp.where(k_par == 0, 1.0, -1.0) * (1.0 / math.sqrt(n))
            o = o + mid * uc_ref[b, half_n:half_n + 1, :].astype(F32)
            four_ref[b, rows, :] = o.astype(BF16)
        return emit

    fr = min(FOURIER_PIECE_ROWS, tq)
    pieces = [fourier_piece(b, slice(r * fr, (r + 1) * fr)) for b in range(bt) for r in range(tq // fr)]
    piece_every = len(chains) // len(pieces)

    depth = max(1, min(len(chains), PREFETCH_KEYS // nk))
    queue = [[scores(chain)(rows) for rows in chunks] for chain in chains[:depth]]
    acc = [None, None]
    for i, (b, h, mhalf) in enumerate(chains):
        cur = queue.pop(0)
        mx = functools.reduce(jnp.maximum, [m for _, m in cur])
        nxt = scores(chains[i + depth]) if i + depth < len(chains) else None
        issued = []
        a = None
        for c, rows in enumerate(chunks):
            if nxt is not None:
                issued.append(nxt(rows))
            e = jnp.exp2(cur[c][0] - mx).astype(BF16)
            part = jnp.dot(vt_ref[b, h, :, rows], e, preferred_element_type=F32)
            a = part if a is None else a + part
        if nxt is not None:
            queue.append(issued)
        if i % piece_every == 0:
            pieces[i // piece_every]()
        acc[mhalf] = a
        if mhalf == 1:
            a0, a1 = acc
            r0 = 1.0 / a0[V_DIM:V_DIM + 1, :]
            r1 = lam / a1[V_DIM:V_DIM + 1, :]
            o = (a0[0:V_DIM, :] * r0 - a1[0:V_DIM, :] * r1).T
            ms = jnp.mean(o * o, axis=-1, keepdims=True)
            o = o * lax.rsqrt(ms + EPS) * sg_ref[...] * (1.0 - LAM_INIT)
            o_ref[b, :, head(h)] = o.astype(BF16)


def _attention(lam_vecs, subln_g, qg_col, q, k, v, uc, us, cache_k=None, cache_v=None, side_jobs=None):
    nb, n, _ = q.shape
    tq = min(ATTN_Q_TILE, n)
    has_cache = cache_k is not None
    past = cache_k.shape[2] if has_cache else 0
    nk = n + past
    bt = max(1, min(nb, ATTN_STEP_KEYS // nk))
    qtile = pl.BlockSpec((bt, tq, QK_WIDTH), lambda b, t: (b, t, 0))
    per_batch = pl.BlockSpec((bt, n, QK_WIDTH), lambda b, t: (b, 0, 0))
    in_specs = [_resident((1, HEAD_DIM))] * 4 + [_resident((1, V_DIM)), _resident((HEAD_DIM, 1)),
                                                   qtile, per_batch, per_batch]
    args = list(lam_vecs) + [subln_g, qg_col, q, k, v]
    if has_cache:
        in_specs += [pl.BlockSpec((bt, QK_WIDTH, past), lambda b, t: (b, 0, 0)),
                     pl.BlockSpec((bt, past * N_HEADS, V_DIM), lambda b, t: (b, 0, 0)),
                     pl.BlockSpec((4, ROPE_AXIS_DIM // 2, tq), lambda b, t: (0, 0, t))]
        args += [cache_k, cache_v, jnp.asarray(_rope_tables_feature_major(n))]
    pc, ps = _position_dft(n)
    dft_rows = pl.BlockSpec((tq, n // 2), lambda b, t: (t, 0))
    mirror = min(MIRROR_BLOCK_ROWS, n // 2)
    in_specs += [dft_rows, dft_rows, _resident((mirror, mirror)), per_batch, per_batch]
    args += [jnp.asarray(pc).astype(BF16), jnp.asarray(ps).astype(BF16),
             jnp.asarray(_mirror_block(mirror)).astype(BF16), uc, us]
    nt = n // tq
    steps = (nb // bt) * nt
    out_specs = [qtile, qtile]
    out_shape = [jax.ShapeDtypeStruct((nb, n, ATTN_WIDTH), BF16), jax.ShapeDtypeStruct((nb, n, FOUR_WIDTH), BF16)]
    cast_weights = ()
    if side_jobs is not None:
        cast_weights, (c_ctx, c, w_mod, b_mod) = side_jobs
        slabs = [pl.BlockSpec((w.shape[0] // steps, w.shape[1]), lambda b, t: (b * nt + t, 0)) for w in cast_weights]
        tail = MOD_TAIL_COLS // steps
        first = MOD_HEAD_COLS // tail
        in_specs += slabs + [_resident(c_ctx.shape), _resident(c.shape),
                             pl.BlockSpec((D_MODEL, tail), lambda b, t: (0, first + b * nt + t)),
                             pl.BlockSpec((1, tail), lambda b, t: (0, first + b * nt + t))]
        args += list(cast_weights) + [c_ctx, c, w_mod, b_mod]
        out_specs += slabs + [pl.BlockSpec((MOD_ROWS, tail), lambda b, t: (0, b * nt + t))]
        out_shape += [jax.ShapeDtypeStruct(w.shape, BF16) for w in cast_weights]
        out_shape += [jax.ShapeDtypeStruct((MOD_ROWS, MOD_TAIL_COLS), F32)]
    return pl.pallas_call(
        functools.partial(_attn_kernel, has_cache=has_cache, n=n, past=past, bt=bt, n_cast=len(cast_weights)),
        grid=(nb // bt, nt),
        in_specs=in_specs,
        out_specs=out_specs,
        out_shape=out_shape,
        scratch_shapes=[pltpu.VMEM((bt, nk, QK_WIDTH), BF16),
                        pltpu.VMEM((bt, N_HEADS, V_DIM + ONES_ROWS, nk), BF16),
                        pltpu.VMEM((bt, n // 2, FOUR_WIDTH), BF16),
                        pltpu.VMEM((bt, n // 2, FOUR_WIDTH), BF16)],
        compiler_params=_params(2),
        name="diff_attention_cached" if has_cache else "diff_attention",
    )(*args)


OUT_TILE = 512
FF_CHUNK = 1024


def _out_mlp_kernel(x_ref, a_ref, f_ref, mod_ref, wo_ref, g2_ref, w1_ref, w2_ref, o_ref, *, mod_row0, per_batch_mod):
    m = _mod_row(mod_ref, mod_row0, per_batch_mod)
    g1 = m[:, 0:D_MODEL]
    sh2 = m[:, D_MODEL:2 * D_MODEL]
    sc2 = m[:, 2 * D_MODEL:3 * D_MODEL]
    g2 = m[:, 3 * D_MODEL:4 * D_MODEL]
    mix = jnp.dot(a_ref[0], wo_ref[0:ATTN_WIDTH, :], preferred_element_type=F32)
    mix = mix + jnp.dot(f_ref[0], wo_ref[ATTN_WIDTH:, :], preferred_element_type=F32)
    x1 = x_ref[0] + g1 * mix
    ms = jnp.mean(x1 * x1, axis=-1, keepdims=True)
    h = x1 * lax.rsqrt(ms + EPS) * g2_ref[...] * (1.0 + sc2) + sh2
    hb = h.astype(BF16)
    acc = jnp.zeros(x1.shape, F32)
    for c in range(D_FF // FF_CHUNK):
        t = jnp.dot(hb, w1_ref[:, c * FF_CHUNK:(c + 1) * FF_CHUNK], preferred_element_type=F32)
        t = jnp.square(jnp.maximum(t, 0.0)).astype(BF16)
        acc = acc + jnp.dot(t, w2_ref[c * FF_CHUNK:(c + 1) * FF_CHUNK, :], preferred_element_type=F32)
    o_ref[0] = x1 + g2 * acc


def _output_mlp(x3, attn, four, mod, mod_row0, per_batch_mod, w_out, norm2_g, w1, w2):
    nb, n, _ = x3.shape
    tm = OUT_TILE
    tile = lambda width: pl.BlockSpec((1, tm, width), lambda t, b: (b, t, 0))
    return pl.pallas_call(
        functools.partial(_out_mlp_kernel, mod_row0=mod_row0, per_batch_mod=per_batch_mod),
        grid=(n // tm, nb),
        in_specs=[tile(D_MODEL), tile(ATTN_WIDTH), tile(FOUR_WIDTH),
                  _resident((MOD_ROWS, MOD_TAIL_COLS)),
                  _resident((ATTN_WIDTH + FOUR_WIDTH, D_MODEL)),
                  _resident((1, D_MODEL)),
                  _resident((D_MODEL, D_FF)),
                  _resident((D_FF, D_MODEL))],
        out_specs=tile(D_MODEL),
        out_shape=jax.ShapeDtypeStruct((nb, n, D_MODEL), F32),
        compiler_params=_params(2),
        name="output_mlp",
    )(x3, attn, four, mod, w_out, norm2_g, w1, w2)


def kernel(x_prompt, x_sample, c, cache_k, cache_v, c_ctx, w_mod, b_mod, norm1_g, w_in, q_norm_g, k_norm_g,
           lambda_q1, lambda_k1, lambda_q2, lambda_k2, subln_g, w_four, w_out, norm2_g, w1, w2):
    batch, seq, _ = x_prompt.shape
    dec_batch, dec_seq, _ = x_sample.shape
    past = cache_k.shape[2]
    l = 0

    cc = c_ctx[None, :]
    bm = b_mod[l][None, :]
    mod, wcs, w_in_b = _modulation_head_and_fourier_weights(cc, c, w_mod[l], bm, w_four[l], w_in[l])

    n1 = norm1_g[l][None, :]
    n2 = norm2_g[l][None, :]
    qg_col = q_norm_g[l][:, None]
    kg = jnp.tile(k_norm_g[l], QK_WIDTH // HEAD_DIM)[None, :]
    sg = subln_g[l][None, :]
    lamv = tuple(t[l][None, :] for t in (lambda_q1, lambda_k1, lambda_q2, lambda_k2))
    gm = jnp.asarray(_head_mean_matrix()).astype(BF16)
    rope_tabs = tuple(jnp.asarray(t) for t in _rope_tables(dec_seq))

    qp, kp, vp, ucp, usp, kt32, v32 = _in_projection(x_prompt, mod, 0, False, n1, w_in_b, kg, gm, wcs, None, True)
    q, k, v, uc, us = _in_projection(x_sample, mod, 1, True, n1, w_in_b, kg, gm, wcs, rope_tabs, False)

    ck = cache_k[:, l].transpose(0, 2, 3, 4, 1).reshape(dec_batch, QK_WIDTH, past)
    cv = cache_v[:, l].reshape(dec_batch, past * N_HEADS, V_DIM)
    attn_s, four_s, w_out_b, w1_b, w2_b, mod_tail = _attention(
        lamv, sg, qg_col, q, k, v, uc, us, ck, cv, ((w_out[l], w1[l], w2[l]), (cc, c, w_mod[l], bm)))

    attn, four = _attention(lamv, sg, qg_col, qp, kp, vp, ucp, usp)
    flat = lambda t: t.reshape(1, batch * seq, t.shape[-1])
    yp = _output_mlp(flat(x_prompt), flat(attn), flat(four), mod_tail, 0, False, w_out_b, n2, w1_b, w2_b)
    y_prompt = yp.reshape(batch, seq, D_MODEL)
    new_cache_k = kt32.reshape(batch, N_HEADS, 2, HEAD_DIM, seq).transpose(0, 4, 1, 2, 3)[:, None]
    new_cache_v = v32.reshape(batch, 1, seq, N_HEADS, V_DIM)

    y_sample = _output_mlp(x_sample, attn_s, four_s, mod_tail, 1, True, w_out_b, n2, w1_b, w2_b)

    return (y_prompt, y_sample, new_cache_k, new_cache_v)
```

```python
import functools
import math

import jax
import jax.numpy as jnp
import numpy as np
from jax import lax
from jax.experimental import pallas as pl
from jax.experimental.pallas import tpu as pltpu

D_MODEL = 1024
N_HEADS = 4
HEAD_DIM = 64
V_DIM = 2 * HEAD_DIM
QK_WIDTH = N_HEADS * 2 * HEAD_DIM
ATTN_WIDTH = N_HEADS * V_DIM
N_FOUR_GROUPS = 4
FOUR_GROUP = 128
FOUR_WIDTH = N_FOUR_GROUPS * FOUR_GROUP
PROJ_WIDTH = 2 * QK_WIDTH + ATTN_WIDTH + FOUR_WIDTH
D_FF = 4 * D_MODEL
GRID_W = 64
ROPE_BASE = 10000.0
ROPE_AXIS_DIM = HEAD_DIM // 2
EPS = 1e-6
LAM_INIT = 0.8 - 0.6 * math.exp(-0.3 * 0)
LOG2E = 1.4426950408889634
MOD_ROWS = 8
MEAN_TILE = 256

F32 = jnp.float32
BF16 = jnp.bfloat16

VMEM_LIMIT_BYTES = 60 * 1024 * 1024


def _params(n_axes):
    return pltpu.CompilerParams(dimension_semantics=("arbitrary",) * n_axes,
                                vmem_limit_bytes=VMEM_LIMIT_BYTES)


def _resident(shape):
    nd = len(shape)
    return pl.BlockSpec(shape, lambda *_: (0,) * nd, pipeline_mode=pl.Buffered(1))


@functools.lru_cache(maxsize=None)
def _channel_dft():
    idx = np.arange(FOUR_GROUP)
    ang = 2.0 * np.pi * ((idx[:, None] * idx[None, :]) % FOUR_GROUP) / FOUR_GROUP
    s = 1.0 / np.sqrt(FOUR_GROUP)
    return np.stack([np.cos(ang) * s, np.sin(ang) * s]).astype(np.float32)


@functools.lru_cache(maxsize=None)
def _position_dft(n):
    idx = np.arange(n)
    ang = 2.0 * np.pi * ((idx[:, None] * idx[None, :]) % n) / n
    s = 1.0 / np.sqrt(n)
    return (np.cos(ang) * s).astype(np.float32), (-np.sin(ang) * s).astype(np.float32)


@functools.lru_cache(maxsize=None)
def _mirror_block(rows):
    p = np.zeros((rows, rows), np.float32)
    r = np.arange(1, rows)
    p[r, rows - r] = 1.0
    return p


@functools.lru_cache(maxsize=None)
def _rope_tables(n):
    rows = n // GRID_W
    row = np.repeat(np.arange(rows), GRID_W).astype(np.float64)
    col = np.tile(np.arange(GRID_W), rows).astype(np.float64)
    inv = ROPE_BASE ** (-np.arange(0, ROPE_AXIS_DIM, 2, dtype=np.float64) / ROPE_AXIS_DIM)
    d = np.arange(QK_WIDTH) % HEAD_DIM
    part = d // ROPE_AXIS_DIM
    i = d % ROPE_AXIS_DIM
    first = i < ROPE_AXIS_DIM // 2
    pos = np.where(part[None, :] == 0, row[:, None], col[:, None])
    ang = pos * inv[i % (ROPE_AXIS_DIM // 2)][None, :]
    cos = np.cos(ang)
    sin = np.where(first[None, :], -np.sin(ang), np.sin(ang))
    return cos.astype(np.float32), sin.astype(np.float32)


@functools.lru_cache(maxsize=None)
def _rope_tables_feature_major(n):
    rows = n // GRID_W
    row = np.repeat(np.arange(rows), GRID_W).astype(np.float64)
    col = np.tile(np.arange(GRID_W), rows).astype(np.float64)
    inv = ROPE_BASE ** (-np.arange(0, ROPE_AXIS_DIM, 2, dtype=np.float64) / ROPE_AXIS_DIM)
    ang_r = inv[:, None] * row[None, :]
    ang_c = inv[:, None] * col[None, :]
    return np.stack([np.cos(ang_r), np.sin(ang_r), np.cos(ang_c), np.sin(ang_c)]).astype(np.float32)


@functools.lru_cache(maxsize=None)
def _head_mean_matrix():
    g = np.kron(np.eye(MEAN_TILE // HEAD_DIM), np.ones((HEAD_DIM, HEAD_DIM))) / HEAD_DIM
    return g.astype(np.float32)


MOD_TILE = 512
MOD_HEAD_COLS = 2 * D_MODEL
MOD_TAIL_COLS = 4 * D_MODEL


def _modulation_block(cc_ref, c_ref, w, b_ref, o_ref):
    nd = c_ref.shape[0]

    def rows(cvec):
        s = cvec / (1.0 + jnp.exp(-cvec))
        return jnp.dot(s.astype(BF16), w, preferred_element_type=F32) + b_ref[...]

    o_ref[0:1, :] = rows(cc_ref[...])
    o_ref[1:1 + nd, :] = rows(c_ref[...])
    o_ref[1 + nd:, :] = jnp.zeros((MOD_ROWS - 1 - nd, o_ref.shape[1]), F32)


def _mod_kernel(cc_ref, c_ref, w_ref, b_ref, dft_ref, wf_ref, win_ref, o_ref, wcs_ref, winb_ref):
    winb_ref[...] = win_ref[...].astype(BF16)

    @pl.when(pl.program_id(0) == 0)
    def _fourier_weights():
        for g in range(N_FOUR_GROUPS):
            w = wf_ref[g]
            wc = jnp.dot(dft_ref[0], w, precision=lax.Precision.HIGHEST, preferred_element_type=F32)
            ws = jnp.dot(dft_ref[1], w, precision=lax.Precision.HIGHEST, preferred_element_type=F32)
            wcs_ref[g] = jnp.concatenate([wc, ws], axis=-1).astype(BF16)

    _modulation_block(cc_ref, c_ref, w_ref[...].astype(BF16), b_ref, o_ref)


def _modulation_head_and_fourier_weights(c_ctx, c, w_mod, b_mod, w_four, w_in):
    nd = c.shape[0]
    steps = MOD_HEAD_COLS // MOD_TILE
    dft = jnp.asarray(_channel_dft())
    slab = pl.BlockSpec((w_in.shape[0] // steps, w_in.shape[1]), lambda j: (j, 0))
    return pl.pallas_call(
        _mod_kernel,
        grid=(steps,),
        in_specs=[pl.BlockSpec((1, D_MODEL), lambda j: (0, 0)),
                  pl.BlockSpec((nd, D_MODEL), lambda j: (0, 0)),
                  pl.BlockSpec((D_MODEL, MOD_TILE), lambda j: (0, j)),
                  pl.BlockSpec((1, MOD_TILE), lambda j: (0, j)),
                  pl.BlockSpec((2, FOUR_GROUP, FOUR_GROUP), lambda j: (0, 0, 0)),
                  pl.BlockSpec((N_FOUR_GROUPS, FOUR_GROUP, FOUR_GROUP), lambda j: (0, 0, 0)),
                  slab],
        out_specs=[pl.BlockSpec((MOD_ROWS, MOD_TILE), lambda j: (0, j)),
                   pl.BlockSpec((N_FOUR_GROUPS, FOUR_GROUP, 2 * FOUR_GROUP), lambda j: (0, 0, 0)),
                   slab],
        out_shape=[jax.ShapeDtypeStruct((MOD_ROWS, MOD_HEAD_COLS), F32),
                   jax.ShapeDtypeStruct((N_FOUR_GROUPS, FOUR_GROUP, 2 * FOUR_GROUP), BF16),
                   jax.ShapeDtypeStruct(w_in.shape, BF16)],
        compiler_params=_params(1),
        name="modulation",
    )(c_ctx, c, w_mod, b_mod, dft, w_four, w_in)


PROJ_TILE = 1024


def _mod_row(mod_ref, mod_row0, per_batch_mod):
    if per_batch_mod:
        return mod_ref[pl.ds(mod_row0 + pl.program_id(1), 1), :]
    return mod_ref[mod_row0:mod_row0 + 1, :]


def _inproj_kernel(*refs, rope, emit_cache, bt, tn, mod_row0, per_batch_mod):
    it = iter(refs)
    x_ref, mod_ref, g1_ref, w_ref, kg_ref, gm_ref, wcs_ref = (next(it) for _ in range(7))
    if rope:
        cos_ref, sin_ref = next(it), next(it)
    q_ref, k_ref, v_ref, uc_ref, us_ref = (next(it) for _ in range(5))
    if emit_cache:
        k32_ref, v32_ref = next(it), next(it)

    x = x_ref[...].reshape(bt * tn, D_MODEL)
    m = _mod_row(mod_ref, mod_row0, per_batch_mod)
    sh1 = m[:, 0:D_MODEL]
    sc1 = m[:, D_MODEL:2 * D_MODEL]
    ms = jnp.mean(x * x, axis=-1, keepdims=True)
    y = x * lax.rsqrt(ms + EPS) * g1_ref[...]
    h = y * (1.0 + sc1) + sh1
    p = jnp.dot(h.astype(BF16), w_ref[...], preferred_element_type=F32)
    q = p[:, 0:QK_WIDTH]
    k = p[:, QK_WIDTH:2 * QK_WIDTH]
    v = p[:, 2 * QK_WIDTH:2 * QK_WIDTH + ATTN_WIDTH]

    k2 = (k * k).astype(BF16)
    msq = jnp.concatenate([jnp.dot(k2[:, c:c + MEAN_TILE], gm_ref[...], preferred_element_type=F32)
                           for c in range(0, QK_WIDTH, MEAN_TILE)], axis=-1)
    k = k * lax.rsqrt(msq + EPS) * kg_ref[...]
    if emit_cache:
        for i in range(bt):
            rows = slice(i * tn, (i + 1) * tn)
            k32_ref[i] = k[rows, :].T
            for hd in range(N_HEADS):
                v32_ref[i, pl.ds(hd, tn, stride=N_HEADS), :] = v[rows, hd * V_DIM:(hd + 1) * V_DIM]

    if rope:
        lane = lax.broadcasted_iota(jnp.int32, (k.shape[0], V_DIM), 1)
        first = (lane % ROPE_AXIS_DIM) < (ROPE_AXIS_DIM // 2)
        half = ROPE_AXIS_DIM // 2
        cols = [k[:, hd * V_DIM:(hd + 1) * V_DIM] for hd in range(N_HEADS)]
        sw = jnp.concatenate([jnp.where(first, pltpu.roll(t, V_DIM - half, 1), pltpu.roll(t, half, 1))
                              for t in cols], axis=-1)
        k = k * cos_ref[...] + sw * sin_ref[...]

    blk = lambda t: t.reshape(bt, tn, t.shape[-1])
    q_ref[...] = blk(q)
    k_ref[...] = blk(k.astype(BF16))
    v_ref[...] = blk(v.astype(BF16))

    f0 = 2 * QK_WIDTH + ATTN_WIDTH
    for g in range(N_FOUR_GROUPS):
        fg = p[:, f0 + g * FOUR_GROUP:f0 + (g + 1) * FOUR_GROUP].astype(BF16)
        u = jnp.dot(fg, wcs_ref[g], preferred_element_type=F32)
        uc_ref[:, :, g * FOUR_GROUP:(g + 1) * FOUR_GROUP] = blk(u[:, 0:FOUR_GROUP].astype(BF16))
        us_ref[:, :, g * FOUR_GROUP:(g + 1) * FOUR_GROUP] = blk(u[:, FOUR_GROUP:].astype(BF16))


def _in_projection(x3, mod, mod_row0, per_batch_mod, norm_g, w_in, kg, gm, wcs, rope_tabs, emit_cache):
    nb, n, _ = x3.shape
    tn = min(PROJ_TILE, n)
    bt = PROJ_TILE // tn
    assert per_batch_mod is False or bt == 1
    rope = rope_tabs is not None
    tile = lambda width: pl.BlockSpec((bt, tn, width), lambda t, b: (b, t, 0))
    in_specs = [tile(D_MODEL),
                _resident((MOD_ROWS, MOD_HEAD_COLS)),
                _resident((1, D_MODEL)),
                _resident((D_MODEL, PROJ_WIDTH)),
                _resident((1, QK_WIDTH)),
                _resident((MEAN_TILE, MEAN_TILE)),
                _resident((N_FOUR_GROUPS, FOUR_GROUP, 2 * FOUR_GROUP))]
    args = [x3, mod, norm_g, w_in, kg, gm, wcs]
    if rope:
        in_specs += [pl.BlockSpec((tn, QK_WIDTH), lambda t, b: (t, 0))] * 2
        args += list(rope_tabs)
    out_shape = [jax.ShapeDtypeStruct((nb, n, QK_WIDTH), F32)]
    out_shape += [jax.ShapeDtypeStruct((nb, n, QK_WIDTH), BF16)] * 4
    out_specs = [tile(QK_WIDTH)] * 5
    if emit_cache:
        assert tn == n
        out_shape += [jax.ShapeDtypeStruct((nb, QK_WIDTH, n), F32),
                      jax.ShapeDtypeStruct((nb, n * N_HEADS, V_DIM), F32)]
        out_specs += [pl.BlockSpec((bt, QK_WIDTH, n), lambda t, b: (b, 0, 0)),
                      pl.BlockSpec((bt, n * N_HEADS, V_DIM), lambda t, b: (b, 0, 0))]
    return pl.pallas_call(
        functools.partial(_inproj_kernel, rope=rope, emit_cache=emit_cache, bt=bt, tn=tn,
                          mod_row0=mod_row0, per_batch_mod=per_batch_mod),
        grid=(n // tn, nb // bt),
        in_specs=in_specs,
        out_specs=out_specs,
        out_shape=out_shape,
        compiler_params=_params(2),
        name="in_projection_rope" if rope else "in_projection",
    )(*args)


ATTN_Q_TILE = 512
KEY_CHUNK = 1280
ONES_ROWS = 16
FOURIER_PIECE_ROWS = 128
MIRROR_BLOCK_ROWS = 256
PREFETCH_KEYS = 1024
ATTN_STEP_KEYS = 1024


def _attn_kernel(*refs, has_cache, n, past, bt, n_cast):
    it = iter(refs)
    lq1_ref, lk1_ref, lq2_ref, lk2_ref, sg_ref, qg_ref, q_ref, k_ref, v_ref = (next(it) for _ in range(9))
    if has_cache:
        ck_ref, cv_ref, rq_ref = next(it), next(it), next(it)
    pc_ref, ps_ref, perm_ref, uc_ref, us_ref = (next(it) for _ in range(5))
    cast_in = [next(it) for _ in range(n_cast)]
    if n_cast:
        cc_ref, c_ref, wm_ref, bm_ref = (next(it) for _ in range(4))
    o_ref, four_ref = next(it), next(it)
    for src, dst in zip(cast_in, [next(it) for _ in range(n_cast)]):
        dst[...] = src[...].astype(BF16)
    if n_cast:
        _modulation_block(cc_ref, c_ref, wm_ref[...].astype(BF16), bm_ref, next(it))
    kall_ref, vt_ref = next(it), next(it)
    fc_ref, fs_ref = next(it), next(it)
    nk = n + past
    half_n = n // 2
    tq = q_ref.shape[1]

    @pl.when(pl.program_id(1) == 0)
    def _per_batch_setup():
        ones = jnp.ones((ONES_ROWS, nk), BF16)
        for b in range(bt):
            if has_cache:
                kall_ref[b, 0:past, :] = ck_ref[b].T.astype(BF16)
            kall_ref[b, past:nk, :] = k_ref[b]
            for h in range(N_HEADS):
                if has_cache:
                    cvh = cv_ref[b, pl.ds(h, past, stride=N_HEADS), :]
                    vt_ref[b, h, 0:V_DIM, 0:past] = cvh.T.astype(BF16)
                vh = v_ref[b, :, h * V_DIM:(h + 1) * V_DIM].astype(F32)
                vt_ref[b, h, 0:V_DIM, past:nk] = vh.T.astype(BF16)
                vt_ref[b, h, V_DIM:V_DIM + ONES_ROWS, :] = ones
            blk = perm_ref.shape[0]
            nblk = half_n // blk
            first_row = lax.broadcasted_iota(jnp.int32, (blk, 1), 0) == 0
            for a in range(nblk):
                dst = slice(a * blk, (a + 1) * blk)
                src = slice((2 * nblk - 1 - a) * blk, (2 * nblk - a) * blk)
                rev_c = jnp.dot(perm_ref[...], uc_ref[b, src, :], preferred_element_type=F32)
                rev_s = jnp.dot(perm_ref[...], us_ref[b, src, :], preferred_element_type=F32)
                if a > 0:
                    edge = slice(n - a * blk, n - a * blk + 1)
                    rev_c = rev_c + jnp.where(first_row, uc_ref[b, edge, :].astype(F32), 0.0)
                    rev_s = rev_s + jnp.where(first_row, us_ref[b, edge, :].astype(F32), 0.0)
                fc_ref[b, dst, :] = (uc_ref[b, dst, :].astype(F32) + rev_c).astype(BF16)
                fs_ref[b, dst, :] = (us_ref[b, dst, :].astype(F32) - rev_s).astype(BF16)

    l1 = jnp.exp(jnp.sum(lq1_ref[...] * lk1_ref[...], axis=-1, keepdims=True))
    l2 = jnp.exp(jnp.sum(lq2_ref[...] * lk2_ref[...], axis=-1, keepdims=True))
    lam = l1 - l2 + LAM_INIT

    head = lambda h: slice(h * V_DIM, (h + 1) * V_DIM)
    qscale = qg_ref[...] * (LOG2E / math.sqrt(HEAD_DIM))
    zeros = jnp.zeros((HEAD_DIM, tq), BF16)
    nf = ROPE_AXIS_DIM // 2

    def prepare_q(b, h):
        qt = q_ref[b, :, head(h)].T
        halves = []
        for mhalf in range(2):
            t = qt[mhalf * HEAD_DIM:(mhalf + 1) * HEAD_DIM, :]
            t = t * lax.rsqrt(jnp.mean(t * t, axis=0, keepdims=True) + EPS) * qscale
            if has_cache:
                parts = []
                for axis in range(2):
                    x1 = t[(2 * axis) * nf:(2 * axis + 1) * nf, :]
                    x2 = t[(2 * axis + 1) * nf:(2 * axis + 2) * nf, :]
                    cos, sin = rq_ref[2 * axis], rq_ref[2 * axis + 1]
                    parts += [x1 * cos - x2 * sin, x2 * cos + x1 * sin]
                t = jnp.concatenate(parts, axis=0)
            halves.append(t.astype(BF16))
        return halves

    kc = min(KEY_CHUNK, nk)
    chunks = [slice(c * kc, (c + 1) * kc) for c in range(nk // kc)]
    chains = [(b, h, mhalf) for b in range(bt) for h in range(N_HEADS) for mhalf in range(2)]
    qts = {}

    def scores(chain):
        b, h, mhalf = chain
        if (b, h) not in qts:
            qts[b, h] = prepare_q(b, h)
        qm = jnp.concatenate([qts[b, h][0], zeros] if mhalf == 0 else [zeros, qts[b, h][1]], axis=0)

        def one(rows):
            s = jnp.dot(kall_ref[b, rows, head(h)], qm, preferred_element_type=F32)
            return s, jnp.max(s, axis=0, keepdims=True)
        return one

    def fourier_piece(b, rows):
        def emit():
            o = jnp.dot(pc_ref[rows, :], fc_ref[b], preferred_element_type=F32)
            o = o + jnp.dot(ps_ref[rows, :], fs_ref[b], preferred_element_type=F32)
            k_par = lax.broadcasted_iota(jnp.int32, (rows.stop - rows.start, 1), 0) % 2
            mid = jnp.where(k_par == 0, 1.0, -1.0) * (1.0 / math.sqrt(n))
            o = o + mid * uc_ref[b, half_n:half_n + 1, :].astype(F32)
            four_ref[b, rows, :] = o.astype(BF16)
        return emit

    fr = min(FOURIER_PIECE_ROWS, tq)
    pieces = [fourier_piece(b, slice(r * fr, (r + 1) * fr)) for b in range(bt) for r in range(tq // fr)]
    piece_every = len(chains) // len(pieces)

    depth = max(1, min(len(chains), PREFETCH_KEYS // nk))
    queue = [[scores(chain)(rows) for rows in chunks] for chain in chains[:depth]]
    acc = [None, None]
    for i, (b, h, mhalf) in enumerate(chains):
        cur = queue.pop(0)
        mx = functools.reduce(jnp.maximum, [m for _, m in cur])
        nxt = scores(chains[i + depth]) if i + depth < len(chains) else None
        issued = []
        a = None
        for c, rows in enumerate(chunks):
            if nxt is not None:
                issued.append(nxt(rows))
            e = jnp.exp2(cur[c][0] - mx).astype(BF16)
            part = jnp.dot(vt_ref[b, h, :, rows], e, preferred_element_type=F32)
            a = part if a is None else a + part
        if nxt is not None:
            queue.append(issued)
        if i % piece_every == 0:
            pieces[i // piece_every]()
        acc[mhalf] = a
        if mhalf == 1:
            a0, a1 = acc
            r0 = 1.0 / a0[V_DIM:V_DIM + 1, :]
            r1 = lam / a1[V_DIM:V_DIM + 1, :]
            o = (a0[0:V_DIM, :] * r0 - a1[0:V_DIM, :] * r1).T
            ms = jnp.mean(o * o, axis=-1, keepdims=True)
            o = o * lax.rsqrt(ms + EPS) * sg_ref[...] * (1.0 - LAM_INIT)
            o_ref[b, :, head(h)] = o.astype(BF16)


def _attention(lam_vecs, subln_g, qg_col, q, k, v, uc, us, cache_k=None, cache_v=None, side_jobs=None):
    nb, n, _ = q.shape
    tq = min(ATTN_Q_TILE, n)
    has_cache = cache_k is not None
    past = cache_k.shape[2] if has_cache else 0
    nk = n + past
    bt = max(1, min(nb, ATTN_STEP_KEYS // nk))
    qtile = pl.BlockSpec((bt, tq, QK_WIDTH), lambda b, t: (b, t, 0))
    per_batch = pl.BlockSpec((bt, n, QK_WIDTH), lambda b, t: (b, 0, 0))
    in_specs = [_resident((1, HEAD_DIM))] * 4 + [_resident((1, V_DIM)), _resident((HEAD_DIM, 1)),
                                                   qtile, per_batch, per_batch]
    args = list(lam_vecs) + [subln_g, qg_col, q, k, v]
    if has_cache:
        in_specs += [pl.BlockSpec((bt, QK_WIDTH, past), lambda b, t: (b, 0, 0)),
                     pl.BlockSpec((bt, past * N_HEADS, V_DIM), lambda b, t: (b, 0, 0)),
                     pl.BlockSpec((4, ROPE_AXIS_DIM // 2, tq), lambda b, t: (0, 0, t))]
        args += [cache_k, cache_v, jnp.asarray(_rope_tables_feature_major(n))]
    pc, ps = _position_dft(n)
    dft_rows = pl.BlockSpec((tq, n // 2), lambda b, t: (t, 0))
    mirror = min(MIRROR_BLOCK_ROWS, n // 2)
    in_specs += [dft_rows, dft_rows, _resident((mirror, mirror)), per_batch, per_batch]
    args += [jnp.asarray(pc).astype(BF16), jnp.asarray(ps).astype(BF16),
             jnp.asarray(_mirror_block(mirror)).astype(BF16), uc, us]
    nt = n // tq
    steps = (nb // bt) * nt
    out_specs = [qtile, qtile]
    out_shape = [jax.ShapeDtypeStruct((nb, n, ATTN_WIDTH), BF16), jax.ShapeDtypeStruct((nb, n, FOUR_WIDTH), BF16)]
    cast_weights = ()
    if side_jobs is not None:
        cast_weights, (c_ctx, c, w_mod, b_mod) = side_jobs
        slabs = [pl.BlockSpec((w.shape[0] // steps, w.shape[1]), lambda b, t: (b * nt + t, 0)) for w in cast_weights]
        tail = MOD_TAIL_COLS // steps
        first = MOD_HEAD_COLS // tail
        in_specs += slabs + [_resident(c_ctx.shape), _resident(c.shape),
                             pl.BlockSpec((D_MODEL, tail), lambda b, t: (0, first + b * nt + t)),
                             pl.BlockSpec((1, tail), lambda b, t: (0, first + b * nt + t))]
        args += list(cast_weights) + [c_ctx, c, w_mod, b_mod]
        out_specs += slabs + [pl.BlockSpec((MOD_ROWS, tail), lambda b, t: (0, b * nt + t))]
        out_shape += [jax.ShapeDtypeStruct(w.shape, BF16) for w in cast_weights]
        out_shape += [jax.ShapeDtypeStruct((MOD_ROWS, MOD_TAIL_COLS), F32)]
    return pl.pallas_call(
        functools.partial(_attn_kernel, has_cache=has_cache, n=n, past=past, bt=bt, n_cast=len(cast_weights)),
        grid=(nb // bt, nt),
        in_specs=in_specs,
        out_specs=out_specs,
        out_shape=out_shape,
        scratch_shapes=[pltpu.VMEM((bt, nk, QK_WIDTH), BF16),
                        pltpu.VMEM((bt, N_HEADS, V_DIM + ONES_ROWS, nk), BF16),
                        pltpu.VMEM((bt, n // 2, FOUR_WIDTH), BF16),
                        pltpu.VMEM((bt, n // 2, FOUR_WIDTH), BF16)],
        compiler_params=_params(2),
        name="diff_attention_cached" if has_cache else "diff_attention",
    )(*args)


OUT_TILE = 512
FF_CHUNK = 1024


def _out_mlp_kernel(x_ref, a_ref, f_ref, mod_ref, wo_ref, g2_ref, w1_ref, w2_ref, o_ref, *, mod_row0, per_batch_mod):
    m = _mod_row(mod_ref, mod_row0, per_batch_mod)
    g1 = m[:, 0:D_MODEL]
    sh2 = m[:, D_MODEL:2 * D_MODEL]
    sc2 = m[:, 2 * D_MODEL:3 * D_MODEL]
    g2 = m[:, 3 * D_MODEL:4 * D_MODEL]
    mix = jnp.dot(a_ref[0], wo_ref[0:ATTN_WIDTH, :], preferred_element_type=F32)
    mix = mix + jnp.dot(f_ref[0], wo_ref[ATTN_WIDTH:, :], preferred_element_type=F32)
    x1 = x_ref[0] + g1 * mix
    ms = jnp.mean(x1 * x1, axis=-1, keepdims=True)
    h = x1 * lax.rsqrt(ms + EPS) * g2_ref[...] * (1.0 + sc2) + sh2
    hb = h.astype(BF16)
    acc = jnp.zeros(x1.shape, F32)
    for c in range(D_FF // FF_CHUNK):
        t = jnp.dot(hb, w1_ref[:, c * FF_CHUNK:(c + 1) * FF_CHUNK], preferred_element_type=F32)
        t = jnp.square(jnp.maximum(t, 0.0)).astype(BF16)
        acc = acc + jnp.dot(t, w2_ref[c * FF_CHUNK:(c + 1) * FF_CHUNK, :], preferred_element_type=F32)
    o_ref[0] = x1 + g2 * acc


def _output_mlp(x3, attn, four, mod, mod_row0, per_batch_mod, w_out, norm2_g, w1, w2):
    nb, n, _ = x3.shape
    tm = OUT_TILE
    tile = lambda width: pl.BlockSpec((1, tm, width), lambda t, b: (b, t, 0))
    return pl.pallas_call(
        functools.partial(_out_mlp_kernel, mod_row0=mod_row0, per_batch_mod=per_batch_mod),
        grid=(n // tm, nb),
        in_specs=[tile(D_MODEL), tile(ATTN_WIDTH), tile(FOUR_WIDTH),
                  _resident((MOD_ROWS, MOD_TAIL_COLS)),
                  _resident((ATTN_WIDTH + FOUR_WIDTH, D_MODEL)),
                  _resident((1, D_MODEL)),
                  _resident((D_MODEL, D_FF)),
                  _resident((D_FF, D_MODEL))],
        out_specs=tile(D_MODEL),
        out_shape=jax.ShapeDtypeStruct((nb, n, D_MODEL), F32),
        compiler_params=_params(2),
        name="output_mlp",
    )(x3, attn, four, mod, w_out, norm2_g, w1, w2)


def kernel(x_prompt, x_sample, c, cache_k, cache_v, c_ctx, w_mod, b_mod, norm1_g, w_in, q_norm_g, k_norm_g,
           lambda_q1, lambda_k1, lambda_q2, lambda_k2, subln_g, w_four, w_out, norm2_g, w1, w2):
    batch, seq, _ = x_prompt.shape
    dec_batch, dec_seq, _ = x_sample.shape
    past = cache_k.shape[2]
    l = 0

    cc = c_ctx[None, :]
    bm = b_mod[l][None, :]
    mod, wcs, w_in_b = _modulation_head_and_fourier_weights(cc, c, w_mod[l], bm, w_four[l], w_in[l])

    n1 = norm1_g[l][None, :]
    n2 = norm2_g[l][None, :]
    qg_col = q_norm_g[l][:, None]
    kg = jnp.tile(k_norm_g[l], QK_WIDTH // HEAD_DIM)[None, :]
    sg = subln_g[l][None, :]
    lamv = tuple(t[l][None, :] for t in (lambda_q1, lambda_k1, lambda_q2, lambda_k2))
    gm = jnp.asarray(_head_mean_matrix()).astype(BF16)
    rope_tabs = tuple(jnp.asarray(t) for t in _rope_tables(dec_seq))

    qp, kp, vp, ucp, usp, kt32, v32 = _in_projection(x_prompt, mod, 0, False, n1, w_in_b, kg, gm, wcs, None, True)
    q, k, v, uc, us = _in_projection(x_sample, mod, 1, True, n1, w_in_b, kg, gm, wcs, rope_tabs, False)

    ck = cache_k[:, l].transpose(0, 2, 3, 4, 1).reshape(dec_batch, QK_WIDTH, past)
    cv = cache_v[:, l].reshape(dec_batch, past * N_HEADS, V_DIM)
    attn_s, four_s, w_out_b, w1_b, w2_b, mod_tail = _attention(
        lamv, sg, qg_col, q, k, v, uc, us, ck, cv, ((w_out[l], w1[l], w2[l]), (cc, c, w_mod[l], bm)))

    attn, four = _attention(lamv, sg, qg_col, qp, kp, vp, ucp, usp)
    flat = lambda t: t.reshape(1, batch * seq, t.shape[-1])
    yp = _output_mlp(flat(x_prompt), flat(attn), flat(four), mod_tail, 0, False, w_out_b, n2, w1_b, w2_b)
    y_prompt = yp.reshape(batch, seq, D_MODEL)
    new_cache_k = kt32.reshape(batch, N_HEADS, 2, HEAD_DIM, seq).transpose(0, 4, 1, 2, 3)[:, None]
    new_cache_v = v32.reshape(batch, 1, seq, N_HEADS, V_DIM)

    y_sample = _output_mlp(x_sample, attn_s, four_s, mod_tail, 1, True, w_out_b, n2, w1_b, w2_b)

    return (y_prompt, y_sample, new_cache_k, new_cache_v)
```

```python
import functools
import math

import jax
import jax.numpy as jnp
import numpy as np
from jax import lax
from jax.experimental import pallas as pl
from jax.experimental.pallas import tpu as pltpu

D_MODEL = 1024
N_HEADS = 4
HEAD_DIM = 64
V_DIM = 2 * HEAD_DIM
QK_WIDTH = N_HEADS * 2 * HEAD_DIM
ATTN_WIDTH = N_HEADS * V_DIM
N_FOUR_GROUPS = 4
FOUR_GROUP = 128
FOUR_WIDTH = N_FOUR_GROUPS * FOUR_GROUP
PROJ_WIDTH = 2 * QK_WIDTH + ATTN_WIDTH + FOUR_WIDTH
D_FF = 4 * D_MODEL
GRID_W = 64
ROPE_BASE = 10000.0
ROPE_AXIS_DIM = HEAD_DIM // 2
EPS = 1e-6
LAM_INIT = 0.8 - 0.6 * math.exp(-0.3 * 0)
LOG2E = 1.4426950408889634
MOD_ROWS = 8
MEAN_TILE = 256

F32 = jnp.float32
BF16 = jnp.bfloat16

VMEM_LIMIT_BYTES = 60 * 1024 * 1024


def _params(n_axes):
    return pltpu.CompilerParams(dimension_semantics=("arbitrary",) * n_axes,
                                vmem_limit_bytes=VMEM_LIMIT_BYTES)


def _resident(shape):
    nd = len(shape)
    return pl.BlockSpec(shape, lambda *_: (0,) * nd, pipeline_mode=pl.Buffered(1))


@functools.lru_cache(maxsize=None)
def _channel_dft():
    idx = np.arange(FOUR_GROUP)
    ang = 2.0 * np.pi * ((idx[:, None] * idx[None, :]) % FOUR_GROUP) / FOUR_GROUP
    s = 1.0 / np.sqrt(FOUR_GROUP)
    return np.stack([np.cos(ang) * s, np.sin(ang) * s]).astype(np.float32)


@functools.lru_cache(maxsize=None)
def _position_dft(n):
    idx = np.arange(n)
    ang = 2.0 * np.pi * ((idx[:, None] * idx[None, :]) % n) / n
    s = 1.0 / np.sqrt(n)
    return (np.cos(ang) * s).astype(np.float32), (-np.sin(ang) * s).astype(np.float32)


@functools.lru_cache(maxsize=None)
def _mirror_block(rows):
    p = np.zeros((rows, rows), np.float32)
    r = np.arange(1, rows)
    p[r, rows - r] = 1.0
    return p


@functools.lru_cache(maxsize=None)
def _rope_tables(n):
    rows = n // GRID_W
    row = np.repeat(np.arange(rows), GRID_W).astype(np.float64)
    col = np.tile(np.arange(GRID_W), rows).astype(np.float64)
    inv = ROPE_BASE ** (-np.arange(0, ROPE_AXIS_DIM, 2, dtype=np.float64) / ROPE_AXIS_DIM)
    d = np.arange(QK_WIDTH) % HEAD_DIM
    part = d // ROPE_AXIS_DIM
    i = d % ROPE_AXIS_DIM
    first = i < ROPE_AXIS_DIM // 2
    pos = np.where(part[None, :] == 0, row[:, None], col[:, None])
    ang = pos * inv[i % (ROPE_AXIS_DIM // 2)][None, :]
    cos = np.cos(ang)
    sin = np.where(first[None, :], -np.sin(ang), np.sin(ang))
    return cos.astype(np.float32), sin.astype(np.float32)


@functools.lru_cache(maxsize=None)
def _rope_tables_feature_major(n):
    rows = n // GRID_W
    row = np.repeat(np.arange(rows), GRID_W).astype(np.float64)
    col = np.tile(np.arange(GRID_W), rows).astype(np.float64)
    inv = ROPE_BASE ** (-np.arange(0, ROPE_AXIS_DIM, 2, dtype=np.float64) / ROPE_AXIS_DIM)
    ang_r = inv[:, None] * row[None, :]
    ang_c = inv[:, None] * col[None, :]
    return np.stack([np.cos(ang_r), np.sin(ang_r), np.cos(ang_c), np.sin(ang_c)]).astype(np.float32)


@functools.lru_cache(maxsize=None)
def _head_mean_matrix():
    g = np.kron(np.eye(MEAN_TILE // HEAD_DIM), np.ones((HEAD_DIM, HEAD_DIM))) / HEAD_DIM
    return g.astype(np.float32)


MOD_TILE = 512
MOD_HEAD_COLS = 2 * D_MODEL
MOD_TAIL_COLS = 4 * D_MODEL


def _modulation_block(cc_ref, c_ref, w, b_ref, o_ref):
    nd = c_ref.shape[0]

    def rows(cvec):
        s = cvec / (1.0 + jnp.exp(-cvec))
        return jnp.dot(s.astype(BF16), w, preferred_element_type=F32) + b_ref[...]

    o_ref[0:1, :] = rows(cc_ref[...])
    o_ref[1:1 + nd, :] = rows(c_ref[...])
    o_ref[1 + nd:, :] = jnp.zeros((MOD_ROWS - 1 - nd, o_ref.shape[1]), F32)


def _mod_kernel(cc_ref, c_ref, w_ref, b_ref, dft_ref, wf_ref, win_ref, o_ref, wcs_ref, winb_ref):
    winb_ref[...] = win_ref[...].astype(BF16)

    @pl.when(pl.program_id(0) == 0)
    def _fourier_weights():
        for g in range(N_FOUR_GROUPS):
            w = wf_ref[g]
            wc = jnp.dot(dft_ref[0], w, precision=lax.Precision.HIGHEST, preferred_element_type=F32)
            ws = jnp.dot(dft_ref[1], w, precision=lax.Precision.HIGHEST, preferred_element_type=F32)
            wcs_ref[g] = jnp.concatenate([wc, ws], axis=-1).astype(BF16)

    _modulation_block(cc_ref, c_ref, w_ref[...].astype(BF16), b_ref, o_ref)


def _modulation_head_and_fourier_weights(c_ctx, c, w_mod, b_mod, w_four, w_in):
    nd = c.shape[0]
    steps = MOD_HEAD_COLS // MOD_TILE
    dft = jnp.asarray(_channel_dft())
    slab = pl.BlockSpec((w_in.shape[0] // steps, w_in.shape[1]), lambda j: (j, 0))
    return pl.pallas_call(
        _mod_kernel,
        grid=(steps,),
        in_specs=[pl.BlockSpec((1, D_MODEL), lambda j: (0, 0)),
                  pl.BlockSpec((nd, D_MODEL), lambda j: (0, 0)),
                  pl.BlockSpec((D_MODEL, MOD_TILE), lambda j: (0, j)),
                  pl.BlockSpec((1, MOD_TILE), lambda j: (0, j)),
                  pl.BlockSpec((2, FOUR_GROUP, FOUR_GROUP), lambda j: (0, 0, 0)),
                  pl.BlockSpec((N_FOUR_GROUPS, FOUR_GROUP, FOUR_GROUP), lambda j: (0, 0, 0)),
                  slab],
        out_specs=[pl.BlockSpec((MOD_ROWS, MOD_TILE), lambda j: (0, j)),
                   pl.BlockSpec((N_FOUR_GROUPS, FOUR_GROUP, 2 * FOUR_GROUP), lambda j: (0, 0, 0)),
                   slab],
        out_shape=[jax.ShapeDtypeStruct((MOD_ROWS, MOD_HEAD_COLS), F32),
                   jax.ShapeDtypeStruct((N_FOUR_GROUPS, FOUR_GROUP, 2 * FOUR_GROUP), BF16),
                   jax.ShapeDtypeStruct(w_in.shape, BF16)],
        compiler_params=_params(1),
        name="modulation",
    )(c_ctx, c, w_mod, b_mod, dft, w_four, w_in)


PROJ_TILE = 1024


def _mod_row(mod_ref, mod_row0, per_batch_mod):
    if per_batch_mod:
        return mod_ref[pl.ds(mod_row0 + pl.program_id(1), 1), :]
    return mod_ref[mod_row0:mod_row0 + 1, :]


def _inproj_kernel(*refs, rope, emit_cache, bt, tn, mod_row0, per_batch_mod):
    it = iter(refs)
    x_ref, mod_ref, g1_ref, w_ref, kg_ref, gm_ref, wcs_ref = (next(it) for _ in range(7))
    if rope:
        cos_ref, sin_ref = next(it), next(it)
    if emit_cache:
        kgc_ref = next(it)
    q_ref, k_ref, v_ref, uc_ref, us_ref = (next(it) for _ in range(5))
    if emit_cache:
        k32_ref, v32_ref = next(it), next(it)

    x = x_ref[...].reshape(bt * tn, D_MODEL)
    m = _mod_row(mod_ref, mod_row0, per_batch_mod)
    sh1 = m[:, 0:D_MODEL]
    sc1 = m[:, D_MODEL:2 * D_MODEL]
    ms = jnp.mean(x * x, axis=-1, keepdims=True)
    y = x * lax.rsqrt(ms + EPS) * g1_ref[...]
    h = y * (1.0 + sc1) + sh1
    p = jnp.dot(h.astype(BF16), w_ref[...], preferred_element_type=F32)
    q = p[:, 0:QK_WIDTH]
    k = p[:, QK_WIDTH:2 * QK_WIDTH]
    v = p[:, 2 * QK_WIDTH:2 * QK_WIDTH + ATTN_WIDTH]

    if emit_cache:
        natural = []
        for i in range(bt):
            rows = slice(i * tn, (i + 1) * tn)
            kt = k[rows, :].T
            groups = []
            for g in range(QK_WIDTH // HEAD_DIM):
                t = kt[g * HEAD_DIM:(g + 1) * HEAD_DIM, :]
                groups.append(t * lax.rsqrt(jnp.mean(t * t, axis=0, keepdims=True) + EPS))
            kt = jnp.concatenate(groups, axis=0) * kgc_ref[...]
            k32_ref[i] = kt
            natural.append(kt.T)
            for hd in range(N_HEADS):
                v32_ref[i, pl.ds(hd, tn, stride=N_HEADS), :] = v[rows, hd * V_DIM:(hd + 1) * V_DIM]
        k = jnp.concatenate(natural, axis=0)
    else:
        k2 = (k * k).astype(BF16)
        msq = jnp.concatenate([jnp.dot(k2[:, c:c + MEAN_TILE], gm_ref[...], preferred_element_type=F32)
                               for c in range(0, QK_WIDTH, MEAN_TILE)], axis=-1)
        k = k * lax.rsqrt(msq + EPS) * kg_ref[...]

    if rope:
        lane = lax.broadcasted_iota(jnp.int32, (k.shape[0], V_DIM), 1)
        first = (lane % ROPE_AXIS_DIM) < (ROPE_AXIS_DIM // 2)
        half = ROPE_AXIS_DIM // 2
        cols = [k[:, hd * V_DIM:(hd + 1) * V_DIM] for hd in range(N_HEADS)]
        sw = jnp.concatenate([jnp.where(first, pltpu.roll(t, V_DIM - half, 1), pltpu.roll(t, half, 1))
                              for t in cols], axis=-1)
        k = k * cos_ref[...] + sw * sin_ref[...]

    blk = lambda t: t.reshape(bt, tn, t.shape[-1])
    q_ref[...] = blk(q)
    k_ref[...] = blk(k.astype(BF16))
    v_ref[...] = blk(v.astype(BF16))

    f0 = 2 * QK_WIDTH + ATTN_WIDTH
    for g in range(N_FOUR_GROUPS):
        fg = p[:, f0 + g * FOUR_GROUP:f0 + (g + 1) * FOUR_GROUP].astype(BF16)
        u = jnp.dot(fg, wcs_ref[g], preferred_element_type=F32)
        uc_ref[:, :, g * FOUR_GROUP:(g + 1) * FOUR_GROUP] = blk(u[:, 0:FOUR_GROUP].astype(BF16))
        us_ref[:, :, g * FOUR_GROUP:(g + 1) * FOUR_GROUP] = blk(u[:, FOUR_GROUP:].astype(BF16))


def _in_projection(x3, mod, mod_row0, per_batch_mod, norm_g, w_in, kg, gm, wcs, rope_tabs, emit_cache):
    nb, n, _ = x3.shape
    tn = min(PROJ_TILE, n)
    bt = PROJ_TILE // tn
    assert per_batch_mod is False or bt == 1
    rope = rope_tabs is not None
    tile = lambda width: pl.BlockSpec((bt, tn, width), lambda t, b: (b, t, 0))
    in_specs = [tile(D_MODEL),
                _resident((MOD_ROWS, MOD_HEAD_COLS)),
                _resident((1, D_MODEL)),
                _resident((D_MODEL, PROJ_WIDTH)),
                _resident((1, QK_WIDTH)),
                _resident((MEAN_TILE, MEAN_TILE)),
                _resident((N_FOUR_GROUPS, FOUR_GROUP, 2 * FOUR_GROUP))]
    args = [x3, mod, norm_g, w_in, kg, gm, wcs]
    if rope:
        in_specs += [pl.BlockSpec((tn, QK_WIDTH), lambda t, b: (t, 0))] * 2
        args += list(rope_tabs)
    if emit_cache:
        in_specs += [_resident((QK_WIDTH, 1))]
        args += [kg.reshape(QK_WIDTH, 1)]
    out_shape = [jax.ShapeDtypeStruct((nb, n, QK_WIDTH), F32)]
    out_shape += [jax.ShapeDtypeStruct((nb, n, QK_WIDTH), BF16)] * 4
    out_specs = [tile(QK_WIDTH)] * 5
    if emit_cache:
        assert tn == n
        out_shape += [jax.ShapeDtypeStruct((nb, QK_WIDTH, n), F32),
                      jax.ShapeDtypeStruct((nb, n * N_HEADS, V_DIM), F32)]
        out_specs += [pl.BlockSpec((bt, QK_WIDTH, n), lambda t, b: (b, 0, 0)),
                      pl.BlockSpec((bt, n * N_HEADS, V_DIM), lambda t, b: (b, 0, 0))]
    return pl.pallas_call(
        functools.partial(_inproj_kernel, rope=rope, emit_cache=emit_cache, bt=bt, tn=tn,
                          mod_row0=mod_row0, per_batch_mod=per_batch_mod),
        grid=(n // tn, nb // bt),
        in_specs=in_specs,
        out_specs=out_specs,
        out_shape=out_shape,
        compiler_params=_params(2),
        name="in_projection_rope" if rope else "in_projection",
    )(*args)


ATTN_Q_TILE = 512
KEY_CHUNK = 1280
ONES_ROWS = 16
FOURIER_PIECE_ROWS = 128
MIRROR_BLOCK_ROWS = 256
PREFETCH_KEYS = 1024
ATTN_STEP_KEYS = 1024


def _attn_kernel(*refs, has_cache, n, past, bt, n_cast):
    it = iter(refs)
    lq1_ref, lk1_ref, lq2_ref, lk2_ref, sg_ref, qg_ref, q_ref, k_ref, v_ref = (next(it) for _ in range(9))
    if has_cache:
        ck_ref, cv_ref, rq_ref = next(it), next(it), next(it)
    pc_ref, ps_ref, perm_ref, uc_ref, us_ref = (next(it) for _ in range(5))
    cast_in = [next(it) for _ in range(n_cast)]
    if n_cast:
        cc_ref, c_ref, wm_ref, bm_ref = (next(it) for _ in range(4))
    o_ref, four_ref = next(it), next(it)
    for src, dst in zip(cast_in, [next(it) for _ in range(n_cast)]):
        dst[...] = src[...].astype(BF16)
    if n_cast:
        _modulation_block(cc_ref, c_ref, wm_ref[...].astype(BF16), bm_ref, next(it))
    kall_ref, vt_ref = next(it), next(it)
    fc_ref, fs_ref = next(it), next(it)
    nk = n + past
    half_n = n // 2
    tq = q_ref.shape[1]

    @pl.when(pl.program_id(1) == 0)
    def _per_batch_setup():
        ones = jnp.ones((ONES_ROWS, nk), BF16)
        for b in range(bt):
            if has_cache:
                kall_ref[b, 0:past, :] = ck_ref[b].T.astype(BF16)
            kall_ref[b, past:nk, :] = k_ref[b]
            for h in range(N_HEADS):
                if has_cache:
                    cvh = cv_ref[b, pl.ds(h, past, stride=N_HEADS), :]
                    vt_ref[b, h, 0:V_DIM, 0:past] = cvh.T.astype(BF16)
                vh = v_ref[b, :, h * V_DIM:(h + 1) * V_DIM].astype(F32)
                vt_ref[b, h, 0:V_DIM, past:nk] = vh.T.astype(BF16)
                vt_ref[b, h, V_DIM:V_DIM + ONES_ROWS, :] = ones
            blk = perm_ref.shape[0]
            nblk = half_n // blk
            first_row = lax.broadcasted_iota(jnp.int32, (blk, 1), 0) == 0
            for a in range(nblk):
                dst = slice(a * blk, (a + 1) * blk)
                src = slice((2 * nblk - 1 - a) * blk, (2 * nblk - a) * blk)
                rev_c = jnp.dot(perm_ref[...], uc_ref[b, src, :], preferred_element_type=F32)
                rev_s = jnp.dot(perm_ref[...], us_ref[b, src, :], preferred_element_type=F32)
                if a > 0:
                    edge = slice(n - a * blk, n - a * blk + 1)
                    rev_c = rev_c + jnp.where(first_row, uc_ref[b, edge, :].astype(F32), 0.0)
                    rev_s = rev_s + jnp.where(first_row, us_ref[b, edge, :].astype(F32), 0.0)
                fc_ref[b, dst, :] = (uc_ref[b, dst, :].astype(F32) + rev_c).astype(BF16)
                fs_ref[b, dst, :] = (us_ref[b, dst, :].astype(F32) - rev_s).astype(BF16)

    l1 = jnp.exp(jnp.sum(lq1_ref[...] * lk1_ref[...], axis=-1, keepdims=True))
    l2 = jnp.exp(jnp.sum(lq2_ref[...] * lk2_ref[...], axis=-1, keepdims=True))
    lam = l1 - l2 + LAM_INIT

    head = lambda h: slice(h * V_DIM, (h + 1) * V_DIM)
    qscale = qg_ref[...] * (LOG2E / math.sqrt(HEAD_DIM))
    zeros = jnp.zeros((HEAD_DIM, tq), BF16)
    nf = ROPE_AXIS_DIM // 2

    def prepare_q(b, h):
        qt = q_ref[b, :, head(h)].T
        halves = []
        for mhalf in range(2):
            t = qt[mhalf * HEAD_DIM:(mhalf + 1) * HEAD_DIM, :]
            t = t * lax.rsqrt(jnp.mean(t * t, axis=0, keepdims=True) + EPS) * qscale
            if has_cache:
                parts = []
                for axis in range(2):
                    x1 = t[(2 * axis) * nf:(2 * axis + 1) * nf, :]
                    x2 = t[(2 * axis + 1) * nf:(2 * axis + 2) * nf, :]
                    cos, sin = rq_ref[2 * axis], rq_ref[2 * axis + 1]
                    parts += [x1 * cos - x2 * sin, x2 * cos + x1 * sin]
                t = jnp.concatenate(parts, axis=0)
            halves.append(t.astype(BF16))
        return halves

    kc = min(KEY_CHUNK, nk)
    chunks = [slice(c * kc, (c + 1) * kc) for c in range(nk // kc)]
    chains = [(b, h, mhalf) for b in range(bt) for h in range(N_HEADS) for mhalf in range(2)]
    qts = {}

    def scores(chain):
        b, h, mhalf = chain
        if (b, h) not in qts:
            qts[b, h] = prepare_q(b, h)
        qm = jnp.concatenate([qts[b, h][0], zeros] if mhalf == 0 else [zeros, qts[b, h][1]], axis=0)

        def one(rows):
            s = jnp.dot(kall_ref[b, rows, head(h)], qm, preferred_element_type=F32)
            return s, jnp.max(s, axis=0, keepdims=True)
        return one

    def fourier_piece(b, rows):
        def emit():
            o = jnp.dot(pc_ref[rows, :], fc_ref[b], preferred_element_type=F32)
            o = o + jnp.dot(ps_ref[rows, :], fs_ref[b], preferred_element_type=F32)
            k_par = lax.broadcasted_iota(jnp.int32, (rows.stop - rows.start, 1), 0) % 2
            mid = jnp.where(k_par == 0, 1.0, -1.0) * (1.0 / math.sqrt(n))
            o = o + mid * uc_ref[b, half_n:half_n + 1, :].astype(F32)
            four_ref[b, rows, :] = o.astype(BF16)
        return emit

    fr = min(FOURIER_PIECE_ROWS, tq)
    pieces = [fourier_piece(b, slice(r * fr, (r + 1) * fr)) for b in range(bt) for r in range(tq // fr)]
    piece_every = len(chains) // len(pieces)

    depth = max(1, min(len(chains), PREFETCH_KEYS // nk))
    queue = [[scores(chain)(rows) for rows in chunks] for chain in chains[:depth]]
    acc = [None, None]
    for i, (b, h, mhalf) in enumerate(chains):
        cur = queue.pop(0)
        mx = functools.reduce(jnp.maximum, [m for _, m in cur])
        nxt = scores(chains[i + depth]) if i + depth < len(chains) else None
        issued = []
        a = None
        for c, rows in enumerate(chunks):
            if nxt is not None:
                issued.append(nxt(rows))
            e = jnp.exp2(cur[c][0] - mx).astype(BF16)
            part = jnp.dot(vt_ref[b, h, :, rows], e, preferred_element_type=F32)
            a = part if a is None else a + part
        if nxt is not None:
            queue.append(issued)
        if i % piece_every == 0:
            pieces[i // piece_every]()
        acc[mhalf] = a
        if mhalf == 1:
            a0, a1 = acc
            r0 = 1.0 / a0[V_DIM:V_DIM + 1, :]
            r1 = lam / a1[V_DIM:V_DIM + 1, :]
            o = (a0[0:V_DIM, :] * r0 - a1[0:V_DIM, :] * r1).T
            ms = jnp.mean(o * o, axis=-1, keepdims=True)
            o = o * lax.rsqrt(ms + EPS) * sg_ref[...] * (1.0 - LAM_INIT)
            o_ref[b, :, head(h)] = o.astype(BF16)


def _attention(lam_vecs, subln_g, qg_col, q, k, v, uc, us, cache_k=None, cache_v=None, side_jobs=None):
    nb, n, _ = q.shape
    tq = min(ATTN_Q_TILE, n)
    has_cache = cache_k is not None
    past = cache_k.shape[2] if has_cache else 0
    nk = n + past
    bt = max(1, min(nb, ATTN_STEP_KEYS // nk))
    qtile = pl.BlockSpec((bt, tq, QK_WIDTH), lambda b, t: (b, t, 0))
    per_batch = pl.BlockSpec((bt, n, QK_WIDTH), lambda b, t: (b, 0, 0))
    in_specs = [_resident((1, HEAD_DIM))] * 4 + [_resident((1, V_DIM)), _resident((HEAD_DIM, 1)),
                                                   qtile, per_batch, per_batch]
    args = list(lam_vecs) + [subln_g, qg_col, q, k, v]
    if has_cache:
        in_specs += [pl.BlockSpec((bt, QK_WIDTH, past), lambda b, t: (b, 0, 0)),
                     pl.BlockSpec((bt, past * N_HEADS, V_DIM), lambda b, t: (b, 0, 0)),
                     pl.BlockSpec((4, ROPE_AXIS_DIM // 2, tq), lambda b, t: (0, 0, t))]
        args += [cache_k, cache_v, jnp.asarray(_rope_tables_feature_major(n))]
    pc, ps = _position_dft(n)
    dft_rows = pl.BlockSpec((tq, n // 2), lambda b, t: (t, 0))
    mirror = min(MIRROR_BLOCK_ROWS, n // 2)
    in_specs += [dft_rows, dft_rows, _resident((mirror, mirror)), per_batch, per_batch]
    args += [jnp.asarray(pc).astype(BF16), jnp.asarray(ps).astype(BF16),
             jnp.asarray(_mirror_block(mirror)).astype(BF16), uc, us]
    nt = n // tq
    steps = (nb // bt) * nt
    out_specs = [qtile, qtile]
    out_shape = [jax.ShapeDtypeStruct((nb, n, ATTN_WIDTH), BF16), jax.ShapeDtypeStruct((nb, n, FOUR_WIDTH), BF16)]
    cast_weights = ()
    if side_jobs is not None:
        cast_weights, (c_ctx, c, w_mod, b_mod) = side_jobs
        slabs = [pl.BlockSpec((w.shape[0] // steps, w.shape[1]), lambda b, t: (b * nt + t, 0)) for w in cast_weights]
        tail = MOD_TAIL_COLS // steps
        first = MOD_HEAD_COLS // tail
        in_specs += slabs + [_resident(c_ctx.shape), _resident(c.shape),
                             pl.BlockSpec((D_MODEL, tail), lambda b, t: (0, first + b * nt + t)),
                             pl.BlockSpec((1, tail), lambda b, t: (0, first + b * nt + t))]
        args += list(cast_weights) + [c_ctx, c, w_mod, b_mod]
        out_specs += slabs + [pl.BlockSpec((MOD_ROWS, tail), lambda b, t: (0, b * nt + t))]
        out_shape += [jax.ShapeDtypeStruct(w.shape, BF16) for w in cast_weights]
        out_shape += [jax.ShapeDtypeStruct((MOD_ROWS, MOD_TAIL_COLS), F32)]
    return pl.pallas_call(
        functools.partial(_attn_kernel, has_cache=has_cache, n=n, past=past, bt=bt, n_cast=len(cast_weights)),
        grid=(nb // bt, nt),
        in_specs=in_specs,
        out_specs=out_specs,
        out_shape=out_shape,
        scratch_shapes=[pltpu.VMEM((bt, nk, QK_WIDTH), BF16),
                        pltpu.VMEM((bt, N_HEADS, V_DIM + ONES_ROWS, nk), BF16),
                        pltpu.VMEM((bt, n // 2, FOUR_WIDTH), BF16),
                        pltpu.VMEM((bt, n // 2, FOUR_WIDTH), BF16)],
        compiler_params=_params(2),
        name="diff_attention_cached" if has_cache else "diff_attention",
    )(*args)


OUT_TILE = 512
FF_CHUNK = 1024


def _out_mlp_kernel(x_ref, a_ref, f_ref, mod_ref, wo_ref, g2_ref, w1_ref, w2_ref, o_ref, *, mod_row0, per_batch_mod):
    m = _mod_row(mod_ref, mod_row0, per_batch_mod)
    g1 = m[:, 0:D_MODEL]
    sh2 = m[:, D_MODEL:2 * D_MODEL]
    sc2 = m[:, 2 * D_MODEL:3 * D_MODEL]
    g2 = m[:, 3 * D_MODEL:4 * D_MODEL]
    mix = jnp.dot(a_ref[0], wo_ref[0:ATTN_WIDTH, :], preferred_element_type=F32)
    mix = mix + jnp.dot(f_ref[0], wo_ref[ATTN_WIDTH:, :], preferred_element_type=F32)
    x1 = x_ref[0] + g1 * mix
    ms = jnp.mean(x1 * x1, axis=-1, keepdims=True)
    h = x1 * lax.rsqrt(ms + EPS) * g2_ref[...] * (1.0 + sc2) + sh2
    hb = h.astype(BF16)
    acc = jnp.zeros(x1.shape, F32)
    for c in range(D_FF // FF_CHUNK):
        t = jnp.dot(hb, w1_ref[:, c * FF_CHUNK:(c + 1) * FF_CHUNK], preferred_element_type=F32)
        t = jnp.square(jnp.maximum(t, 0.0)).astype(BF16)
        acc = acc + jnp.dot(t, w2_ref[c * FF_CHUNK:(c + 1) * FF_CHUNK, :], preferred_element_type=F32)
    o_ref[0] = x1 + g2 * acc


def _output_mlp(x3, attn, four, mod, mod_row0, per_batch_mod, w_out, norm2_g, w1, w2):
    nb, n, _ = x3.shape
    tm = OUT_TILE
    tile = lambda width: pl.BlockSpec((1, tm, width), lambda t, b: (b, t, 0))
    return pl.pallas_call(
        functools.partial(_out_mlp_kernel, mod_row0=mod_row0, per_batch_mod=per_batch_mod),
        grid=(n // tm, nb),
        in_specs=[tile(D_MODEL), tile(ATTN_WIDTH), tile(FOUR_WIDTH),
                  _resident((MOD_ROWS, MOD_TAIL_COLS)),
                  _resident((ATTN_WIDTH + FOUR_WIDTH, D_MODEL)),
                  _resident((1, D_MODEL)),
                  _resident((D_MODEL, D_FF)),
                  _resident((D_FF, D_MODEL))],
        out_specs=tile(D_MODEL),
        out_shape=jax.ShapeDtypeStruct((nb, n, D_MODEL), F32),
        compiler_params=_params(2),
        name="output_mlp",
    )(x3, attn, four, mod, w_out, norm2_g, w1, w2)


def kernel(x_prompt, x_sample, c, cache_k, cache_v, c_ctx, w_mod, b_mod, norm1_g, w_in, q_norm_g, k_norm_g,
           lambda_q1, lambda_k1, lambda_q2, lambda_k2, subln_g, w_four, w_out, norm2_g, w1, w2):
    batch, seq, _ = x_prompt.shape
    dec_batch, dec_seq, _ = x_sample.shape
    past = cache_k.shape[2]
    l = 0

    cc = c_ctx[None, :]
    bm = b_mod[l][None, :]
    mod, wcs, w_in_b = _modulation_head_and_fourier_weights(cc, c, w_mod[l], bm, w_four[l], w_in[l])

    n1 = norm1_g[l][None, :]
    n2 = norm2_g[l][None, :]
    qg_col = q_norm_g[l][:, None]
    kg = jnp.tile(k_norm_g[l], QK_WIDTH // HEAD_DIM)[None, :]
    sg = subln_g[l][None, :]
    lamv = tuple(t[l][None, :] for t in (lambda_q1, lambda_k1, lambda_q2, lambda_k2))
    gm = jnp.asarray(_head_mean_matrix()).astype(BF16)
    rope_tabs = tuple(jnp.asarray(t) for t in _rope_tables(dec_seq))

    qp, kp, vp, ucp, usp, kt32, v32 = _in_projection(x_prompt, mod, 0, False, n1, w_in_b, kg, gm, wcs, None, True)
    q, k, v, uc, us = _in_projection(x_sample, mod, 1, True, n1, w_in_b, kg, gm, wcs, rope_tabs, False)

    ck = cache_k[:, l].transpose(0, 2, 3, 4, 1).reshape(dec_batch, QK_WIDTH, past)
    cv = cache_v[:, l].reshape(dec_batch, past * N_HEADS, V_DIM)
    attn_s, four_s, w_out_b, w1_b, w2_b, mod_tail = _attention(
        lamv, sg, qg_col, q, k, v, uc, us, ck, cv, ((w_out[l], w1[l], w2[l]), (cc, c, w_mod[l], bm)))

    attn, four = _attention(lamv, sg, qg_col, qp, kp, vp, ucp, usp)
    flat = lambda t: t.reshape(1, batch * seq, t.shape[-1])
    yp = _output_mlp(flat(x_prompt), flat(attn), flat(four), mod_tail, 0, False, w_out_b, n2, w1_b, w2_b)
    y_prompt = yp.reshape(batch, seq, D_MODEL)
    new_cache_k = kt32.reshape(batch, N_HEADS, 2, HEAD_DIM, seq).transpose(0, 4, 1, 2, 3)[:, None]
    new_cache_v = v32.reshape(batch, 1, seq, N_HEADS, V_DIM)

    y_sample = _output_mlp(x_sample, attn_s, four_s, mod_tail, 1, True, w_out_b, n2, w1_b, w2_b)

    return (y_prompt, y_sample, new_cache_k, new_cache_v)
```

```python
import functools
import math

import jax
import jax.numpy as jnp
import numpy as np
from jax import lax
from jax.experimental import pallas as pl
from jax.experimental.pallas import tpu as pltpu

D_MODEL = 1024
N_HEADS = 4
HEAD_DIM = 64
V_DIM = 2 * HEAD_DIM
QK_WIDTH = N_HEADS * 2 * HEAD_DIM
ATTN_WIDTH = N_HEADS * V_DIM
N_FOUR_GROUPS = 4
FOUR_GROUP = 128
FOUR_WIDTH = N_FOUR_GROUPS * FOUR_GROUP
PROJ_WIDTH = 2 * QK_WIDTH + ATTN_WIDTH + FOUR_WIDTH
D_FF = 4 * D_MODEL
GRID_W = 64
ROPE_BASE = 10000.0
ROPE_AXIS_DIM = HEAD_DIM // 2
EPS = 1e-6
LAM_INIT = 0.8 - 0.6 * math.exp(-0.3 * 0)
LOG2E = 1.4426950408889634
MOD_ROWS = 8
MEAN_TILE = 256

F32 = jnp.float32
BF16 = jnp.bfloat16

VMEM_LIMIT_BYTES = 60 * 1024 * 1024


def _params(n_axes):
    return pltpu.CompilerParams(dimension_semantics=("arbitrary",) * n_axes,
                                vmem_limit_bytes=VMEM_LIMIT_BYTES)


def _resident(shape):
    nd = len(shape)
    return pl.BlockSpec(shape, lambda *_: (0,) * nd, pipeline_mode=pl.Buffered(1))


@functools.lru_cache(maxsize=None)
def _channel_dft():
    idx = np.arange(FOUR_GROUP)
    ang = 2.0 * np.pi * ((idx[:, None] * idx[None, :]) % FOUR_GROUP) / FOUR_GROUP
    s = 1.0 / np.sqrt(FOUR_GROUP)
    return np.stack([np.cos(ang) * s, np.sin(ang) * s]).astype(np.float32)


@functools.lru_cache(maxsize=None)
def _position_dft(n):
    idx = np.arange(n)
    ang = 2.0 * np.pi * ((idx[:, None] * idx[None, :]) % n) / n
    s = 1.0 / np.sqrt(n)
    return (np.cos(ang) * s).astype(np.float32), (-np.sin(ang) * s).astype(np.float32)


@functools.lru_cache(maxsize=None)
def _mirror_block(rows):
    p = np.zeros((rows, rows), np.float32)
    r = np.arange(1, rows)
    p[r, rows - r] = 1.0
    return p


@functools.lru_cache(maxsize=None)
def _rope_tables(n):
    rows = n // GRID_W
    row = np.repeat(np.arange(rows), GRID_W).astype(np.float64)
    col = np.tile(np.arange(GRID_W), rows).astype(np.float64)
    inv = ROPE_BASE ** (-np.arange(0, ROPE_AXIS_DIM, 2, dtype=np.float64) / ROPE_AXIS_DIM)
    d = np.arange(QK_WIDTH) % HEAD_DIM
    part = d // ROPE_AXIS_DIM
    i = d % ROPE_AXIS_DIM
    first = i < ROPE_AXIS_DIM // 2
    pos = np.where(part[None, :] == 0, row[:, None], col[:, None])
    ang = pos * inv[i % (ROPE_AXIS_DIM // 2)][None, :]
    cos = np.cos(ang)
    sin = np.where(first[None, :], -np.sin(ang), np.sin(ang))
    return cos.astype(np.float32), sin.astype(np.float32)


@functools.lru_cache(maxsize=None)
def _rope_tables_feature_major(n):
    rows = n // GRID_W
    row = np.repeat(np.arange(rows), GRID_W).astype(np.float64)
    col = np.tile(np.arange(GRID_W), rows).astype(np.float64)
    inv = ROPE_BASE ** (-np.arange(0, ROPE_AXIS_DIM, 2, dtype=np.float64) / ROPE_AXIS_DIM)
    ang_r = inv[:, None] * row[None, :]
    ang_c = inv[:, None] * col[None, :]
    return np.stack([np.cos(ang_r), np.sin(ang_r), np.cos(ang_c), np.sin(ang_c)]).astype(np.float32)


@functools.lru_cache(maxsize=None)
def _head_mean_matrix():
    g = np.kron(np.eye(MEAN_TILE // HEAD_DIM), np.ones((HEAD_DIM, HEAD_DIM))) / HEAD_DIM
    return g.astype(np.float32)


MOD_TILE = 512
MOD_HEAD_COLS = 2 * D_MODEL
MOD_TAIL_COLS = 4 * D_MODEL


def _modulation_block(cc_ref, c_ref, w, b_ref, o_ref):
    nd = c_ref.shape[0]

    def rows(cvec):
        s = cvec / (1.0 + jnp.exp(-cvec))
        return jnp.dot(s.astype(BF16), w, preferred_element_type=F32) + b_ref[...]

    o_ref[0:1, :] = rows(cc_ref[...])
    o_ref[1:1 + nd, :] = rows(c_ref[...])
    o_ref[1 + nd:, :] = jnp.zeros((MOD_ROWS - 1 - nd, o_ref.shape[1]), F32)


def _mod_kernel(cc_ref, c_ref, w_ref, b_ref, dft_ref, wf_ref, win_ref, o_ref, wcs_ref, winb_ref):
    winb_ref[...] = win_ref[...].astype(BF16)

    @pl.when(pl.program_id(0) == 0)
    def _fourier_weights():
        for g in range(N_FOUR_GROUPS):
            w = wf_ref[g]
            wc = jnp.dot(dft_ref[0], w, precision=lax.Precision.HIGHEST, preferred_element_type=F32)
            ws = jnp.dot(dft_ref[1], w, precision=lax.Precision.HIGHEST, preferred_element_type=F32)
            wcs_ref[g] = jnp.concatenate([wc, ws], axis=-1).astype(BF16)

    _modulation_block(cc_ref, c_ref, w_ref[...].astype(BF16), b_ref, o_ref)


def _modulation_head_and_fourier_weights(c_ctx, c, w_mod, b_mod, w_four, w_in):
    nd = c.shape[0]
    steps = MOD_HEAD_COLS // MOD_TILE
    dft = jnp.asarray(_channel_dft())
    slab = pl.BlockSpec((w_in.shape[0] // steps, w_in.shape[1]), lambda j: (j, 0))
    return pl.pallas_call(
        _mod_kernel,
        grid=(steps,),
        in_specs=[pl.BlockSpec((1, D_MODEL), lambda j: (0, 0)),
                  pl.BlockSpec((nd, D_MODEL), lambda j: (0, 0)),
                  pl.BlockSpec((D_MODEL, MOD_TILE), lambda j: (0, j)),
                  pl.BlockSpec((1, MOD_TILE), lambda j: (0, j)),
                  pl.BlockSpec((2, FOUR_GROUP, FOUR_GROUP), lambda j: (0, 0, 0)),
                  pl.BlockSpec((N_FOUR_GROUPS, FOUR_GROUP, FOUR_GROUP), lambda j: (0, 0, 0)),
                  slab],
        out_specs=[pl.BlockSpec((MOD_ROWS, MOD_TILE), lambda j: (0, j)),
                   pl.BlockSpec((N_FOUR_GROUPS, FOUR_GROUP, 2 * FOUR_GROUP), lambda j: (0, 0, 0)),
                   slab],
        out_shape=[jax.ShapeDtypeStruct((MOD_ROWS, MOD_HEAD_COLS), F32),
                   jax.ShapeDtypeStruct((N_FOUR_GROUPS, FOUR_GROUP, 2 * FOUR_GROUP), BF16),
                   jax.ShapeDtypeStruct(w_in.shape, BF16)],
        compiler_params=_params(1),
        name="modulation",
    )(c_ctx, c, w_mod, b_mod, dft, w_four, w_in)


PROJ_TILE = 1024


def _mod_row(mod_ref, mod_row0, per_batch_mod):
    if per_batch_mod:
        return mod_ref[pl.ds(mod_row0 + pl.program_id(1), 1), :]
    return mod_ref[mod_row0:mod_row0 + 1, :]


def _inproj_kernel(*refs, rope, emit_cache, bt, tn, mod_row0, per_batch_mod):
    it = iter(refs)
    x_ref, mod_ref, g1_ref, w_ref, kg_ref, gm_ref, wcs_ref = (next(it) for _ in range(7))
    if rope:
        rk_ref = next(it)
    kgc_ref = next(it)
    q_ref, k_ref, v_ref, uc_ref, us_ref = (next(it) for _ in range(5))
    if emit_cache:
        k32_ref, v32_ref = next(it), next(it)

    x = x_ref[...].reshape(bt * tn, D_MODEL)
    m = _mod_row(mod_ref, mod_row0, per_batch_mod)
    sh1 = m[:, 0:D_MODEL]
    sc1 = m[:, D_MODEL:2 * D_MODEL]
    ms = jnp.mean(x * x, axis=-1, keepdims=True)
    y = x * lax.rsqrt(ms + EPS) * g1_ref[...]
    h = y * (1.0 + sc1) + sh1
    p = jnp.dot(h.astype(BF16), w_ref[...], preferred_element_type=F32)
    q = p[:, 0:QK_WIDTH]
    k = p[:, QK_WIDTH:2 * QK_WIDTH]
    v = p[:, 2 * QK_WIDTH:2 * QK_WIDTH + ATTN_WIDTH]

    if emit_cache:
        natural = []
        for i in range(bt):
            rows = slice(i * tn, (i + 1) * tn)
            kt = k[rows, :].T
            groups = []
            for g in range(QK_WIDTH // HEAD_DIM):
                t = kt[g * HEAD_DIM:(g + 1) * HEAD_DIM, :]
                groups.append(t * lax.rsqrt(jnp.mean(t * t, axis=0, keepdims=True) + EPS))
            kt = jnp.concatenate(groups, axis=0) * kgc_ref[...]
            k32_ref[i] = kt
            natural.append(kt.T)
            for hd in range(N_HEADS):
                v32_ref[i, pl.ds(hd, tn, stride=N_HEADS), :] = v[rows, hd * V_DIM:(hd + 1) * V_DIM]
        k = jnp.concatenate(natural, axis=0)
    else:
        kt = k.T
        nf = ROPE_AXIS_DIM // 2
        groups = []
        for g in range(QK_WIDTH // HEAD_DIM):
            t = kt[g * HEAD_DIM:(g + 1) * HEAD_DIM, :]
            t = t * lax.rsqrt(jnp.mean(t * t, axis=0, keepdims=True) + EPS) * kgc_ref[g * HEAD_DIM:(g + 1) * HEAD_DIM, :]
            for axis in range(2):
                x1 = t[(2 * axis) * nf:(2 * axis + 1) * nf, :]
                x2 = t[(2 * axis + 1) * nf:(2 * axis + 2) * nf, :]
                cos, sin = rk_ref[2 * axis], rk_ref[2 * axis + 1]
                groups += [x1 * cos - x2 * sin, x2 * cos + x1 * sin]
        k = jnp.concatenate(groups, axis=0).T

    blk = lambda t: t.reshape(bt, tn, t.shape[-1])
    q_ref[...] = blk(q)
    k_ref[...] = blk(k.astype(BF16))
    v_ref[...] = blk(v.astype(BF16))

    f0 = 2 * QK_WIDTH + ATTN_WIDTH
    for g in range(N_FOUR_GROUPS):
        fg = p[:, f0 + g * FOUR_GROUP:f0 + (g + 1) * FOUR_GROUP].astype(BF16)
        u = jnp.dot(fg, wcs_ref[g], preferred_element_type=F32)
        uc_ref[:, :, g * FOUR_GROUP:(g + 1) * FOUR_GROUP] = blk(u[:, 0:FOUR_GROUP].astype(BF16))
        us_ref[:, :, g * FOUR_GROUP:(g + 1) * FOUR_GROUP] = blk(u[:, FOUR_GROUP:].astype(BF16))


def _in_projection(x3, mod, mod_row0, per_batch_mod, norm_g, w_in, kg, gm, wcs, rope_tabs, emit_cache):
    nb, n, _ = x3.shape
    tn = min(PROJ_TILE, n)
    bt = PROJ_TILE // tn
    assert per_batch_mod is False or bt == 1
    rope = rope_tabs is not None
    tile = lambda width: pl.BlockSpec((bt, tn, width), lambda t, b: (b, t, 0))
    in_specs = [tile(D_MODEL),
                _resident((MOD_ROWS, MOD_HEAD_COLS)),
                _resident((1, D_MODEL)),
                _resident((D_MODEL, PROJ_WIDTH)),
                _resident((1, QK_WIDTH)),
                _resident((MEAN_TILE, MEAN_TILE)),
                _resident((N_FOUR_GROUPS, FOUR_GROUP, 2 * FOUR_GROUP))]
    args = [x3, mod, norm_g, w_in, kg, gm, wcs]
    if rope:
        in_specs += [pl.BlockSpec((4, ROPE_AXIS_DIM // 2, tn), lambda t, b: (0, 0, t))]
        args += [rope_tabs]
    assert rope != emit_cache
    in_specs += [_resident((QK_WIDTH, 1))]
    args += [kg.reshape(QK_WIDTH, 1)]
    out_shape = [jax.ShapeDtypeStruct((nb, n, QK_WIDTH), F32)]
    out_shape += [jax.ShapeDtypeStruct((nb, n, QK_WIDTH), BF16)] * 4
    out_specs = [tile(QK_WIDTH)] * 5
    if emit_cache:
        assert tn == n
        out_shape += [jax.ShapeDtypeStruct((nb, QK_WIDTH, n), F32),
                      jax.ShapeDtypeStruct((nb, n * N_HEADS, V_DIM), F32)]
        out_specs += [pl.BlockSpec((bt, QK_WIDTH, n), lambda t, b: (b, 0, 0)),
                      pl.BlockSpec((bt, n * N_HEADS, V_DIM), lambda t, b: (b, 0, 0))]
    return pl.pallas_call(
        functools.partial(_inproj_kernel, rope=rope, emit_cache=emit_cache, bt=bt, tn=tn,
                          mod_row0=mod_row0, per_batch_mod=per_batch_mod),
        grid=(n // tn, nb // bt),
        in_specs=in_specs,
        out_specs=out_specs,
        out_shape=out_shape,
        compiler_params=_params(2),
        name="in_projection_rope" if rope else "in_projection",
    )(*args)


ATTN_Q_TILE = 512
KEY_CHUNK = 1280
ONES_ROWS = 16
FOURIER_PIECE_ROWS = 128
MIRROR_BLOCK_ROWS = 256
PREFETCH_KEYS = 1024
ATTN_STEP_KEYS = 1024


def _attn_kernel(*refs, has_cache, n, past, bt, n_cast):
    it = iter(refs)
    lq1_ref, lk1_ref, lq2_ref, lk2_ref, sg_ref, qg_ref, q_ref, k_ref, v_ref = (next(it) for _ in range(9))
    if has_cache:
        ck_ref, cv_ref, rq_ref = next(it), next(it), next(it)
    pc_ref, ps_ref, perm_ref, uc_ref, us_ref = (next(it) for _ in range(5))
    cast_in = [next(it) for _ in range(n_cast)]
    if n_cast:
        cc_ref, c_ref, wm_ref, bm_ref = (next(it) for _ in range(4))
    o_ref, four_ref = next(it), next(it)
    for src, dst in zip(cast_in, [next(it) for _ in range(n_cast)]):
        dst[...] = src[...].astype(BF16)
    if n_cast:
        _modulation_block(cc_ref, c_ref, wm_ref[...].astype(BF16), bm_ref, next(it))
    kall_ref, vt_ref = next(it), next(it)
    fc_ref, fs_ref = next(it), next(it)
    nk = n + past
    half_n = n // 2
    tq = q_ref.shape[1]

    @pl.when(pl.program_id(1) == 0)
    def _per_batch_setup():
        ones = jnp.ones((ONES_ROWS, nk), BF16)
        for b in range(bt):
            if has_cache:
                kall_ref[b, 0:past, :] = ck_ref[b].T.astype(BF16)
            kall_ref[b, past:nk, :] = k_ref[b]
            for h in range(N_HEADS):
                if has_cache:
                    cvh = cv_ref[b, pl.ds(h, past, stride=N_HEADS), :]
                    vt_ref[b, h, 0:V_DIM, 0:past] = cvh.T.astype(BF16)
                vh = v_ref[b, :, h * V_DIM:(h + 1) * V_DIM].astype(F32)
                vt_ref[b, h, 0:V_DIM, past:nk] = vh.T.astype(BF16)
                vt_ref[b, h, V_DIM:V_DIM + ONES_ROWS, :] = ones
            blk = perm_ref.shape[0]
            nblk = half_n // blk
            first_row = lax.broadcasted_iota(jnp.int32, (blk, 1), 0) == 0
            for a in range(nblk):
                dst = slice(a * blk, (a + 1) * blk)
                src = slice((2 * nblk - 1 - a) * blk, (2 * nblk - a) * blk)
                rev_c = jnp.dot(perm_ref[...], uc_ref[b, src, :], preferred_element_type=F32)
                rev_s = jnp.dot(perm_ref[...], us_ref[b, src, :], preferred_element_type=F32)
                if a > 0:
                    edge = slice(n - a * blk, n - a * blk + 1)
                    rev_c = rev_c + jnp.where(first_row, uc_ref[b, edge, :].astype(F32), 0.0)
                    rev_s = rev_s + jnp.where(first_row, us_ref[b, edge, :].astype(F32), 0.0)
                fc_ref[b, dst, :] = (uc_ref[b, dst, :].astype(F32) + rev_c).astype(BF16)
                fs_ref[b, dst, :] = (us_ref[b, dst, :].astype(F32) - rev_s).astype(BF16)

    l1 = jnp.exp(jnp.sum(lq1_ref[...] * lk1_ref[...], axis=-1, keepdims=True))
    l2 = jnp.exp(jnp.sum(lq2_ref[...] * lk2_ref[...], axis=-1, keepdims=True))
    lam = l1 - l2 + LAM_INIT

    head = lambda h: slice(h * V_DIM, (h + 1) * V_DIM)
    qscale = qg_ref[...] * (LOG2E / math.sqrt(HEAD_DIM))
    zeros = jnp.zeros((HEAD_DIM, tq), BF16)
    nf = ROPE_AXIS_DIM // 2

    def prepare_q(b, h):
        qt = q_ref[b, :, head(h)].T
        halves = []
        for mhalf in range(2):
            t = qt[mhalf * HEAD_DIM:(mhalf + 1) * HEAD_DIM, :]
            t = t * lax.rsqrt(jnp.mean(t * t, axis=0, keepdims=True) + EPS) * qscale
            if has_cache:
                parts = []
                for axis in range(2):
                    x1 = t[(2 * axis) * nf:(2 * axis + 1) * nf, :]
                    x2 = t[(2 * axis + 1) * nf:(2 * axis + 2) * nf, :]
                    cos, sin = rq_ref[2 * axis], rq_ref[2 * axis + 1]
                    parts += [x1 * cos - x2 * sin, x2 * cos + x1 * sin]
                t = jnp.concatenate(parts, axis=0)
            halves.append(t.astype(BF16))
        return halves

    kc = min(KEY_CHUNK, nk)
    chunks = [slice(c * kc, (c + 1) * kc) for c in range(nk // kc)]
    chains = [(b, h, mhalf) for b in range(bt) for h in range(N_HEADS) for mhalf in range(2)]
    qts = {}

    def scores(chain):
        b, h, mhalf = chain
        if (b, h) not in qts:
            qts[b, h] = prepare_q(b, h)
        qm = jnp.concatenate([qts[b, h][0], zeros] if mhalf == 0 else [zeros, qts[b, h][1]], axis=0)

        def one(rows):
            s = jnp.dot(kall_ref[b, rows, head(h)], qm, preferred_element_type=F32)
            return s, jnp.max(s, axis=0, keepdims=True)
        return one

    def fourier_piece(b, rows):
        def emit():
            o = jnp.dot(pc_ref[rows, :], fc_ref[b], preferred_element_type=F32)
            o = o + jnp.dot(ps_ref[rows, :], fs_ref[b], preferred_element_type=F32)
            k_par = lax.broadcasted_iota(jnp.int32, (rows.stop - rows.start, 1), 0) % 2
            mid = jnp.where(k_par == 0, 1.0, -1.0) * (1.0 / math.sqrt(n))
            o = o + mid * uc_ref[b, half_n:half_n + 1, :].astype(F32)
            four_ref[b, rows, :] = o.astype(BF16)
        return emit

    fr = min(FOURIER_PIECE_ROWS, tq)
    pieces = [fourier_piece(b, slice(r * fr, (r + 1) * fr)) for b in range(bt) for r in range(tq // fr)]
    piece_every = len(chains) // len(pieces)

    depth = max(1, min(len(chains), PREFETCH_KEYS // nk))
    queue = [[scores(chain)(rows) for rows in chunks] for chain in chains[:depth]]
    acc = [None, None]
    for i, (b, h, mhalf) in enumerate(chains):
        cur = queue.pop(0)
        mx = functools.reduce(jnp.maximum, [m for _, m in cur])
        nxt = scores(chains[i + depth]) if i + depth < len(chains) else None
        issued = []
        a = None
        for c, rows in enumerate(chunks):
            if nxt is not None:
                issued.append(nxt(rows))
            e = jnp.exp2(cur[c][0] - mx).astype(BF16)
            part = jnp.dot(vt_ref[b, h, :, rows], e, preferred_element_type=F32)
            a = part if a is None else a + part
        if nxt is not None:
            queue.append(issued)
        if i % piece_every == 0:
            pieces[i // piece_every]()
        acc[mhalf] = a
        if mhalf == 1:
            a0, a1 = acc
            r0 = 1.0 / a0[V_DIM:V_DIM + 1, :]
            r1 = lam / a1[V_DIM:V_DIM + 1, :]
            o = (a0[0:V_DIM, :] * r0 - a1[0:V_DIM, :] * r1).T
            ms = jnp.mean(o * o, axis=-1, keepdims=True)
            o = o * lax.rsqrt(ms + EPS) * sg_ref[...] * (1.0 - LAM_INIT)
            o_ref[b, :, head(h)] = o.astype(BF16)


def _attention(lam_vecs, subln_g, qg_col, q, k, v, uc, us, cache_k=None, cache_v=None, side_jobs=None):
    nb, n, _ = q.shape
    tq = min(ATTN_Q_TILE, n)
    has_cache = cache_k is not None
    past = cache_k.shape[2] if has_cache else 0
    nk = n + past
    bt = max(1, min(nb, ATTN_STEP_KEYS // nk))
    qtile = pl.BlockSpec((bt, tq, QK_WIDTH), lambda b, t: (b, t, 0))
    per_batch = pl.BlockSpec((bt, n, QK_WIDTH), lambda b, t: (b, 0, 0))
    in_specs = [_resident((1, HEAD_DIM))] * 4 + [_resident((1, V_DIM)), _resident((HEAD_DIM, 1)),
                                                   qtile, per_batch, per_batch]
    args = list(lam_vecs) + [subln_g, qg_col, q, k, v]
    if has_cache:
        in_specs += [pl.BlockSpec((bt, QK_WIDTH, past), lambda b, t: (b, 0, 0)),
                     pl.BlockSpec((bt, past * N_HEADS, V_DIM), lambda b, t: (b, 0, 0)),
                     pl.BlockSpec((4, ROPE_AXIS_DIM // 2, tq), lambda b, t: (0, 0, t))]
        args += [cache_k, cache_v, jnp.asarray(_rope_tables_feature_major(n))]
    pc, ps = _position_dft(n)
    dft_rows = pl.BlockSpec((tq, n // 2), lambda b, t: (t, 0))
    mirror = min(MIRROR_BLOCK_ROWS, n // 2)
    in_specs += [dft_rows, dft_rows, _resident((mirror, mirror)), per_batch, per_batch]
    args += [jnp.asarray(pc).astype(BF16), jnp.asarray(ps).astype(BF16),
             jnp.asarray(_mirror_block(mirror)).astype(BF16), uc, us]
    nt = n // tq
    steps = (nb // bt) * nt
    out_specs = [qtile, qtile]
    out_shape = [jax.ShapeDtypeStruct((nb, n, ATTN_WIDTH), BF16), jax.ShapeDtypeStruct((nb, n, FOUR_WIDTH), BF16)]
    cast_weights = ()
    if side_jobs is not None:
        cast_weights, (c_ctx, c, w_mod, b_mod) = side_jobs
        slabs = [pl.BlockSpec((w.shape[0] // steps, w.shape[1]), lambda b, t: (b * nt + t, 0)) for w in cast_weights]
        tail = MOD_TAIL_COLS // steps
        first = MOD_HEAD_COLS // tail
        in_specs += slabs + [_resident(c_ctx.shape), _resident(c.shape),
                             pl.BlockSpec((D_MODEL, tail), lambda b, t: (0, first + b * nt + t)),
                             pl.BlockSpec((1, tail), lambda b, t: (0, first + b * nt + t))]
        args += list(cast_weights) + [c_ctx, c, w_mod, b_mod]
        out_specs += slabs + [pl.BlockSpec((MOD_ROWS, tail), lambda b, t: (0, b * nt + t))]
        out_shape += [jax.ShapeDtypeStruct(w.shape, BF16) for w in cast_weights]
        out_shape += [jax.ShapeDtypeStruct((MOD_ROWS, MOD_TAIL_COLS), F32)]
    return pl.pallas_call(
        functools.partial(_attn_kernel, has_cache=has_cache, n=n, past=past, bt=bt, n_cast=len(cast_weights)),
        grid=(nb // bt, nt),
        in_specs=in_specs,
        out_specs=out_specs,
        out_shape=out_shape,
        scratch_shapes=[pltpu.VMEM((bt, nk, QK_WIDTH), BF16),
                        pltpu.VMEM((bt, N_HEADS, V_DIM + ONES_ROWS, nk), BF16),
                        pltpu.VMEM((bt, n // 2, FOUR_WIDTH), BF16),
                        pltpu.VMEM((bt, n // 2, FOUR_WIDTH), BF16)],
        compiler_params=_params(2),
        name="diff_attention_cached" if has_cache else "diff_attention",
    )(*args)


OUT_TILE = 512
FF_CHUNK = 1024


def _out_mlp_kernel(x_ref, a_ref, f_ref, mod_ref, wo_ref, g2_ref, w1_ref, w2_ref, o_ref, *, mod_row0, per_batch_mod):
    m = _mod_row(mod_ref, mod_row0, per_batch_mod)
    g1 = m[:, 0:D_MODEL]
    sh2 = m[:, D_MODEL:2 * D_MODEL]
    sc2 = m[:, 2 * D_MODEL:3 * D_MODEL]
    g2 = m[:, 3 * D_MODEL:4 * D_MODEL]
    mix = jnp.dot(a_ref[0], wo_ref[0:ATTN_WIDTH, :], preferred_element_type=F32)
    mix = mix + jnp.dot(f_ref[0], wo_ref[ATTN_WIDTH:, :], preferred_element_type=F32)
    x1 = x_ref[0] + g1 * mix
    ms = jnp.mean(x1 * x1, axis=-1, keepdims=True)
    h = x1 * lax.rsqrt(ms + EPS) * g2_ref[...] * (1.0 + sc2) + sh2
    hb = h.astype(BF16)
    acc = jnp.zeros(x1.shape, F32)
    for c in range(D_FF // FF_CHUNK):
        t = jnp.dot(hb, w1_ref[:, c * FF_CHUNK:(c + 1) * FF_CHUNK], preferred_element_type=F32)
        t = jnp.square(jnp.maximum(t, 0.0)).astype(BF16)
        acc = acc + jnp.dot(t, w2_ref[c * FF_CHUNK:(c + 1) * FF_CHUNK, :], preferred_element_type=F32)
    o_ref[0] = x1 + g2 * acc


def _output_mlp(x3, attn, four, mod, mod_row0, per_batch_mod, w_out, norm2_g, w1, w2):
    nb, n, _ = x3.shape
    tm = OUT_TILE
    tile = lambda width: pl.BlockSpec((1, tm, width), lambda t, b: (b, t, 0))
    return pl.pallas_call(
        functools.partial(_out_mlp_kernel, mod_row0=mod_row0, per_batch_mod=per_batch_mod),
        grid=(n // tm, nb),
        in_specs=[tile(D_MODEL), tile(ATTN_WIDTH), tile(FOUR_WIDTH),
                  _resident((MOD_ROWS, MOD_TAIL_COLS)),
                  _resident((ATTN_WIDTH + FOUR_WIDTH, D_MODEL)),
                  _resident((1, D_MODEL)),
                  _resident((D_MODEL, D_FF)),
                  _resident((D_FF, D_MODEL))],
        out_specs=tile(D_MODEL),
        out_shape=jax.ShapeDtypeStruct((nb, n, D_MODEL), F32),
        compiler_params=_params(2),
        name="output_mlp",
    )(x3, attn, four, mod, w_out, norm2_g, w1, w2)


def kernel(x_prompt, x_sample, c, cache_k, cache_v, c_ctx, w_mod, b_mod, norm1_g, w_in, q_norm_g, k_norm_g,
           lambda_q1, lambda_k1, lambda_q2, lambda_k2, subln_g, w_four, w_out, norm2_g, w1, w2):
    batch, seq, _ = x_prompt.shape
    dec_batch, dec_seq, _ = x_sample.shape
    past = cache_k.shape[2]
    l = 0

    cc = c_ctx[None, :]
    bm = b_mod[l][None, :]
    mod, wcs, w_in_b = _modulation_head_and_fourier_weights(cc, c, w_mod[l], bm, w_four[l], w_in[l])

    n1 = norm1_g[l][None, :]
    n2 = norm2_g[l][None, :]
    qg_col = q_norm_g[l][:, None]
    kg = jnp.tile(k_norm_g[l], QK_WIDTH // HEAD_DIM)[None, :]
    sg = subln_g[l][None, :]
    lamv = tuple(t[l][None, :] for t in (lambda_q1, lambda_k1, lambda_q2, lambda_k2))
    gm = jnp.asarray(_head_mean_matrix()).astype(BF16)
    rope_tabs = jnp.asarray(_rope_tables_feature_major(dec_seq))

    qp, kp, vp, ucp, usp, kt32, v32 = _in_projection(x_prompt, mod, 0, False, n1, w_in_b, kg, gm, wcs, None, True)
    q, k, v, uc, us = _in_projection(x_sample, mod, 1, True, n1, w_in_b, kg, gm, wcs, rope_tabs, False)

    ck = cache_k[:, l].transpose(0, 2, 3, 4, 1).reshape(dec_batch, QK_WIDTH, past)
    cv = cache_v[:, l].reshape(dec_batch, past * N_HEADS, V_DIM)
    attn_s, four_s, w_out_b, w1_b, w2_b, mod_tail = _attention(
        lamv, sg, qg_col, q, k, v, uc, us, ck, cv, ((w_out[l], w1[l], w2[l]), (cc, c, w_mod[l], bm)))

    attn, four = _attention(lamv, sg, qg_col, qp, kp, vp, ucp, usp)
    flat = lambda t: t.reshape(1, batch * seq, t.shape[-1])
    yp = _output_mlp(flat(x_prompt), flat(attn), flat(four), mod_tail, 0, False, w_out_b, n2, w1_b, w2_b)
    y_prompt = yp.reshape(batch, seq, D_MODEL)
    new_cache_k = kt32.reshape(batch, N_HEADS, 2, HEAD_DIM, seq).transpose(0, 4, 1, 2, 3)[:, None]
    new_cache_v = v32.reshape(batch, 1, seq, N_HEADS, V_DIM)

    y_sample = _output_mlp(x_sample, attn_s, four_s, mod_tail, 1, True, w_out_b, n2, w1_b, w2_b)

    return (y_prompt, y_sample, new_cache_k, new_cache_v)
```
